```python
import jax
import jax.numpy as jnp
from jax import lax
import numpy as np

D_MODEL = 2048
BATCH = 4
SEQ = 4096
DEPTH = 2

GRID_W = 64
CTX_LEN = 256
HEAD_DIM = 128
EPS = 1e-6
N_MOD = 9
N_BRANCH = 3
FFN_DIM = 256 * ((8 * D_MODEL // 3 + 255) // 256)
A_GROUPS = D_MODEL // (4 * HEAD_DIM)
A_CHUNK = 128
A_WIDTH = A_GROUPS * HEAD_DIM
B_HEADS = D_MODEL // (4 * HEAD_DIM)
B_DK = HEAD_DIM // 2
B_DV = HEAD_DIM
B_KW = B_HEADS * B_DK
B_VW = B_HEADS * B_DV
B_RANK = 16
B_TAU = 16.0
B_CHUNK = 64
C_HEADS = D_MODEL // (2 * HEAD_DIM)
C_KV_HEADS = C_HEADS // 4
C_QW = C_HEADS * HEAD_DIM
C_KVW = C_KV_HEADS * HEAD_DIM
C_WINDOW = 128
C_BLOCK = 128
ROPE_BASE = 10000.0
COL_BK = 0
COL_BV = COL_BK + B_KW
COL_CK = COL_BV + B_VW
COL_CV = COL_CK + C_KVW
CTX_STATE_COLS = COL_CV + C_KVW
COL_AU = CTX_STATE_COLS
COL_AV = COL_AU + A_WIDTH
COL_BQ = COL_AV + A_WIDTH
COL_BG = COL_BQ + B_KW
COL_CQ = COL_BG + B_VW
COL_GATE = COL_CQ + C_QW
IN_COLS = COL_GATE + N_BRANCH * D_MODEL

kernel_name = 'hybrid_gmlp_gla_swa_macaron_dit'


def rms_norm(x, gain):
    xf = x.astype(jnp.float32)
    y = xf * lax.rsqrt(jnp.mean(xf * xf, axis=-1, keepdims=True) + EPS)
    return (y * gain.astype(jnp.float32)).astype(x.dtype)


def layer_norm(x, gain):
    xf = x.astype(jnp.float32)
    mu = jnp.mean(xf, axis=-1, keepdims=True)
    var = jnp.mean(jnp.square(xf - mu), axis=-1, keepdims=True)
    return ((xf - mu) * lax.rsqrt(var + EPS) * gain.astype(jnp.float32)).astype(x.dtype)


def modulate(h, shift, scale):
    return h * (1 + scale) + shift


def swiglu(h, w_up, w_down):
    gate, val = jnp.split(h @ w_up, 2, axis=-1)
    return (jax.nn.silu(gate) * val) @ w_down


def cols(z, start, width):
    return z[..., start:start + width]


def heads(z, start, width, n, d):
    return cols(z, start, width).reshape(z.shape[0], z.shape[1], n, d)


def flip(t):
    return jnp.flip(t, axis=1)


def rope_axis(x, pos):
    half = x.shape[-1] // 2
    inv_freq = ROPE_BASE ** (-jnp.arange(half, dtype=jnp.float32) / half)
    ang = pos.astype(jnp.float32)[:, None] * inv_freq[None, :]
    cos = jnp.cos(ang)[:, None, :]
    sin = jnp.sin(ang)[:, None, :]
    xf = x.astype(jnp.float32)
    x1, x2 = xf[..., :half], xf[..., half:]
    return jnp.concatenate([x1 * cos - x2 * sin, x1 * sin + x2 * cos], axis=-1).astype(x.dtype)


def rope_2d(x):
    L = x.shape[1]
    rows = L // GRID_W
    row = jnp.repeat(jnp.arange(rows), GRID_W)
    col = jnp.tile(jnp.arange(GRID_W), rows)
    ax = x.shape[-1] // 2
    return jnp.concatenate([rope_axis(x[..., :ax], row), rope_axis(x[..., ax:], col)], axis=-1)


def chunk_spatial_gating(u, v, v_gain, w_s, b_s):
    B_, L, _ = u.shape
    u = jax.nn.gelu(u)
    v = layer_norm(jax.nn.gelu(v), v_gain)
    vr = v.reshape(B_, L // A_CHUNK, A_CHUNK, A_GROUPS, HEAD_DIM)
    mixed = jnp.einsum('gpq,bnqgd->bnpgd', w_s, vr) + b_s.T[None, None, :, :, None]
    return u * mixed.reshape(B_, L, A_WIDTH)


def gla_log_decay(h, w1, w2, bias):
    logit = ((h @ w1) @ w2 + bias).astype(jnp.float32)
    return (jax.nn.log_sigmoid(logit) / B_TAU).reshape(h.shape[0], h.shape[1], B_HEADS, B_DK)


def gla_chunked(q, k, v, g, s0):
    B_, L, H, _ = q.shape
    n = L // B_CHUNK

    def to_chunks(t):
        return t.reshape(B_, n, B_CHUNK, H, t.shape[-1]).transpose(1, 0, 3, 2, 4)

    qc, kc, vc, gc = to_chunks(q), to_chunks(k), to_chunks(v), to_chunks(g)
    bc = jnp.cumsum(gc, axis=3)
    b_last = bc[:, :, :, -1:, :]
    q_in = qc * jnp.exp(bc)
    k_in = kc * jnp.exp(-bc)
    k_out = kc * jnp.exp(b_last - bc)
    mask = jnp.tril(jnp.ones((B_CHUNK, B_CHUNK), dtype=bool))
    attn = jnp.where(mask, jnp.einsum('nbhtd,nbhsd->nbhts', q_in, k_in), 0.0)
    intra = jnp.einsum('nbhts,nbhsv->nbhtv', attn, vc)

    def step(s, inp):
        qi, ko, vi, bl = inp
        o = jnp.einsum('bhtd,bhdv->bhtv', qi, s)
        s = s * jnp.exp(bl[:, :, 0, :])[..., None] + jnp.einsum('bhsd,bhsv->bhdv', ko, vi)
        return s, o

    s_fin, inter = lax.scan(step, s0, (q_in, k_out, vc, b_last))
    o = (intra + inter).transpose(1, 0, 3, 2, 4).reshape(B_, L, H, v.shape[-1])
    return o, s_fin


def gla_final_state(k, v, g):
    b = jnp.cumsum(g, axis=1)
    w = jnp.exp(b[:, -1:] - b)
    return jnp.einsum('blhd,blhv->bhdv', k * w, v)


def gla_output(o, og, gain):
    o = rms_norm(o, gain).reshape(o.shape[0], o.shape[1], B_VW).astype(og.dtype)
    return o * jax.nn.silu(og)


def window_attention(q, k, v, k_ctx, v_ctx, sink):
    B_, L, H, D = q.shape
    KV = k.shape[2]
    G = H // KV
    nb = L // C_BLOCK
    scale = D ** -0.5
    qb = q.reshape(B_, nb, C_BLOCK, KV, G, D)
    pad = ((0, 0), (C_BLOCK, C_BLOCK), (0, 0), (0, 0))

    def band(t):
        tp = jnp.pad(t, pad).reshape(B_, nb + 2, C_BLOCK, KV, D)
        return jnp.concatenate([tp[:, :-2], tp[:, 1:-1], tp[:, 2:]], axis=2)

    kb, vb = band(k), band(v)
    s_loc = jnp.einsum('bnqkgd,bnjkd->bkgnqj', qb, kb).astype(jnp.float32) * scale
    qpos = jnp.arange(nb)[:, None] * C_BLOCK + jnp.arange(C_BLOCK)[None, :]
    kpos = jnp.arange(nb)[:, None] * C_BLOCK - C_BLOCK + jnp.arange(3 * C_BLOCK)[None, :]
    valid = ((jnp.abs(qpos[:, :, None] - kpos[:, None, :]) <= C_WINDOW)
             & (kpos[:, None, :] >= 0) & (kpos[:, None, :] < L))
    s_loc = jnp.where(valid, s_loc, -jnp.inf)
    s_ctx = jnp.einsum('bnqkgd,bjkd->bkgnqj', qb, k_ctx).astype(jnp.float32) * scale
    s_sink = jnp.broadcast_to(sink.astype(jnp.float32).reshape(KV, G)[None, :, :, None, None, None],
                              s_loc.shape[:-1] + (1,))
    p = jax.nn.softmax(jnp.concatenate([s_loc, s_ctx, s_sink], axis=-1), axis=-1)
    n_loc = 3 * C_BLOCK
    n_ctx = k_ctx.shape[1]
    o = (jnp.einsum('bkgnqj,bnjkd->bnqkgd', p[..., :n_loc].astype(v.dtype), vb)
         + jnp.einsum('bkgnqj,bjkd->bnqkgd', p[..., n_loc:n_loc + n_ctx].astype(v.dtype), v_ctx))
    return o.reshape(B_, L, H * D)


def context_attention(q, k, v, sink):
    B_, L, H, D = q.shape
    KV = k.shape[2]
    G = H // KV
    qg = q.reshape(B_, L, KV, G, D)
    s = jnp.einsum('bqkgd,bjkd->bkgqj', qg, k).astype(jnp.float32) * D ** -0.5
    s_sink = jnp.broadcast_to(sink.astype(jnp.float32).reshape(KV, G)[None, :, :, None, None],
                              s.shape[:-1] + (1,))
    p = jax.nn.softmax(jnp.concatenate([s, s_sink], axis=-1), axis=-1)[..., :-1]
    o = jnp.einsum('bkgqj,bjkd->bqkgd', p.astype(v.dtype), v)
    return o.reshape(B_, L, H * D)


def branch_merge(z, a, b, c, w_br_a, w_br_b, w_br_c, w_out):
    g_a, g_b, g_c = jnp.split(jax.nn.sigmoid(cols(z, COL_GATE, N_BRANCH * D_MODEL)), N_BRANCH, axis=-1)
    merged = g_a * (a @ w_br_a) + g_b * (b @ w_br_b) + g_c * (c @ w_br_c)
    return merged @ w_out


def token_mix(h, hc, need_ctx, w_in, a_v_gain, a_ws, a_bs, b_dw1, b_dw2, b_db, b_norm_g,
              c_q_gain, c_k_gain, c_sink, w_br_a, w_br_b, w_br_c, w_out):
    B_ = h.shape[0]
    z = h @ w_in
    zc = hc @ (w_in if need_ctx else w_in[:, :CTX_STATE_COLS])

    a_out = chunk_spatial_gating(cols(z, COL_AU, A_WIDTH), cols(z, COL_AV, A_WIDTH), a_v_gain, a_ws, a_bs)

    def gla_kvg(zz, hh):
        k = heads(zz, COL_BK, B_KW, B_HEADS, B_DK).astype(jnp.float32)
        v = heads(zz, COL_BV, B_VW, B_HEADS, B_DV).astype(jnp.float32)
        g_f = gla_log_decay(hh, b_dw1[0], b_dw2[0], b_db[0])
        g_b = gla_log_decay(hh, b_dw1[1], b_dw2[1], b_db[1])
        return k, v, g_f, g_b

    k, v, g_f, g_b = gla_kvg(z, h)
    q = heads(z, COL_BQ, B_KW, B_HEADS, B_DK).astype(jnp.float32) * B_DK ** -0.5
    kc, vc, gc_f, gc_b = gla_kvg(zc, hc)
    if need_ctx:
        qc = heads(zc, COL_BQ, B_KW, B_HEADS, B_DK).astype(jnp.float32) * B_DK ** -0.5
        zero = jnp.zeros((B_, B_HEADS, B_DK, B_DV), jnp.float32)
        oc_f, s_f = gla_chunked(qc, kc, vc, gc_f, zero)
        oc_b, s_b = gla_chunked(flip(qc), flip(kc), flip(vc), flip(gc_b), zero)
    else:
        s_f = gla_final_state(kc, vc, gc_f)
        s_b = gla_final_state(flip(kc), flip(vc), flip(gc_b))
    o_f, _ = gla_chunked(q, k, v, g_f, s_f)
    o_b, _ = gla_chunked(flip(q), flip(k), flip(v), flip(g_b), s_b)
    b_out = gla_output(o_f + flip(o_b), cols(z, COL_BG, B_VW), b_norm_g)

    def kv_heads(zz):
        kk = rms_norm(heads(zz, COL_CK, C_KVW, C_KV_HEADS, HEAD_DIM), c_k_gain)
        vv = heads(zz, COL_CV, C_KVW, C_KV_HEADS, HEAD_DIM)
        return kk, vv

    k_att, v_att = kv_heads(z)
    kc_att, vc_att = kv_heads(zc)
    q_att = rope_2d(rms_norm(heads(z, COL_CQ, C_QW, C_HEADS, HEAD_DIM), c_q_gain))
    c_out = window_attention(q_att, rope_2d(k_att), v_att, kc_att, vc_att, c_sink)

    out = branch_merge(z, a_out, b_out, c_out, w_br_a, w_br_b, w_br_c, w_out)
    if not need_ctx:
        return out, None
    ac_out = chunk_spatial_gating(cols(zc, COL_AU, A_WIDTH), cols(zc, COL_AV, A_WIDTH), a_v_gain, a_ws, a_bs)
    bc_out = gla_output(oc_f + flip(oc_b), cols(zc, COL_BG, B_VW), b_norm_g)
    qc_att = rms_norm(heads(zc, COL_CQ, C_QW, C_HEADS, HEAD_DIM), c_q_gain)
    cc_out = context_attention(qc_att, kc_att, vc_att, c_sink)
    out_c = branch_merge(zc, ac_out, bc_out, cc_out, w_br_a, w_br_b, w_br_c, w_out)
    return out, out_c


def setup_inputs(seed: int = 0) -> dict:
    key = jax.random.key(seed)
    ks = jax.random.split(key, 24)
    D = D_MODEL

    def nrm(k, shape, scale):
        return jax.random.normal(k, shape, jnp.float32) * scale

    return {
        'x': nrm(ks[0], (BATCH, SEQ, D), 1.0),
        'c': nrm(ks[1], (BATCH, D), 1.0),
        'ctx': nrm(ks[2], (BATCH, CTX_LEN, D), 1.0),
        'c_ctx': nrm(ks[3], (D,), 1.0),
        'w_ada': nrm(ks[4], (DEPTH, D, N_MOD * D), 0.5 * D ** -0.5),
        'b_ada': nrm(ks[5], (DEPTH, N_MOD * D), 0.02),
        'norm_g': 1.0 + nrm(ks[6], (DEPTH, 3, D), 0.05),
        'w_ffn_up': nrm(ks[7], (DEPTH, 2, D, 2 * FFN_DIM), D ** -0.5),
        'w_ffn_down': nrm(ks[8], (DEPTH, 2, FFN_DIM, D), FFN_DIM ** -0.5),
        'w_in': nrm(ks[9], (DEPTH, D, IN_COLS), D ** -0.5),
        'a_v_gain': 1.0 + nrm(ks[10], (DEPTH, A_WIDTH), 0.05),
        'a_ws': nrm(ks[11], (DEPTH, A_GROUPS, A_CHUNK, A_CHUNK), A_CHUNK ** -0.5),
        'a_bs': 1.0 + nrm(ks[12], (DEPTH, A_GROUPS, A_CHUNK), 0.05),
        'b_decay_w1': nrm(ks[13], (DEPTH, 2, D, B_RANK), D ** -0.5),
        'b_decay_w2': nrm(ks[14], (DEPTH, 2, B_RANK, B_KW), B_RANK ** -0.5),
        'b_decay_b': nrm(ks[15], (DEPTH, 2, B_KW), 0.1),
        'b_norm_g': 1.0 + nrm(ks[16], (DEPTH, B_DV), 0.05),
        'c_q_gain': 1.0 + nrm(ks[17], (DEPTH, HEAD_DIM), 0.05),
        'c_k_gain': 1.0 + nrm(ks[18], (DEPTH, HEAD_DIM), 0.05),
        'c_sink': nrm(ks[19], (DEPTH, C_HEADS), 0.5),
        'w_br_a': nrm(ks[20], (DEPTH, A_WIDTH, D), A_WIDTH ** -0.5),
        'w_br_b': nrm(ks[21], (DEPTH, B_VW, D), B_VW ** -0.5),
        'w_br_c': nrm(ks[22], (DEPTH, C_QW, D), C_QW ** -0.5),
        'w_out': nrm(ks[23], (DEPTH, D, D), D ** -0.5),
    }


def reference(x, c, ctx, c_ctx, w_ada, b_ada, norm_g, w_ffn_up, w_ffn_down, w_in, a_v_gain, a_ws, a_bs,
              b_decay_w1, b_decay_w2, b_decay_b, b_norm_g, c_q_gain, c_k_gain, c_sink,
              w_br_a, w_br_b, w_br_c, w_out):
    D = D_MODEL
    c_act = jax.nn.silu(c)
    cc_act = jax.nn.silu(c_ctx)
    xc = ctx
    for l in range(DEPTH):
        last = l == DEPTH - 1
        m = jnp.split((c_act @ w_ada[l] + b_ada[l])[:, None, :], N_MOD, axis=-1)
        n_c = 5 if last else N_MOD
        mc = jnp.split(cc_act @ w_ada[l][:, :n_c * D] + b_ada[l][:n_c * D], n_c)
        x = x + 0.5 * m[2] * swiglu(modulate(rms_norm(x, norm_g[l, 0]), m[0], m[1]), w_ffn_up[l, 0], w_ffn_down[l, 0])
        xc = xc + 0.5 * mc[2] * swiglu(modulate(rms_norm(xc, norm_g[l, 0]), mc[0], mc[1]), w_ffn_up[l, 0], w_ffn_down[l, 0])
        h = modulate(rms_norm(x, norm_g[l, 1]), m[3], m[4])
        hc = modulate(rms_norm(xc, norm_g[l, 1]), mc[3], mc[4])
        mix, mix_c = token_mix(h, hc, not last, w_in[l], a_v_gain[l], a_ws[l], a_bs[l],
                               b_decay_w1[l], b_decay_w2[l], b_decay_b[l], b_norm_g[l],
                               c_q_gain[l], c_k_gain[l], c_sink[l],
                               w_br_a[l], w_br_b[l], w_br_c[l], w_out[l])
        x = x + m[5] * mix
        x = x + 0.5 * m[8] * swiglu(modulate(rms_norm(x, norm_g[l, 2]), m[6], m[7]), w_ffn_up[l, 1], w_ffn_down[l, 1])
        if not last:
            xc = xc + mc[5] * mix_c
            xc = xc + 0.5 * mc[8] * swiglu(modulate(rms_norm(xc, norm_g[l, 2]), mc[6], mc[7]), w_ffn_up[l, 1], w_ffn_down[l, 1])
    return x
```

```python
import functools

import jax
import jax.numpy as jnp
import numpy as np
from jax import lax
from jax.experimental import pallas as pl
from jax.experimental.pallas import tpu as pltpu

F32 = jnp.float32
BF16 = jnp.bfloat16

HEAD_DIM = 128
EPS = 1e-6
N_MOD = 9
GRID_W = 64
A_CHUNK = 128
B_DK = 64
B_DV = 128
B_RANK = 16
B_TAU = 16.0
B_CHUNK = 64
C_BLOCK = 128
ROPE_BASE = 10000.0
MOD_ROWS = 8
LANES = 128
NEG_BIG = -1e30
VMEM_LIMIT = 56 * 1024 * 1024

Z_BK, Z_CK, Z_BV, Z_CV, Z_BQ, Z_AU, Z_AV, Z_BG, Z_CQ, Z_GATE = (
    0, 256, 512, 1024, 1280, 1536, 2048, 2560, 3072, 4096)


def _cparams(sem):
    return pltpu.CompilerParams(dimension_semantics=sem, vmem_limit_bytes=VMEM_LIMIT)


def _dot(a, b):
    return jnp.dot(a, b, preferred_element_type=F32)


def _dot_nt(a, b):
    return lax.dot_general(a, b, (((1,), (1,)), ((), ())), preferred_element_type=F32)


def _dot_tn(a, b):
    return lax.dot_general(a, b, (((0,), (0,)), ((), ())), preferred_element_type=F32)


def _dot_f32(a, b):
    return jnp.dot(a, b, preferred_element_type=F32, precision=lax.Precision.HIGHEST)


def _silu(x):
    return x * jax.nn.sigmoid(x)


def _rms_mod(x, gain, shift, scale):
    ms = jnp.mean(x * x, axis=-1, keepdims=True)
    y = x * lax.rsqrt(ms + EPS) * gain
    return y * (1.0 + scale) + shift


def _mod_kernel(c_ref, w_ref, b_ref, o_ref):
    act = _silu(c_ref[...])
    o_ref[...] = _dot(act.astype(BF16), w_ref[...].astype(BF16)) + b_ref[...]


def _mod_table(c8, w_ada, b_ada):
    depth, d, nd = w_ada.shape
    tn = 1024
    nj = d // tn
    out = pl.pallas_call(
        _mod_kernel,
        out_shape=jax.ShapeDtypeStruct((depth, N_MOD, MOD_ROWS, d), F32),
        grid=(depth, N_MOD, nj),
        in_specs=[
            pl.BlockSpec((MOD_ROWS, d), lambda l, k, j: (0, 0)),
            pl.BlockSpec((None, d, tn), lambda l, k, j: (l, 0, k * nj + j)),
            pl.BlockSpec((None, 1, tn), lambda l, k, j: (l, 0, k * nj + j)),
        ],
        out_specs=pl.BlockSpec((None, None, MOD_ROWS, tn), lambda l, k, j: (l, k, 0, j)),
        compiler_params=_cparams(("parallel", "parallel", "parallel")),
        name="mod_table",
    )(c8, w_ada, b_ada.reshape(depth, 1, nd))
    return out.reshape(depth, N_MOD, MOD_ROWS, 1, d)


def _mod_spec(layer, k, row_fn, d):
    return pl.BlockSpec((None, None, None, 1, d), lambda i, *_: (layer, k, row_fn(i), 0, 0))


def _ffn_kernel(x_ref, g_ref, sh_ref, sc_ref, gt_ref, wg_ref, wv_ref, wd_ref, o_ref, h_ref, acc_ref):
    f = pl.program_id(1)

    @pl.when(f == 0)
    def _():
        h = _rms_mod(x_ref[...], g_ref[...], sh_ref[...], sc_ref[...])
        h_ref[...] = h.astype(BF16)
        acc_ref[...] = jnp.zeros_like(acc_ref)

    h = h_ref[...]
    gate = _dot(h, wg_ref[...])
    val = _dot(h, wv_ref[...])
    act = (_silu(gate) * val).astype(BF16)
    acc_ref[...] += _dot(act, wd_ref[...])

    @pl.when(f == pl.num_programs(1) - 1)
    def _():
        o_ref[...] = x_ref[...] + 0.5 * gt_ref[...] * acc_ref[...]


def _ffn(x, n_rows, mod, layer, k0, row_fn, gain, w_up, w_down, tm, tf):
    d = x.shape[1]
    ffn = w_down.shape[0]
    nf = ffn // tf
    return pl.pallas_call(
        _ffn_kernel,
        out_shape=jax.ShapeDtypeStruct((n_rows, d), F32),
        grid=(n_rows // tm, nf),
        in_specs=[
            pl.BlockSpec((tm, d), lambda i, f: (i, 0)),
            pl.BlockSpec((1, d), lambda i, f: (0, 0)),
            _mod_spec(layer, k0, row_fn, d),
            _mod_spec(layer, k0 + 1, row_fn, d),
            _mod_spec(layer, k0 + 2, row_fn, d),
            pl.BlockSpec((d, tf), lambda i, f: (0, f)),
            pl.BlockSpec((d, tf), lambda i, f: (0, nf + f)),
            pl.BlockSpec((tf, d), lambda i, f: (f, 0)),
        ],
        out_specs=pl.BlockSpec((tm, d), lambda i, f: (i, 0)),
        scratch_shapes=[pltpu.VMEM((tm, d), BF16), pltpu.VMEM((tm, d), F32)],
        compiler_params=_cparams(("parallel", "arbitrary")),
        name="ffn",
    )(x, gain, mod, mod, mod, w_up, w_up, w_down)


def _inproj_kernel(x_ref, g_ref, sh_ref, sc_ref, w_ref, w1_ref, z_ref, r_ref, h_ref):
    n = pl.program_id(1)

    @pl.when(n == 0)
    def _():
        h = _rms_mod(x_ref[...], g_ref[...], sh_ref[...], sc_ref[...]).astype(BF16)
        h_ref[...] = h
        r_ref[...] = _dot(h, w1_ref[...])

    z_ref[...] = _dot(h_ref[...], w_ref[...]).astype(z_ref.dtype)


def _inproj(x, mod, layer, row_fn, gain, w_in, w1, tm, tn):
    n_rows, d = x.shape
    ncols = w_in.shape[1]
    return pl.pallas_call(
        _inproj_kernel,
        out_shape=(jax.ShapeDtypeStruct((n_rows, ncols), F32),
                   jax.ShapeDtypeStruct((n_rows, LANES), F32)),
        grid=(n_rows // tm, ncols // tn),
        in_specs=[
            pl.BlockSpec((tm, d), lambda i, n: (i, 0)),
            pl.BlockSpec((1, d), lambda i, n: (0, 0)),
            _mod_spec(layer, 3, row_fn, d),
            _mod_spec(layer, 4, row_fn, d),
            pl.BlockSpec((d, tn), lambda i, n: (0, n)),
            pl.BlockSpec((d, LANES), lambda i, n: (0, 0)),
        ],
        out_specs=(pl.BlockSpec((tm, tn), lambda i, n: (i, n)),
                   pl.BlockSpec((tm, LANES), lambda i, n: (i, 0))),
        scratch_shapes=[pltpu.VMEM((tm, d), BF16)],
        compiler_params=_cparams(("parallel", "arbitrary")),
        name="inproj",
    )(x, gain, mod, mod, w_in, w1)


def _gmlp_kernel(u_ref, v_ref, gain_ref, ws_ref, bs_ref, o_ref):
    tg, width = u_ref.shape
    groups = width // HEAD_DIM
    u = jax.nn.gelu(u_ref[...].astype(F32))
    v = jax.nn.gelu(v_ref[...].astype(F32))
    mu = jnp.mean(v, axis=-1, keepdims=True)
    vc = v - mu
    var = jnp.mean(vc * vc, axis=-1, keepdims=True)
    vn = (vc * lax.rsqrt(var + EPS) * gain_ref[...]).astype(BF16)
    for c in range(tg // A_CHUNK):
        rows = slice(c * A_CHUNK, (c + 1) * A_CHUNK)
        for g in range(groups):
            cols = slice(g * HEAD_DIM, (g + 1) * HEAD_DIM)
            mixed = _dot(ws_ref[g], vn[rows, cols]) + bs_ref[g]
            o_ref[rows, cols] = (u[rows, cols] * mixed).astype(o_ref.dtype)


def _gmlp(z, n_rows, a_v_gain, a_ws, a_bs, tg):
    groups = a_ws.shape[0]
    width = groups * HEAD_DIM
    bs_full = jnp.broadcast_to(a_bs[:, :, None], (groups, A_CHUNK, HEAD_DIM)).astype(F32)
    return pl.pallas_call(
        _gmlp_kernel,
        out_shape=jax.ShapeDtypeStruct((n_rows, width), BF16),
        grid=(n_rows // tg,),
        in_specs=[
            pl.BlockSpec((tg, width), lambda i: (i, Z_AU // width)),
            pl.BlockSpec((tg, width), lambda i: (i, Z_AV // width)),
            pl.BlockSpec((1, width), lambda i: (0, 0)),
            pl.BlockSpec((groups, A_CHUNK, A_CHUNK), lambda i: (0, 0, 0)),
            pl.BlockSpec((groups, A_CHUNK, HEAD_DIM), lambda i: (0, 0, 0)),
        ],
        out_specs=pl.BlockSpec((tg, width), lambda i: (i, 0)),
        compiler_params=_cparams(("parallel",)),
        name="gmlp",
    )(z, z, a_v_gain.reshape(1, width), a_ws.astype(BF16), bs_full)


def _gla_block(k_ref, v_ref, q_ref, r_ref, w2, bias, o_ref, st_ref, reverse):
    tb = k_ref.shape[0]
    heads = k_ref.shape[1] // B_DK
    nchunk = tb // B_CHUNK
    logit = _dot_f32(r_ref[...], w2) + bias
    g = (jnp.minimum(logit, 0.0) - jnp.log1p(jnp.exp(-jnp.abs(logit)))) * (1.0 / B_TAU)
    t_idx = lax.broadcasted_iota(jnp.int32, (B_CHUNK, B_CHUNK), 0)
    s_idx = lax.broadcasted_iota(jnp.int32, (B_CHUNK, B_CHUNK), 1)
    keep = (s_idx >= t_idx) if reverse else (s_idx <= t_idx)
    tri = jnp.where(keep, 1.0, 0.0).astype(F32)
    order = range(nchunk - 1, -1, -1) if reverse else range(nchunk)
    for c in order:
        rows = slice(c * B_CHUNK, (c + 1) * B_CHUNK)
        gc = g[rows, :]
        bc = _dot_f32(tri, gc)
        tot = bc[0:1, :] if reverse else bc[B_CHUNK - 1:B_CHUNK, :]
        kc = k_ref[rows, :].astype(F32)
        q_in = q_ref[rows, :].astype(F32) * jnp.exp(bc) * (B_DK ** -0.5)
        k_in = kc * jnp.exp(-bc)
        k_out = kc * jnp.exp(tot - bc)
        dec = jnp.exp(tot)
        vc = v_ref[rows, :].astype(BF16)
        for h in range(heads):
            kcols = slice(h * B_DK, (h + 1) * B_DK)
            vcols = slice(h * B_DV, (h + 1) * B_DV)
            qh = q_in[:, kcols].astype(BF16)
            attn = jnp.where(keep, _dot_nt(qh, k_in[:, kcols].astype(BF16)), 0.0)
            st = st_ref[h]
            o = _dot(attn.astype(BF16), vc[:, vcols]) + _dot_nt(qh, st.astype(BF16))
            o_ref[rows, vcols] = o
            st_ref[h] = st * dec[:, kcols] + _dot_tn(vc[:, vcols], k_out[:, kcols].astype(BF16))


def _gla_kernel(kf_ref, vf_ref, qf_ref, rf_ref, kb_ref, vb_ref, qb_ref, rb_ref, w2_ref, b_ref,
                of_ref, ob_ref, sf_ref, sb_ref):
    @pl.when(pl.program_id(1) == 0)
    def _():
        sf_ref[...] = jnp.zeros_like(sf_ref)
        sb_ref[...] = jnp.zeros_like(sb_ref)

    _gla_block(kf_ref, vf_ref, qf_ref, rf_ref, w2_ref[0], b_ref[0], of_ref, sf_ref, False)
    _gla_block(kb_ref, vb_ref, qb_ref, rb_ref, w2_ref[1], b_ref[1], ob_ref, sb_ref, True)


def _gla(z, r, w2pad, bias, batch, seq, ctx_len, tb):
    n_rows = z.shape[0]
    heads = bias.shape[-1] // B_DK
    kw, vw = heads * B_DK, heads * B_DV
    nlat, nctx = seq // tb, ctx_len // tb
    base = batch * nlat

    def fwd(b, j):
        return jnp.where(j < nctx, base + b * nctx + j, b * nlat + (j - nctx))

    def bwd(b, j):
        return jnp.where(j < nctx, base + b * nctx + (nctx - 1 - j), b * nlat + (nlat - 1 - (j - nctx)))

    def specs(blk):
        return [pl.BlockSpec((tb, kw), lambda b, j: (blk(b, j), Z_BK // kw)),
                pl.BlockSpec((tb, vw), lambda b, j: (blk(b, j), Z_BV // vw)),
                pl.BlockSpec((tb, kw), lambda b, j: (blk(b, j), Z_BQ // kw)),
                pl.BlockSpec((tb, LANES), lambda b, j: (blk(b, j), 0))]

    return pl.pallas_call(
        _gla_kernel,
        out_shape=(jax.ShapeDtypeStruct((n_rows, vw), F32), jax.ShapeDtypeStruct((n_rows, vw), F32)),
        grid=(batch, nctx + nlat),
        in_specs=specs(fwd) + specs(bwd) + [
            pl.BlockSpec((2, LANES, kw), lambda b, j: (0, 0, 0)),
            pl.BlockSpec((2, 1, kw), lambda b, j: (0, 0, 0)),
        ],
        out_specs=(pl.BlockSpec((tb, vw), lambda b, j: (fwd(b, j), 0)),
                   pl.BlockSpec((tb, vw), lambda b, j: (bwd(b, j), 0))),
        scratch_shapes=[pltpu.VMEM((heads, B_DV, B_DK), F32), pltpu.VMEM((heads, B_DV, B_DK), F32)],
        compiler_params=_cparams(("parallel", "arbitrary")),
        name="gla",
    )(z, z, z, r, z, z, z, r, w2pad, bias)


def _rms_head(xh, gain):
    ms = jnp.mean(xh * xh, axis=-1, keepdims=True)
    return xh * lax.rsqrt(ms + EPS) * gain


def _rope(xh, cos, sin_signed):
    lane = lax.broadcasted_iota(jnp.int32, xh.shape, 1)
    first_half = jnp.bitwise_and(lane, HEAD_DIM // 4) == 0
    partner = jnp.where(first_half, pltpu.roll(xh, HEAD_DIM - HEAD_DIM // 4, 1), pltpu.roll(xh, HEAD_DIM // 4, 1))
    return xh * cos + partner * sin_signed


def _sink_column(sink_ref, kh, group, rows_per_head):
    row = lax.broadcasted_iota(jnp.int32, (group * rows_per_head, 1), 0)
    col = jnp.full((group * rows_per_head, 1), sink_ref[kh * group], F32)
    for g in range(1, group):
        col = jnp.where(row >= g * rows_per_head, sink_ref[kh * group + g], col)
    return col


def _attn_kernel(sink_ref, q_ref, kp_ref, kc_ref, kn_ref, kx_ref, vp_ref, vc_ref, vn_ref, vx_ref,
                 cq_ref, sq_ref, cp_ref, sp_ref, cn_ref, sn_ref, qg_ref, kg_ref, o_ref):
    n = pl.program_id(1)
    nb = pl.num_programs(1)
    tq = q_ref.shape[0]
    n_heads = q_ref.shape[1] // HEAD_DIM
    kv_heads = kc_ref.shape[1] // HEAD_DIM
    group = n_heads // kv_heads
    scale = HEAD_DIM ** -0.5
    qg, kg = qg_ref[...], kg_ref[...]
    cq, sq = cq_ref[...], sq_ref[...]

    row = jnp.bitwise_and(lax.broadcasted_iota(jnp.int32, (group * tq, 3 * tq), 0), tq - 1)
    col = lax.broadcasted_iota(jnp.int32, (group * tq, 3 * tq), 1)
    lo = jnp.maximum(row, jnp.where(n == 0, tq, 0))
    hi = jnp.minimum(row + 2 * tq, jnp.where(n == nb - 1, 2 * tq - 1, 3 * tq - 1))

    for kh in range(kv_heads):
        cols = slice(kh * HEAD_DIM, (kh + 1) * HEAD_DIM)
        k_loc = jnp.concatenate([
            _rope(_rms_head(kp_ref[:, cols].astype(F32), kg), cp_ref[...], sp_ref[...]),
            _rope(_rms_head(kc_ref[:, cols].astype(F32), kg), cq, sq),
            _rope(_rms_head(kn_ref[:, cols].astype(F32), kg), cn_ref[...], sn_ref[...]),
        ], axis=0).astype(BF16)
        k_ctx = _rms_head(kx_ref[:, cols].astype(F32), kg).astype(BF16)
        v_loc = jnp.concatenate([vp_ref[:, cols], vc_ref[:, cols], vn_ref[:, cols]], axis=0).astype(BF16)
        v_ctx = vx_ref[:, cols].astype(BF16)
        q = jnp.concatenate([
            _rope(_rms_head(q_ref[:, (kh * group + g) * HEAD_DIM:(kh * group + g + 1) * HEAD_DIM].astype(F32), qg), cq, sq)
            for g in range(group)], axis=0).astype(BF16)
        s_loc = _dot_nt(q, k_loc) * scale
        s_loc = jnp.where(col >= lo, s_loc, NEG_BIG)
        s_loc = jnp.where(col <= hi, s_loc, NEG_BIG)
        s_ctx = _dot_nt(q, k_ctx) * scale
        sink = _sink_column(sink_ref, kh, group, tq)
        m = jnp.maximum(jnp.maximum(jnp.max(s_loc, axis=-1, keepdims=True),
                                    jnp.max(s_ctx, axis=-1, keepdims=True)), sink)
        p_loc = jnp.exp(s_loc - m)
        p_ctx = jnp.exp(s_ctx - m)
        den = (jnp.sum(p_loc, axis=-1, keepdims=True) + jnp.sum(p_ctx, axis=-1, keepdims=True)
               + jnp.exp(sink - m))
        o = (_dot(p_loc.astype(BF16), v_loc) + _dot(p_ctx.astype(BF16), v_ctx)) / den
        for g in range(group):
            h = kh * group + g
            o_ref[:, h * HEAD_DIM:(h + 1) * HEAD_DIM] = o[g * tq:(g + 1) * tq].astype(o_ref.dtype)


def _attn(z, sink, q_gain, k_gain, cos, sin_signed, n_out, batch, seq, ctx_len):
    n_heads = sink.shape[0]
    qw = n_heads * HEAD_DIM
    kvw = Z_BV - Z_CK
    tq = C_BLOCK
    nb = seq // tq
    ctx_base = batch * seq // ctx_len

    def cur(b, n):
        return b * nb + n

    def prev(b, n):
        return b * nb + jnp.maximum(n - 1, 0)

    def nxt(b, n):
        return b * nb + jnp.minimum(n + 1, nb - 1)

    def zspec(blk, col0):
        return pl.BlockSpec((tq, kvw), lambda b, n: (blk(b, n), col0 // kvw))

    def tspec(fn):
        return pl.BlockSpec((tq, HEAD_DIM), lambda b, n: (fn(0, n), 0))

    return pl.pallas_call(
        _attn_kernel,
        out_shape=jax.ShapeDtypeStruct((n_out, qw), BF16),
        grid=(batch, nb),
        in_specs=[
            pl.BlockSpec(memory_space=pltpu.SMEM),
            pl.BlockSpec((tq, qw), lambda b, n: (cur(b, n), Z_CQ // qw)),
            zspec(prev, Z_CK), zspec(cur, Z_CK), zspec(nxt, Z_CK),
            pl.BlockSpec((ctx_len, kvw), lambda b, n: (ctx_base + b, Z_CK // kvw)),
            zspec(prev, Z_CV), zspec(cur, Z_CV), zspec(nxt, Z_CV),
            pl.BlockSpec((ctx_len, kvw), lambda b, n: (ctx_base + b, Z_CV // kvw)),
            tspec(cur), tspec(cur), tspec(prev), tspec(prev), tspec(nxt), tspec(nxt),
            pl.BlockSpec((1, HEAD_DIM), lambda b, n: (0, 0)),
            pl.BlockSpec((1, HEAD_DIM), lambda b, n: (0, 0)),
        ],
        out_specs=pl.BlockSpec((tq, qw), lambda b, n: (cur(b, n), 0)),
        compiler_params=_cparams(("parallel", "parallel")),
        name="window_attn",
    )(sink, z, z, z, z, z, z, z, z, z, cos, sin_signed, cos, sin_signed, cos, sin_signed,
      q_gain.reshape(1, HEAD_DIM), k_gain.reshape(1, HEAD_DIM))


def _ctx_attn_kernel(sink_ref, q_ref, k_ref, v_ref, qg_ref, kg_ref, c_in_ref, o_ref):
    del c_in_ref
    tq = q_ref.shape[0]
    n_heads = q_ref.shape[1] // HEAD_DIM
    kv_heads = k_ref.shape[1] // HEAD_DIM
    group = n_heads // kv_heads
    scale = HEAD_DIM ** -0.5
    qg, kg = qg_ref[...], kg_ref[...]
    for kh in range(kv_heads):
        cols = slice(kh * HEAD_DIM, (kh + 1) * HEAD_DIM)
        k = _rms_head(k_ref[:, cols].astype(F32), kg).astype(BF16)
        v = v_ref[:, cols].astype(BF16)
        q = jnp.concatenate([
            _rms_head(q_ref[:, (kh * group + g) * HEAD_DIM:(kh * group + g + 1) * HEAD_DIM].astype(F32), qg)
            for g in range(group)], axis=0).astype(BF16)
        s = _dot_nt(q, k) * scale
        sink = _sink_column(sink_ref, kh, group, tq)
        m = jnp.maximum(jnp.max(s, axis=-1, keepdims=True), sink)
        p = jnp.exp(s - m)
        den = jnp.sum(p, axis=-1, keepdims=True) + jnp.exp(sink - m)
        o = _dot(p.astype(BF16), v) / den
        for g in range(group):
            h = kh * group + g
            o_ref[:, h * HEAD_DIM:(h + 1) * HEAD_DIM] = o[g * tq:(g + 1) * tq].astype(o_ref.dtype)


def _ctx_attn(z, sink, q_gain, k_gain, c_out, batch, seq, ctx_len):
    n_heads = sink.shape[0]
    qw = n_heads * HEAD_DIM
    kvw = Z_BV - Z_CK
    ctx_base = batch * seq // ctx_len
    return pl.pallas_call(
        _ctx_attn_kernel,
        out_shape=jax.ShapeDtypeStruct(c_out.shape, c_out.dtype),
        grid=(batch,),
        in_specs=[
            pl.BlockSpec(memory_space=pltpu.SMEM),
            pl.BlockSpec((ctx_len, qw), lambda b: (ctx_base + b, Z_CQ // qw)),
            pl.BlockSpec((ctx_len, kvw), lambda b: (ctx_base + b, Z_CK // kvw)),
            pl.BlockSpec((ctx_len, kvw), lambda b: (ctx_base + b, Z_CV // kvw)),
            pl.BlockSpec((1, HEAD_DIM), lambda b: (0, 0)),
            pl.BlockSpec((1, HEAD_DIM), lambda b: (0, 0)),
            pl.BlockSpec(memory_space=pl.ANY),
        ],
        out_specs=pl.BlockSpec((ctx_len, qw), lambda b: (ctx_base + b, 0)),
        input_output_aliases={6: 0},
        compiler_params=_cparams(("parallel",)),
        name="ctx_attn",
    )(sink, z, z, z, q_gain.reshape(1, HEAD_DIM), k_gain.reshape(1, HEAD_DIM), c_out)


def _merge_kernel(x_ref, gt_ref, a_ref, of_ref, ob_ref, og_ref, c_ref, ga_ref, gb_ref, gc_ref,
                  bg_ref, wa_ref, wb_ref, wc_ref, wo_ref, o_ref):
    heads = of_ref.shape[1] // B_DV
    o = of_ref[...] + ob_ref[...]
    bn = jnp.concatenate([_rms_head(o[:, h * B_DV:(h + 1) * B_DV], bg_ref[...]) for h in range(heads)], axis=1)
    b = (bn * _silu(og_ref[...].astype(F32))).astype(BF16)
    merged = (jax.nn.sigmoid(ga_ref[...].astype(F32)) * _dot(a_ref[...], wa_ref[...])
              + jax.nn.sigmoid(gb_ref[...].astype(F32)) * _dot(b, wb_ref[...])
              + jax.nn.sigmoid(gc_ref[...].astype(F32)) * _dot(c_ref[...], wc_ref[...]))
    mix = _dot(merged.astype(BF16), wo_ref[...])
    o_ref[...] = x_ref[...] + gt_ref[...] * mix


def _merge(x, n_rows, mod, layer, row_fn, z, a, o_f, o_b, c, b_norm_g, wa, wb, wc, wo, tm):
    d = x.shape[1]
    aw, bw, cw = a.shape[1], o_f.shape[1], c.shape[1]

    def const(shape):
        return pl.BlockSpec(shape, lambda i: (0,) * len(shape), pipeline_mode=pl.Buffered(1))

    return pl.pallas_call(
        _merge_kernel,
        out_shape=jax.ShapeDtypeStruct((n_rows, d), F32),
        grid=(n_rows // tm,),
        in_specs=[
            pl.BlockSpec((tm, d), lambda i: (i, 0)),
            _mod_spec(layer, 5, row_fn, d),
            pl.BlockSpec((tm, aw), lambda i: (i, 0)),
            pl.BlockSpec((tm, bw), lambda i: (i, 0)),
            pl.BlockSpec((tm, bw), lambda i: (i, 0)),
            pl.BlockSpec((tm, bw), lambda i: (i, Z_BG // bw)),
            pl.BlockSpec((tm, cw), lambda i: (i, 0)),
            pl.BlockSpec((tm, d), lambda i: (i, Z_GATE // d)),
            pl.BlockSpec((tm, d), lambda i: (i, Z_GATE // d + 1)),
            pl.BlockSpec((tm, d), lambda i: (i, Z_GATE // d + 2)),
            pl.BlockSpec((1, B_DV), lambda i: (0, 0)),
            const((aw, d)), const((bw, d)), const((cw, d)), const((d, d)),
        ],
        out_specs=pl.BlockSpec((tm, d), lambda i: (i, 0)),
        compiler_params=_cparams(("parallel",)),
        name="merge",
    )(x, mod, a, o_f, o_b, z, c, z, z, z, b_norm_g.reshape(1, B_DV), wa, wb, wc, wo)


def _rope_tables(seq):
    half = HEAD_DIM // 4
    t = np.arange(seq)
    pos = np.stack([t // GRID_W, t % GRID_W], axis=1).astype(np.float32)
    inv_freq = jnp.asarray(ROPE_BASE, F32) ** (-jnp.arange(half, dtype=F32) / half)
    ang = jnp.asarray(pos)[:, :, None] * inv_freq[None, None, :]
    cos = jnp.cos(ang)
    sin = jnp.sin(ang)
    cos_t = jnp.concatenate([cos, cos], axis=-1).reshape(seq, HEAD_DIM)
    sin_t = jnp.concatenate([-sin, sin], axis=-1).reshape(seq, HEAD_DIM)
    return cos_t, sin_t


def _reorder_w_in(w, d):
    bkw, bvw, ckvw, aw, cqw = 256, 512, 256, 512, 1024
    o_bk = 0
    o_bv = o_bk + bkw
    o_ck = o_bv + bvw
    o_cv = o_ck + ckvw
    o_au = o_cv + ckvw
    o_av = o_au + aw
    o_bq = o_av + aw
    o_bg = o_bq + bkw
    o_cq = o_bg + bvw
    o_gate = o_cq + cqw
    seg = lambda s, width: w[:, s:s + width]
    return jnp.concatenate([seg(o_bk, bkw), seg(o_ck, ckvw), seg(o_bv, bvw), seg(o_cv, ckvw), seg(o_bq, bkw),
                            seg(o_au, aw), seg(o_av, aw), seg(o_bg, bvw), seg(o_cq, cqw), seg(o_gate, 3 * d)],
                           axis=1)


def kernel(x, c, ctx, c_ctx, w_ada, b_ada, norm_g, w_ffn_up, w_ffn_down, w_in, a_v_gain, a_ws, a_bs,
           b_decay_w1, b_decay_w2, b_decay_b, b_norm_g, c_q_gain, c_k_gain, c_sink,
           w_br_a, w_br_b, w_br_c, w_out):
    batch, seq, d = x.shape
    ctx_len = ctx.shape[1]
    depth = w_ada.shape[0]
    assert d == 2048 and batch < MOD_ROWS and seq % 256 == 0 and ctx_len == 256
    n_lat, n_ctx = batch * seq, batch * ctx_len
    n_all = n_lat + n_ctx
    tm = 512
    assert seq % tm == 0 and n_ctx % tm == 0

    def row_fn(i):
        return jnp.minimum(i * tm // seq, batch)

    def row_fn_256(i):
        return jnp.minimum(i * 256 // seq, batch)

    c8 = jnp.zeros((MOD_ROWS, d), F32).at[:batch].set(c).at[batch].set(c_ctx)
    mod = _mod_table(c8, w_ada, b_ada)
    cos_t, sin_t = _rope_tables(seq)

    xs = jnp.concatenate([x.reshape(n_lat, d), ctx.reshape(n_ctx, d)], axis=0)
    for l in range(depth):
        last = l == depth - 1
        w_up = [w_ffn_up[l, i].astype(BF16) for i in range(2)]
        w_dn = [w_ffn_down[l, i].astype(BF16) for i in range(2)]
        w_in_l = _reorder_w_in(w_in[l], d).astype(BF16)
        w1 = jnp.zeros((d, LANES), F32).at[:, :B_RANK].set(b_decay_w1[l, 0]).at[:, B_RANK:2 * B_RANK].set(
            b_decay_w1[l, 1]).astype(BF16)
        kw = b_decay_w2.shape[-1]
        w2pad = jnp.zeros((2, LANES, kw), F32).at[0, :B_RANK].set(b_decay_w2[l, 0]).at[
            1, B_RANK:2 * B_RANK].set(b_decay_w2[l, 1])
        gains = norm_g[l].reshape(3, 1, d)

        xs = _ffn(xs, n_all, mod, l, 0, row_fn, gains[0], w_up[0], w_dn[0], tm, 512)
        z, r = _inproj(xs, mod, l, row_fn, gains[1], w_in_l, w1, tm, 1024)
        n_mix = n_lat if last else n_all
        a_out = _gmlp(z, n_mix, a_v_gain[l], a_ws[l], a_bs[l], 512)
        o_f, o_b = _gla(z, r, w2pad, b_decay_b[l].reshape(2, 1, kw), batch, seq, ctx_len, 256)
        c_out = _attn(z, c_sink[l], c_q_gain[l], c_k_gain[l], cos_t, sin_t, n_mix, batch, seq, ctx_len)
        if not last:
            c_out = _ctx_attn(z, c_sink[l], c_q_gain[l], c_k_gain[l], c_out, batch, seq, ctx_len)
        xs = _merge(xs, n_mix, mod, l, row_fn_256, z, a_out, o_f, o_b, c_out, b_norm_g[l],
                    w_br_a[l].astype(BF16), w_br_b[l].astype(BF16), w_br_c[l].astype(BF16),
                    w_out[l].astype(BF16), 256)
        xs = _ffn(xs, n_mix, mod, l, 6, row_fn, gains[2], w_up[1], w_dn[1], tm, 512)
    return xs.reshape(batch, seq, d)
```

```python
import jax
import jax.numpy as jnp
import numpy as np
from jax import lax
from jax.experimental import pallas as pl
from jax.experimental.pallas import tpu as pltpu

F32 = jnp.float32
BF16 = jnp.bfloat16

HEAD_DIM = 128
EPS = 1e-6
N_MOD = 9
GRID_W = 64
A_CHUNK = 128
B_DK = 64
B_DV = 128
B_RANK = 16
B_TAU = 16.0
B_CHUNK = 64
C_BLOCK = 128
ROPE_BASE = 10000.0
MOD_ROWS = 8
LANES = 128
NEG_BIG = -1e30
VMEM_LIMIT = 56 * 1024 * 1024

Z_CQ, Z_CK, Z_CV, Z_BK, Z_BQ, Z_BV, Z_AU, Z_AV, Z_BG, Z_GATE = (
    0, 1024, 1280, 1536, 1792, 2048, 2560, 3072, 3584, 4096)
QK_COLS = Z_CV
C_KVW = Z_CV - Z_CK


def _cparams(sem):
    return pltpu.CompilerParams(dimension_semantics=sem, vmem_limit_bytes=VMEM_LIMIT)


def _dot(a, b):
    return jnp.dot(a, b, preferred_element_type=F32)


def _dot_nt(a, b):
    return lax.dot_general(a, b, (((1,), (1,)), ((), ())), preferred_element_type=F32)


def _dot_tn(a, b):
    return lax.dot_general(a, b, (((0,), (0,)), ((), ())), preferred_element_type=F32)


def _dot_f32(a, b):
    return jnp.dot(a, b, preferred_element_type=F32, precision=lax.Precision.HIGHEST)


def _silu(x):
    return x * jax.nn.sigmoid(x)


def _rms_mod(x, gain, shift, scale):
    ms = jnp.mean(x * x, axis=-1, keepdims=True)
    y = x * lax.rsqrt(ms + EPS) * gain
    return y * (1.0 + scale) + shift


def _rms_head(xh, gain):
    ms = jnp.mean(xh * xh, axis=-1, keepdims=True)
    return xh * lax.rsqrt(ms + EPS) * gain


def _mod_kernel(c_ref, w_ref, b_ref, o_ref):
    act = _silu(c_ref[...])
    o_ref[...] = _dot(act.astype(BF16), w_ref[...].astype(BF16)) + b_ref[...]


def _mod_table(c8, w_ada, b_ada):
    depth, d, nd = w_ada.shape
    tn = 1024
    nj = d // tn
    out = pl.pallas_call(
        _mod_kernel,
        out_shape=jax.ShapeDtypeStruct((depth, N_MOD, MOD_ROWS, d), F32),
        grid=(depth, N_MOD, nj),
        in_specs=[
            pl.BlockSpec((MOD_ROWS, d), lambda l, k, j: (0, 0)),
            pl.BlockSpec((None, d, tn), lambda l, k, j: (l, 0, k * nj + j)),
            pl.BlockSpec((None, 1, tn), lambda l, k, j: (l, 0, k * nj + j)),
        ],
        out_specs=pl.BlockSpec((None, None, MOD_ROWS, tn), lambda l, k, j: (l, k, 0, j)),
        compiler_params=_cparams(("parallel", "parallel", "parallel")),
        name="mod_table",
    )(c8, w_ada, b_ada.reshape(depth, 1, nd))
    return out.reshape(depth, N_MOD, MOD_ROWS, 1, d)


def _mod_spec(layer, k, row_fn, d):
    return pl.BlockSpec((None, None, None, 1, d), lambda i, *_: (layer, k, row_fn(i), 0, 0))


def _ffn_kernel(x_ref, g_ref, sh_ref, sc_ref, gt_ref, wg_ref, wv_ref, wd_ref, o_ref, h_ref, acc_ref):
    f = pl.program_id(1)

    @pl.when(f == 0)
    def _():
        h = _rms_mod(x_ref[...], g_ref[...], sh_ref[...], sc_ref[...])
        h_ref[...] = h.astype(BF16)
        acc_ref[...] = jnp.zeros_like(acc_ref)

    h = h_ref[...]
    gate = _dot(h, wg_ref[...])
    val = _dot(h, wv_ref[...])
    act = (_silu(gate) * val).astype(BF16)
    acc_ref[...] += _dot(act, wd_ref[...])

    @pl.when(f == pl.num_programs(1) - 1)
    def _():
        o_ref[...] = x_ref[...] + 0.5 * gt_ref[...] * acc_ref[...]


def _ffn(x, n_rows, mod, layer, which, row_fn, gain, w_up, w_down, tm, tf):
    d = x.shape[1]
    ffn = w_down.shape[2]
    nf = ffn // tf
    k0 = 6 * which
    return pl.pallas_call(
        _ffn_kernel,
        out_shape=jax.ShapeDtypeStruct((n_rows, d), F32),
        grid=(n_rows // tm, nf),
        in_specs=[
            pl.BlockSpec((tm, d), lambda i, f: (i, 0)),
            pl.BlockSpec((1, d), lambda i, f: (0, 0)),
            _mod_spec(layer, k0, row_fn, d),
            _mod_spec(layer, k0 + 1, row_fn, d),
            _mod_spec(layer, k0 + 2, row_fn, d),
            pl.BlockSpec((None, None, d, tf), lambda i, f: (layer, which, 0, f)),
            pl.BlockSpec((None, None, d, tf), lambda i, f: (layer, which, 0, nf + f)),
            pl.BlockSpec((None, None, tf, d), lambda i, f: (layer, which, f, 0)),
        ],
        out_specs=pl.BlockSpec((tm, d), lambda i, f: (i, 0)),
        scratch_shapes=[pltpu.VMEM((tm, d), BF16), pltpu.VMEM((tm, d), F32)],
        compiler_params=_cparams(("parallel", "arbitrary")),
        name="ffn",
    )(x, gain, mod, mod, mod, w_up, w_up, w_down)


def _inproj_kernel(x_ref, g_ref, sh_ref, sc_ref, w_ref, w1_ref, qkg_ref, cos_ref, sin_ref,
                   z_ref, r_ref, h_ref):
    n = pl.program_id(1)

    @pl.when(n == 0)
    def _():
        h = _rms_mod(x_ref[...], g_ref[...], sh_ref[...], sc_ref[...]).astype(BF16)
        h_ref[...] = h
        r_ref[...] = _dot(h, w1_ref[...])

    zt = _dot(h_ref[...], w_ref[...])

    @pl.when(n == 0)
    def _():
        cos, sin = cos_ref[...], sin_ref[...]
        for hd in range(zt.shape[1] // HEAD_DIM):
            cols = slice(hd * HEAD_DIM, (hd + 1) * HEAD_DIM)
            y = _rms_head(zt[:, cols], qkg_ref[:, cols])
            z_ref[:, cols] = (y * cos + pltpu.roll(y, HEAD_DIM // 2, 1) * sin).astype(z_ref.dtype)

    @pl.when(n != 0)
    def _():
        z_ref[...] = zt.astype(z_ref.dtype)


def _inproj(x, mod, layer, row_fn, gain, w_in, w1, qk_gain, cos_t, sin_t, n_lat, seq, tm):
    n_rows, d = x.shape
    ncols = w_in.shape[2]
    tn = QK_COLS
    rope_blocks = seq // tm

    def rope_blk(i, n):
        return jnp.where(i * tm < n_lat, (i % rope_blocks), rope_blocks)

    return pl.pallas_call(
        _inproj_kernel,
        out_shape=(jax.ShapeDtypeStruct((n_rows, ncols), BF16),
                   jax.ShapeDtypeStruct((n_rows, LANES), F32)),
        grid=(n_rows // tm, ncols // tn),
        in_specs=[
            pl.BlockSpec((tm, d), lambda i, n: (i, 0)),
            pl.BlockSpec((1, d), lambda i, n: (0, 0)),
            _mod_spec(layer, 3, row_fn, d),
            _mod_spec(layer, 4, row_fn, d),
            pl.BlockSpec((None, d, tn), lambda i, n: (layer, 0, n)),
            pl.BlockSpec((None, d, LANES), lambda i, n: (layer, 0, 0)),
            pl.BlockSpec((1, tn), lambda i, n: (0, 0)),
            pl.BlockSpec((tm, HEAD_DIM), lambda i, n: (rope_blk(i, n), 0)),
            pl.BlockSpec((tm, HEAD_DIM), lambda i, n: (rope_blk(i, n), 0)),
        ],
        out_specs=(pl.BlockSpec((tm, tn), lambda i, n: (i, n)),
                   pl.BlockSpec((tm, LANES), lambda i, n: (i, 0))),
        scratch_shapes=[pltpu.VMEM((tm, d), BF16)],
        compiler_params=_cparams(("parallel", "arbitrary")),
        name="inproj",
    )(x, gain, mod, mod, w_in, w1, qk_gain, cos_t, sin_t)


def _gmlp_kernel(u_ref, v_ref, gain_ref, ws_ref, bs_ref, o_ref):
    tg, width = u_ref.shape
    groups = width // HEAD_DIM
    u = jax.nn.gelu(u_ref[...].astype(F32))
    v = jax.nn.gelu(v_ref[...].astype(F32))
    mu = jnp.mean(v, axis=-1, keepdims=True)
    vc = v - mu
    var = jnp.mean(vc * vc, axis=-1, keepdims=True)
    vn = (vc * lax.rsqrt(var + EPS) * gain_ref[...]).astype(BF16)
    for c in range(tg // A_CHUNK):
        rows = slice(c * A_CHUNK, (c + 1) * A_CHUNK)
        for g in range(groups):
            cols = slice(g * HEAD_DIM, (g + 1) * HEAD_DIM)
            mixed = _dot(ws_ref[g], vn[rows, cols]) + bs_ref[g]
            o_ref[rows, cols] = (u[rows, cols] * mixed).astype(o_ref.dtype)


def _gmlp(z, n_rows, a_v_gain, a_ws, a_bs, tg):
    groups = a_ws.shape[0]
    width = groups * HEAD_DIM
    bs_full = jnp.broadcast_to(a_bs[:, :, None], (groups, A_CHUNK, HEAD_DIM)).astype(F32)
    return pl.pallas_call(
        _gmlp_kernel,
        out_shape=jax.ShapeDtypeStruct((n_rows, width), BF16),
        grid=(n_rows // tg,),
        in_specs=[
            pl.BlockSpec((tg, width), lambda i: (i, Z_AU // width)),
            pl.BlockSpec((tg, width), lambda i: (i, Z_AV // width)),
            pl.BlockSpec((1, width), lambda i: (0, 0)),
            pl.BlockSpec((groups, A_CHUNK, A_CHUNK), lambda i: (0, 0, 0)),
            pl.BlockSpec((groups, A_CHUNK, HEAD_DIM), lambda i: (0, 0, 0)),
        ],
        out_specs=pl.BlockSpec((tg, width), lambda i: (i, 0)),
        compiler_params=_cparams(("parallel",)),
        name="gmlp",
    )(z, z, a_v_gain.reshape(1, width), a_ws.astype(BF16), bs_full)


def _gla_dir(k_ref, v_ref, q_ref, r_ref, w2, bias, tri_ref, keep_ref, o_ref, st_ref, reverse):
    tb = k_ref.shape[0]
    heads = k_ref.shape[1] // B_DK
    nchunk = tb // B_CHUNK
    logit = _dot(r_ref[...].astype(BF16), w2) + bias
    g = (jnp.minimum(logit, 0.0) - jnp.log1p(jnp.exp(-jnp.abs(logit)))) * (1.0 / B_TAU)
    g_hi = g.astype(BF16)
    g_lo = (g - g_hi.astype(F32)).astype(BF16)
    tri = tri_ref[...]
    bc = _dot(tri, g_hi) + _dot(tri, g_lo)
    last = [c * B_CHUNK + (0 if reverse else B_CHUNK - 1) for c in range(nchunk)]
    tots = [bc[r:r + 1, :] for r in last]
    tot_rows = jnp.concatenate([jnp.broadcast_to(t, (B_CHUNK, t.shape[1])) for t in tots], axis=0)
    kf = k_ref[...].astype(F32)
    q_in = (q_ref[...].astype(F32) * jnp.exp(bc) * (B_DK ** -0.5)).astype(BF16)
    k_in = kf * jnp.exp(-bc)
    k_out = kf * jnp.exp(tot_rows - bc)
    dec = [jnp.exp(t) for t in tots]
    keep = keep_ref[...] > 0.5
    lane = lax.broadcasted_iota(jnp.int32, (tb, LANES), 1)
    zeros = jnp.zeros((B_CHUNK, LANES), BF16)
    order = range(nchunk - 1, -1, -1) if reverse else range(nchunk)
    for h in range(heads):
        slab = slice((h // 2) * LANES, (h // 2 + 1) * LANES)
        vcols = slice(h * B_DV, (h + 1) * B_DV)
        mine = (lane >= B_DK) if h % 2 else (lane < B_DK)
        q_s = q_in[:, slab]
        k_in_h = jnp.where(mine, k_in[:, slab], 0.0).astype(BF16)
        k_out_h = jnp.where(mine, k_out[:, slab], 0.0).astype(BF16)
        v_h = v_ref[:, vcols]
        attn = jnp.where(keep, _dot_nt(q_s, k_in_h), 0.0).astype(BF16)
        intra = _dot(attn, v_h)
        k4 = jnp.concatenate([
            jnp.concatenate([zeros] * c + [k_out_h[c * B_CHUNK:(c + 1) * B_CHUNK]] + [zeros] * (nchunk - 1 - c),
                            axis=0) for c in range(nchunk)], axis=1)
        u4 = _dot_tn(v_h, k4)
        st = st_ref[h]
        for c in order:
            rows = slice(c * B_CHUNK, (c + 1) * B_CHUNK)
            o_ref[rows, vcols] = intra[rows] + _dot_nt(q_s[rows], st.astype(BF16))
            st = st * dec[c][:, slab] + u4[:, c * LANES:(c + 1) * LANES]
        st_ref[h] = st


def _gla_kernel(kf_ref, vf_ref, qf_ref, rf_ref, kb_ref, vb_ref, qb_ref, rb_ref, w2_ref, b_ref,
                tri_ref, keep_ref, of_ref, ob_ref, sf_ref, sb_ref):
    @pl.when(pl.program_id(1) == 0)
    def _():
        sf_ref[...] = jnp.zeros_like(sf_ref)
        sb_ref[...] = jnp.zeros_like(sb_ref)

    _gla_dir(kf_ref, vf_ref, qf_ref, rf_ref, w2_ref[0], b_ref[0], tri_ref.at[0], keep_ref.at[0],
             of_ref, sf_ref, False)
    _gla_dir(kb_ref, vb_ref, qb_ref, rb_ref, w2_ref[1], b_ref[1], tri_ref.at[1], keep_ref.at[1],
             ob_ref, sb_ref, True)


def _chunk_triangles(tb):
    t = np.arange(tb)[:, None]
    s = np.arange(tb)[None, :]
    same = (t // B_CHUNK) == (s // B_CHUNK)
    return np.stack([same & (s <= t), same & (s >= t)]).astype(np.float32)


def _gla(z, r, w2pad, bias, batch, seq, ctx_len, tb):
    n_rows = z.shape[0]
    heads = bias.shape[-1] // B_DK
    assert heads % 2 == 0 and 2 * B_DK == LANES
    kw, vw = heads * B_DK, heads * B_DV
    nlat, nctx = seq // tb, ctx_len // tb
    base = batch * nlat
    tri = _chunk_triangles(tb)

    def fwd(b, j):
        return jnp.where(j < nctx, base + b * nctx + j, b * nlat + (j - nctx))

    def bwd(b, j):
        return jnp.where(j < nctx, base + b * nctx + (nctx - 1 - j), b * nlat + (nlat - 1 - (j - nctx)))

    def specs(blk):
        return [pl.BlockSpec((tb, kw), lambda b, j: (blk(b, j), Z_BK // kw)),
                pl.BlockSpec((tb, vw), lambda b, j: (blk(b, j), Z_BV // vw)),
                pl.BlockSpec((tb, kw), lambda b, j: (blk(b, j), Z_BQ // kw)),
                pl.BlockSpec((tb, LANES), lambda b, j: (blk(b, j), 0))]

    return pl.pallas_call(
        _gla_kernel,
        out_shape=(jax.ShapeDtypeStruct((n_rows, vw), F32), jax.ShapeDtypeStruct((n_rows, vw), F32)),
        grid=(batch, nctx + nlat),
        in_specs=specs(fwd) + specs(bwd) + [
            pl.BlockSpec((2, LANES, kw), lambda b, j: (0, 0, 0)),
            pl.BlockSpec((2, 1, kw), lambda b, j: (0, 0, 0)),
            pl.BlockSpec((2, tb, tb), lambda b, j: (0, 0, 0)),
            pl.BlockSpec((2, tb, tb), lambda b, j: (0, 0, 0)),
        ],
        out_specs=(pl.BlockSpec((tb, vw), lambda b, j: (fwd(b, j), 0)),
                   pl.BlockSpec((tb, vw), lambda b, j: (bwd(b, j), 0))),
        scratch_shapes=[pltpu.VMEM((heads, B_DV, LANES), F32), pltpu.VMEM((heads, B_DV, LANES), F32)],
        compiler_params=_cparams(("parallel", "arbitrary")),
        name="gla",
    )(z, z, z, r, z, z, z, r, w2pad.astype(BF16), bias, jnp.asarray(tri, BF16), jnp.asarray(tri, F32))


def _softmax_pv(s, bias, sink, v_all, group, tq, o_ref, head0):
    probs, dens = [], []
    for g in range(group):
        sg = s[g * tq:(g + 1) * tq]
        if bias is not None:
            sg = sg + bias
        m = jnp.maximum(jnp.max(sg, axis=-1, keepdims=True), sink[g])
        e = jnp.exp(sg - m)
        dens.append(jnp.sum(e, axis=-1, keepdims=True) + jnp.exp(sink[g] - m))
        probs.append(e.astype(BF16))
    o = _dot(jnp.concatenate(probs, axis=0), v_all)
    for g in range(group):
        h = head0 + g
        o_ref[:, h * HEAD_DIM:(h + 1) * HEAD_DIM] = (o[g * tq:(g + 1) * tq] / dens[g]).astype(o_ref.dtype)


def _attn_kernel(sink_ref, q_ref, kp_ref, kc_ref, kn_ref, kx_ref, vp_ref, vc_ref, vn_ref, vx_ref,
                 bias_ref, o_ref):
    tq = q_ref.shape[0]
    n_heads = q_ref.shape[1] // HEAD_DIM
    kv_heads = kc_ref.shape[1] // HEAD_DIM
    group = n_heads // kv_heads
    bias = bias_ref[...]
    for kh in range(kv_heads):
        cols = slice(kh * HEAD_DIM, (kh + 1) * HEAD_DIM)
        k_all = jnp.concatenate([kp_ref[:, cols], kc_ref[:, cols], kn_ref[:, cols], kx_ref[:, cols]], axis=0)
        v_all = jnp.concatenate([vp_ref[:, cols], vc_ref[:, cols], vn_ref[:, cols], vx_ref[:, cols]], axis=0)
        q = jnp.concatenate([q_ref[:, (kh * group + g) * HEAD_DIM:(kh * group + g + 1) * HEAD_DIM]
                             for g in range(group)], axis=0)
        s = _dot_nt(q, k_all)
        sink = [sink_ref[kh * group + g] for g in range(group)]
        _softmax_pv(s, bias, sink, v_all, group, tq, o_ref, kh * group)


def _window_bias(tq, ctx_len):
    i = np.arange(tq)[:, None]
    j = np.arange(3 * tq + ctx_len)[None, :]
    band = ((j >= i) & (j <= i + 2 * tq)) | (j >= 3 * tq)
    cases = [band & (j >= tq), band, band & ((j < 2 * tq) | (j >= 3 * tq))]
    return jnp.asarray(np.where(np.stack(cases), 0.0, NEG_BIG), F32)


def _attn(z, sink, n_out, batch, seq, ctx_len):
    n_heads = sink.shape[0]
    qw = n_heads * HEAD_DIM
    kvw = C_KVW
    tq = C_BLOCK
    nb = seq // tq
    assert nb >= 2
    ctx_base = batch * seq // ctx_len

    def cur(b, n):
        return b * nb + n

    def prev(b, n):
        return b * nb + jnp.maximum(n - 1, 0)

    def nxt(b, n):
        return b * nb + jnp.minimum(n + 1, nb - 1)

    def zspec(blk, col0):
        return pl.BlockSpec((tq, kvw), lambda b, n: (blk(b, n), col0 // kvw))

    def bias_case(b, n):
        return jnp.where(n == 0, 0, jnp.where(n == nb - 1, 2, 1))

    return pl.pallas_call(
        _attn_kernel,
        out_shape=jax.ShapeDtypeStruct((n_out, qw), BF16),
        grid=(batch, nb),
        in_specs=[
            pl.BlockSpec(memory_space=pltpu.SMEM),
            pl.BlockSpec((tq, qw), lambda b, n: (cur(b, n), Z_CQ // qw)),
            zspec(prev, Z_CK), zspec(cur, Z_CK), zspec(nxt, Z_CK),
            pl.BlockSpec((ctx_len, kvw), lambda b, n: (ctx_base + b, Z_CK // kvw)),
            zspec(prev, Z_CV), zspec(cur, Z_CV), zspec(nxt, Z_CV),
            pl.BlockSpec((ctx_len, kvw), lambda b, n: (ctx_base + b, Z_CV // kvw)),
            pl.BlockSpec((None, tq, 3 * tq + ctx_len), lambda b, n: (bias_case(b, n), 0, 0)),
        ],
        out_specs=pl.BlockSpec((tq, qw), lambda b, n: (cur(b, n), 0)),
        compiler_params=_cparams(("parallel", "parallel")),
        name="window_attn",
    )(sink, z, z, z, z, z, z, z, z, z, _window_bias(tq, ctx_len))


def _ctx_attn_kernel(sink_ref, q_ref, k_ref, v_ref, c_in_ref, o_ref):
    del c_in_ref
    tq = q_ref.shape[0]
    n_heads = q_ref.shape[1] // HEAD_DIM
    kv_heads = k_ref.shape[1] // HEAD_DIM
    group = n_heads // kv_heads
    for kh in range(kv_heads):
        cols = slice(kh * HEAD_DIM, (kh + 1) * HEAD_DIM)
        q = jnp.concatenate([q_ref[:, (kh * group + g) * HEAD_DIM:(kh * group + g + 1) * HEAD_DIM]
                             for g in range(group)], axis=0)
        s = _dot_nt(q, k_ref[:, cols])
        sink = [sink_ref[kh * group + g] for g in range(group)]
        _softmax_pv(s, None, sink, v_ref[:, cols], group, tq, o_ref, kh * group)


def _ctx_attn(z, sink, c_out, batch, seq, ctx_len):
    n_heads = sink.shape[0]
    qw = n_heads * HEAD_DIM
    kvw = C_KVW
    ctx_base = batch * seq // ctx_len
    return pl.pallas_call(
        _ctx_attn_kernel,
        out_shape=jax.ShapeDtypeStruct(c_out.shape, c_out.dtype),
        grid=(batch,),
        in_specs=[
            pl.BlockSpec(memory_space=pltpu.SMEM),
            pl.BlockSpec((ctx_len, qw), lambda b: (ctx_base + b, Z_CQ // qw)),
            pl.BlockSpec((ctx_len, kvw), lambda b: (ctx_base + b, Z_CK // kvw)),
            pl.BlockSpec((ctx_len, kvw), lambda b: (ctx_base + b, Z_CV // kvw)),
            pl.BlockSpec(memory_space=pl.ANY),
        ],
        out_specs=pl.BlockSpec((ctx_len, qw), lambda b: (ctx_base + b, 0)),
        input_output_aliases={4: 0},
        compiler_params=_cparams(("parallel",)),
        name="ctx_attn",
    )(sink, z, z, z, c_out)


def _merge_kernel(x_ref, gt_ref, a_ref, of_ref, ob_ref, og_ref, c_ref, ga_ref, gb_ref, gc_ref,
                  bg_ref, wa_ref, wb_ref, wc_ref, wo_ref, o_ref):
    heads = of_ref.shape[1] // B_DV
    o = of_ref[...] + ob_ref[...]
    bn = jnp.concatenate([_rms_head(o[:, h * B_DV:(h + 1) * B_DV], bg_ref[...]) for h in range(heads)], axis=1)
    b = (bn * _silu(og_ref[...].astype(F32))).astype(BF16)
    merged = (jax.nn.sigmoid(ga_ref[...].astype(F32)) * _dot(a_ref[...], wa_ref[...])
              + jax.nn.sigmoid(gb_ref[...].astype(F32)) * _dot(b, wb_ref[...])
              + jax.nn.sigmoid(gc_ref[...].astype(F32)) * _dot(c_ref[...], wc_ref[...]))
    mix = _dot(merged.astype(BF16), wo_ref[...])
    o_ref[...] = x_ref[...] + gt_ref[...] * mix


def _merge(x, n_rows, mod, layer, row_fn, z, a, o_f, o_b, c, b_norm_g, wa, wb, wc, wo, tm):
    d = x.shape[1]
    aw, bw, cw = a.shape[1], o_f.shape[1], c.shape[1]

    def const(rows):
        return pl.BlockSpec((None, rows, d), lambda i: (layer, 0, 0), pipeline_mode=pl.Buffered(1))

    return pl.pallas_call(
        _merge_kernel,
        out_shape=jax.ShapeDtypeStruct((n_rows, d), F32),
        grid=(n_rows // tm,),
        in_specs=[
            pl.BlockSpec((tm, d), lambda i: (i, 0)),
            _mod_spec(layer, 5, row_fn, d),
            pl.BlockSpec((tm, aw), lambda i: (i, 0)),
            pl.BlockSpec((tm, bw), lambda i: (i, 0)),
            pl.BlockSpec((tm, bw), lambda i: (i, 0)),
            pl.BlockSpec((tm, bw), lambda i: (i, Z_BG // bw)),
            pl.BlockSpec((tm, cw), lambda i: (i, 0)),
            pl.BlockSpec((tm, d), lambda i: (i, Z_GATE // d)),
            pl.BlockSpec((tm, d), lambda i: (i, Z_GATE // d + 1)),
            pl.BlockSpec((tm, d), lambda i: (i, Z_GATE // d + 2)),
            pl.BlockSpec((1, B_DV), lambda i: (0, 0)),
            const(aw), const(bw), const(cw), const(d),
        ],
        out_specs=pl.BlockSpec((tm, d), lambda i: (i, 0)),
        compiler_params=_cparams(("parallel",)),
        name="merge",
    )(x, mod, a, o_f, o_b, z, c, z, z, z, b_norm_g.reshape(1, B_DV), wa, wb, wc, wo)


def _rope_tables(seq, pad_rows):
    half = HEAD_DIM // 4
    t = np.arange(seq)
    pos = np.stack([t // GRID_W, t % GRID_W], axis=1).astype(np.float32)
    inv_freq = jnp.asarray(ROPE_BASE, F32) ** (-jnp.arange(half, dtype=F32) / half)
    ang = jnp.asarray(pos)[:, :, None] * inv_freq[None, None, :]
    cos = jnp.cos(ang)
    sin = jnp.sin(ang)
    cos_t = jnp.stack([cos, cos], axis=1).reshape(seq, HEAD_DIM)
    sin_t = jnp.stack([-sin, sin], axis=1).reshape(seq, HEAD_DIM)
    cos_t = jnp.concatenate([cos_t, jnp.ones((pad_rows, HEAD_DIM), F32)], axis=0)
    sin_t = jnp.concatenate([sin_t, jnp.zeros((pad_rows, HEAD_DIM), F32)], axis=0)
    return cos_t, sin_t


def _rotary_lane_order(w):
    lead = w.shape[:-1]
    nh = w.shape[-1] // HEAD_DIM
    w = w.reshape(lead + (nh, 2, 2, HEAD_DIM // 4))
    w = jnp.swapaxes(w, -3, -2)
    return w.reshape(lead + (nh * HEAD_DIM,))


def _reorder_w_in(w, d):
    bkw, bvw, ckvw, aw, cqw = 256, 512, 256, 512, 1024
    o_bk = 0
    o_bv = o_bk + bkw
    o_ck = o_bv + bvw
    o_cv = o_ck + ckvw
    o_au = o_cv + ckvw
    o_av = o_au + aw
    o_bq = o_av + aw
    o_bg = o_bq + bkw
    o_cq = o_bg + bvw
    o_gate = o_cq + cqw
    seg = lambda s, width: w[..., s:s + width]
    return jnp.concatenate([_rotary_lane_order(seg(o_cq, cqw)), _rotary_lane_order(seg(o_ck, ckvw)),
                            seg(o_cv, ckvw), seg(o_bk, bkw), seg(o_bq, bkw), seg(o_bv, bvw),
                            seg(o_au, aw), seg(o_av, aw), seg(o_bg, bvw), seg(o_gate, 3 * d)], axis=-1)


def kernel(x, c, ctx, c_ctx, w_ada, b_ada, norm_g, w_ffn_up, w_ffn_down, w_in, a_v_gain, a_ws, a_bs,
           b_decay_w1, b_decay_w2, b_decay_b, b_norm_g, c_q_gain, c_k_gain, c_sink,
           w_br_a, w_br_b, w_br_c, w_out):
    batch, seq, d = x.shape
    ctx_len = ctx.shape[1]
    depth = w_ada.shape[0]
    assert d == 2048 and batch < MOD_ROWS and seq % 256 == 0 and ctx_len == 256
    n_lat, n_ctx = batch * seq, batch * ctx_len
    n_all = n_lat + n_ctx
    tm_ffn, tm_in, tm_merge = 512, 512, 256
    for t in (tm_ffn, tm_in, tm_merge):
        assert seq % t == 0 and n_ctx % t == 0

    def row_fn(t):
        return lambda i: jnp.minimum(i * t // seq, batch)

    c8 = jnp.zeros((MOD_ROWS, d), F32).at[:batch].set(c).at[batch].set(c_ctx)
    mod = _mod_table(c8, w_ada, b_ada)
    cos_t, sin_t = _rope_tables(seq, tm_in)

    w_up = w_ffn_up.astype(BF16)
    w_dn = w_ffn_down.astype(BF16)
    w_in_r = _reorder_w_in(w_in, d).astype(BF16)
    w1 = jnp.zeros((depth, d, LANES), F32).at[:, :, :B_RANK].set(b_decay_w1[:, 0]).at[
        :, :, B_RANK:2 * B_RANK].set(b_decay_w1[:, 1]).astype(BF16)
    kw = b_decay_w2.shape[-1]
    w2pad = jnp.zeros((depth, 2, LANES, kw), F32).at[:, 0, :B_RANK].set(b_decay_w2[:, 0]).at[
        :, 1, B_RANK:2 * B_RANK].set(b_decay_w2[:, 1])
    n_qh = c_sink.shape[1]
    qk_gain = jnp.concatenate([jnp.tile(_rotary_lane_order(c_q_gain) * HEAD_DIM ** -0.5, (1, n_qh)),
                               jnp.tile(_rotary_lane_order(c_k_gain), (1, C_KVW // HEAD_DIM))], axis=1)
    wa, wb, wc, wo = (w.astype(BF16) for w in (w_br_a, w_br_b, w_br_c, w_out))

    xs = jnp.concatenate([x.reshape(n_lat, d), ctx.reshape(n_ctx, d)], axis=0)
    for l in range(depth):
        last = l == depth - 1
        gains = norm_g[l].reshape(3, 1, d)
        n_mix = n_lat if last else n_all
        xs = _ffn(xs, n_all, mod, l, 0, row_fn(tm_ffn), gains[0], w_up, w_dn, tm_ffn, 512)
        z, r = _inproj(xs, mod, l, row_fn(tm_in), gains[1], w_in_r, w1, qk_gain[l:l + 1], cos_t, sin_t,
                       n_lat, seq, tm_in)
        a_out = _gmlp(z, n_mix, a_v_gain[l], a_ws[l], a_bs[l], 512)
        o_f, o_b = _gla(z, r, w2pad[l], b_decay_b[l].reshape(2, 1, kw), batch, seq, ctx_len, 256)
        c_out = _attn(z, c_sink[l], n_mix, batch, seq, ctx_len)
        if not last:
            c_out = _ctx_attn(z, c_sink[l], c_out, batch, seq, ctx_len)
        xs = _merge(xs, n_mix, mod, l, row_fn(tm_merge), z, a_out, o_f, o_b, c_out, b_norm_g[l],
                    wa, wb, wc, wo, tm_merge)
        xs = _ffn(xs, n_mix, mod, l, 1, row_fn(tm_ffn), gains[2], w_up, w_dn, tm_ffn, 512)
    return xs.reshape(batch, seq, d)
```

```python
import functools

import jax
import jax.numpy as jnp
import numpy as np
from jax import lax
from jax.experimental import pallas as pl
from jax.experimental.pallas import tpu as pltpu

F32 = jnp.float32
BF16 = jnp.bfloat16

HEAD_DIM = 128
EPS = 1e-6
N_MOD = 9
GRID_W = 64
A_CHUNK = 128
B_DK = 64
B_DV = 128
B_RANK = 16
B_TAU = 16.0
B_CHUNK = 64
C_BLOCK = 128
ROPE_BASE = 10000.0
MOD_ROWS = 8
LANES = 128
NEG_BIG = -1e30
VMEM_LIMIT = 56 * 1024 * 1024

Z_CQ, Z_CK, Z_CV, Z_BK, Z_BQ, Z_BV, Z_AU, Z_AV, Z_BG, Z_GATE = (
    0, 1024, 1280, 1536, 1792, 2048, 2560, 3072, 3584, 4096)
QK_COLS = Z_CV
C_KVW = Z_CV - Z_CK


def _cparams(sem):
    return pltpu.CompilerParams(dimension_semantics=sem, vmem_limit_bytes=VMEM_LIMIT)


def _dot(a, b):
    return jnp.dot(a, b, preferred_element_type=F32)


def _dot_nt(a, b):
    return lax.dot_general(a, b, (((1,), (1,)), ((), ())), preferred_element_type=F32)


def _dot_tn(a, b):
    return lax.dot_general(a, b, (((0,), (0,)), ((), ())), preferred_element_type=F32)


def _dot_f32(a, b):
    return jnp.dot(a, b, preferred_element_type=F32, precision=lax.Precision.HIGHEST)


def _silu(x):
    return x * jax.nn.sigmoid(x)


def _rms_mod(x, gain, shift, scale):
    ms = jnp.mean(x * x, axis=-1, keepdims=True)
    y = x * lax.rsqrt(ms + EPS) * gain
    return y * (1.0 + scale) + shift


def _rms_head(xh, gain):
    ms = jnp.mean(xh * xh, axis=-1, keepdims=True)
    return xh * lax.rsqrt(ms + EPS) * gain


def _mod_kernel(c_ref, w_ref, b_ref, o_ref):
    act = _silu(c_ref[...])
    o_ref[...] = _dot(act.astype(BF16), w_ref[...].astype(BF16)) + b_ref[...]


def _mod_table(c8, w_ada, b_ada):
    depth, d, nd = w_ada.shape
    tn = 1024
    nj = d // tn
    out = pl.pallas_call(
        _mod_kernel,
        out_shape=jax.ShapeDtypeStruct((depth, N_MOD, MOD_ROWS, d), F32),
        grid=(depth, N_MOD, nj),
        in_specs=[
            pl.BlockSpec((MOD_ROWS, d), lambda l, k, j: (0, 0)),
            pl.BlockSpec((None, d, tn), lambda l, k, j: (l, 0, k * nj + j)),
            pl.BlockSpec((None, 1, tn), lambda l, k, j: (l, 0, k * nj + j)),
        ],
        out_specs=pl.BlockSpec((None, None, MOD_ROWS, tn), lambda l, k, j: (l, k, 0, j)),
        compiler_params=_cparams(("parallel", "parallel", "parallel")),
        name="mod_table",
    )(c8, w_ada, b_ada.reshape(depth, 1, nd))
    return out.reshape(depth, N_MOD, MOD_ROWS, 1, d)


def _mod_spec(layer, k, row_fn, d):
    return pl.BlockSpec((None, None, None, 1, d), lambda i, *_: (layer, k, row_fn(i), 0, 0))


def _mod3_spec(layer, k0, row_fn, d):
    assert k0 % 3 == 0
    return pl.BlockSpec((None, 3, None, 1, d), lambda i, *_: (layer, k0 // 3, row_fn(i), 0, 0))


def _prefetch_rows(tm, steps):
    return 16 * pl.cdiv(pl.cdiv(tm, steps), 16)


def _normalise_next_slice(step, steps, xn_ref, g_ref, mn_ref, h_next_ref):
    tm = xn_ref.shape[0]
    pre = _prefetch_rows(tm, steps)
    rows = pl.ds(pl.multiple_of(jnp.minimum(step * pre, tm - pre), 16), pre)
    h_next_ref[rows, :] = _rms_mod(xn_ref[rows, :], g_ref[...], mn_ref[0], mn_ref[1]).astype(BF16)


def _ffn_kernel(x_ref, xn_ref, g_ref, m_ref, mn_ref, wg_ref, wv_ref, wd_ref,
                o_ref, h_ref, hn_ref, acc_ref, *, nf):
    i, f = pl.program_id(0), pl.program_id(1)

    @pl.when(jnp.logical_and(i == 0, f == 0))
    def _():
        h_ref[...] = _rms_mod(x_ref[...], g_ref[...], m_ref[0], m_ref[1]).astype(BF16)

    @pl.when(jnp.logical_and(i > 0, f == 0))
    def _():
        h_ref[...] = hn_ref[...]

    h = h_ref[...]
    gate = _dot(h, wg_ref[...])
    val = _dot(h, wv_ref[...])
    act = (_silu(gate) * val).astype(BF16)
    acc_ref[...] = jnp.where(f == 0, 0.0, acc_ref[...]) + _dot(act, wd_ref[...])
    _normalise_next_slice(f, nf, xn_ref, g_ref, mn_ref, hn_ref)

    @pl.when(f == nf - 1)
    def _():
        o_ref[...] = x_ref[...] + 0.5 * m_ref[2] * acc_ref[...]


def _ffn(x, n_rows, mod, layer, which, row_fn, gain, w_up, w_down, tm, tf):
    d = x.shape[1]
    ffn = w_down.shape[2]
    nf = ffn // tf
    nt = n_rows // tm
    k0 = 6 * which

    def nxt(i):
        return jnp.minimum(i + 1, nt - 1)

    return pl.pallas_call(
        functools.partial(_ffn_kernel, nf=nf),
        out_shape=jax.ShapeDtypeStruct((n_rows, d), F32),
        grid=(nt, nf),
        in_specs=[
            pl.BlockSpec((tm, d), lambda i, f: (i, 0)),
            pl.BlockSpec((tm, d), lambda i, f: (nxt(i), 0)),
            pl.BlockSpec((1, d), lambda i, f: (0, 0)),
            _mod3_spec(layer, k0, row_fn, d),
            _mod3_spec(layer, k0, lambda i: row_fn(nxt(i)), d),
            pl.BlockSpec((None, None, d, tf), lambda i, f: (layer, which, 0, f)),
            pl.BlockSpec((None, None, d, tf), lambda i, f: (layer, which, 0, nf + f)),
            pl.BlockSpec((None, None, tf, d), lambda i, f: (layer, which, f, 0)),
        ],
        out_specs=pl.BlockSpec((tm, d), lambda i, f: (i, 0)),
        scratch_shapes=[pltpu.VMEM((tm, d), BF16), pltpu.VMEM((tm, d), BF16), pltpu.VMEM((tm, d), F32)],
        compiler_params=_cparams(("arbitrary", "arbitrary")),
        name="ffn",
    )(x, x, gain, mod, mod, w_up, w_up, w_down)


def _inproj_kernel(x_ref, xn_ref, g_ref, m_ref, mn_ref, w_ref, w1_ref, qkg_ref,
                   cos_ref, sin_ref, z_ref, r_ref, h_ref, hn_ref, *, ncol):
    i, n = pl.program_id(0), pl.program_id(1)

    @pl.when(jnp.logical_and(i == 0, n == 0))
    def _():
        h_ref[...] = _rms_mod(x_ref[...], g_ref[...], m_ref[0], m_ref[1]).astype(BF16)

    @pl.when(jnp.logical_and(i > 0, n == 0))
    def _():
        h_ref[...] = hn_ref[...]

    @pl.when(n == 0)
    def _():
        r_ref[...] = _dot(h_ref[...], w1_ref[...])

    _normalise_next_slice(n, ncol, xn_ref, g_ref, mn_ref, hn_ref)
    zt = _dot(h_ref[...], w_ref[...])

    @pl.when(n == 0)
    def _():
        cos, sin = cos_ref[...], sin_ref[...]
        for hd in range(zt.shape[1] // HEAD_DIM):
            cols = slice(hd * HEAD_DIM, (hd + 1) * HEAD_DIM)
            y = _rms_head(zt[:, cols], qkg_ref[:, cols])
            z_ref[:, cols] = (y * cos + pltpu.roll(y, HEAD_DIM // 2, 1) * sin).astype(z_ref.dtype)

    @pl.when(n != 0)
    def _():
        z_ref[...] = zt.astype(z_ref.dtype)


def _inproj(x, mod, layer, row_fn, gain, w_in, w1, qk_gain, cos_t, sin_t, n_lat, seq, tm):
    n_rows, d = x.shape
    ncols = w_in.shape[2]
    tn = QK_COLS
    rope_blocks = seq // tm

    def rope_blk(i, n):
        return jnp.where(i * tm < n_lat, (i % rope_blocks), rope_blocks)

    nt = n_rows // tm

    def nxt(i):
        return jnp.minimum(i + 1, nt - 1)

    return pl.pallas_call(
        functools.partial(_inproj_kernel, ncol=ncols // tn),
        out_shape=(jax.ShapeDtypeStruct((n_rows, ncols), BF16),
                   jax.ShapeDtypeStruct((n_rows, LANES), F32)),
        grid=(nt, ncols // tn),
        in_specs=[
            pl.BlockSpec((tm, d), lambda i, n: (i, 0)),
            pl.BlockSpec((tm, d), lambda i, n: (nxt(i), 0)),
            pl.BlockSpec((1, d), lambda i, n: (0, 0)),
            _mod3_spec(layer, 3, row_fn, d),
            _mod3_spec(layer, 3, lambda i: row_fn(nxt(i)), d),
            pl.BlockSpec((None, d, tn), lambda i, n: (layer, 0, n)),
            pl.BlockSpec((None, d, LANES), lambda i, n: (layer, 0, 0)),
            pl.BlockSpec((1, tn), lambda i, n: (0, 0)),
            pl.BlockSpec((tm, HEAD_DIM), lambda i, n: (rope_blk(i, n), 0)),
            pl.BlockSpec((tm, HEAD_DIM), lambda i, n: (rope_blk(i, n), 0)),
        ],
        out_specs=(pl.BlockSpec((tm, tn), lambda i, n: (i, n)),
                   pl.BlockSpec((tm, LANES), lambda i, n: (i, 0))),
        scratch_shapes=[pltpu.VMEM((tm, d), BF16), pltpu.VMEM((tm, d), BF16)],
        compiler_params=_cparams(("arbitrary", "arbitrary")),
        name="inproj",
    )(x, x, gain, mod, mod, w_in, w1, qk_gain, cos_t, sin_t)


def _gmlp_kernel(u_ref, v_ref, gain_ref, ws_ref, bs_ref, o_ref):
    tg, width = u_ref.shape
    groups = width // HEAD_DIM
    u = jax.nn.gelu(u_ref[...].astype(F32))
    v = jax.nn.gelu(v_ref[...].astype(F32))
    mu = jnp.mean(v, axis=-1, keepdims=True)
    vc = v - mu
    var = jnp.mean(vc * vc, axis=-1, keepdims=True)
    vn = (vc * lax.rsqrt(var + EPS) * gain_ref[...]).astype(BF16)
    for c in range(tg // A_CHUNK):
        rows = slice(c * A_CHUNK, (c + 1) * A_CHUNK)
        for g in range(groups):
            cols = slice(g * HEAD_DIM, (g + 1) * HEAD_DIM)
            mixed = _dot(ws_ref[g], vn[rows, cols]) + bs_ref[g]
            o_ref[rows, cols] = (u[rows, cols] * mixed).astype(o_ref.dtype)


def _gmlp(z, n_rows, a_v_gain, a_ws, a_bs, tg):
    groups = a_ws.shape[0]
    width = groups * HEAD_DIM
    bs_full = jnp.broadcast_to(a_bs[:, :, None], (groups, A_CHUNK, HEAD_DIM)).astype(F32)
    return pl.pallas_call(
        _gmlp_kernel,
        out_shape=jax.ShapeDtypeStruct((n_rows, width), BF16),
        grid=(n_rows // tg,),
        in_specs=[
            pl.BlockSpec((tg, width), lambda i: (i, Z_AU // width)),
            pl.BlockSpec((tg, width), lambda i: (i, Z_AV // width)),
            pl.BlockSpec((1, width), lambda i: (0, 0)),
            pl.BlockSpec((groups, A_CHUNK, A_CHUNK), lambda i: (0, 0, 0)),
            pl.BlockSpec((groups, A_CHUNK, HEAD_DIM), lambda i: (0, 0, 0)),
        ],
        out_specs=pl.BlockSpec((tg, width), lambda i: (i, 0)),
        compiler_params=_cparams(("parallel",)),
        name="gmlp",
    )(z, z, a_v_gain.reshape(1, width), a_ws.astype(BF16), bs_full)


def _gla_prep(k_ref, q_ref, r_ref, w2, bias, tri_ref, reverse):
    tb = k_ref.shape[0]
    heads = k_ref.shape[1] // B_DK
    nchunk = tb // B_CHUNK
    logit = _dot(r_ref[...].astype(BF16), w2) + bias
    g = (jnp.minimum(logit, 0.0) - jnp.log(1.0 + jnp.exp(-jnp.abs(logit)))) * (1.0 / B_TAU)
    g_hi = g.astype(BF16)
    g_lo = (g - g_hi.astype(F32)).astype(BF16)
    tri = tri_ref[...]
    bc = _dot(tri, g_hi) + _dot(tri, g_lo)
    last = [c * B_CHUNK + (0 if reverse else B_CHUNK - 1) for c in range(nchunk)]
    tots = [bc[r:r + 1, :] for r in last]
    tot_rows = jnp.concatenate([jnp.broadcast_to(t, (B_CHUNK, t.shape[1])) for t in tots], axis=0)
    kf = k_ref[...].astype(F32)
    q_in = (q_ref[...].astype(F32) * jnp.exp(bc) * (B_DK ** -0.5)).astype(BF16)
    k_in = kf * jnp.exp(-bc)
    k_out = kf * jnp.exp(tot_rows - bc)
    dec = [jnp.exp(t) for t in tots]
    lane = lax.broadcasted_iota(jnp.int32, (tb, LANES), 1)
    zeros = jnp.zeros((B_CHUNK, LANES), BF16)
    q_slabs, k_in_heads, k4_heads = [], [], []
    for h in range(heads):
        slab = slice((h // 2) * LANES, (h // 2 + 1) * LANES)
        mine = (lane >= B_DK) if h % 2 else (lane < B_DK)
        q_slabs.append(q_in[:, slab])
        k_in_heads.append(jnp.where(mine, k_in[:, slab], 0.0).astype(BF16))
        k_out_h = jnp.where(mine, k_out[:, slab], 0.0).astype(BF16)
        k4_heads.append(jnp.concatenate([
            jnp.concatenate([zeros] * c + [k_out_h[c * B_CHUNK:(c + 1) * B_CHUNK]] + [zeros] * (nchunk - 1 - c),
                            axis=0) for c in range(nchunk)], axis=1))
    return q_slabs, k_in_heads, k4_heads, dec


def _gla_kernel(kf_ref, vf_ref, qf_ref, rf_ref, kb_ref, vb_ref, qb_ref, rb_ref, w2_ref, b_ref,
                tri_ref, keep_ref, of_ref, ob_ref, sf_ref, sb_ref, a_ref, u_ref):
    @pl.when(pl.program_id(1) == 0)
    def _():
        sf_ref[...] = jnp.zeros_like(sf_ref)
        sb_ref[...] = jnp.zeros_like(sb_ref)

    heads = sf_ref.shape[0]
    nchunk = kf_ref.shape[0] // B_CHUNK
    v_refs, o_refs, st_refs = (vf_ref, vb_ref), (of_ref, ob_ref), (sf_ref, sb_ref)
    prep = [_gla_prep(kf_ref, qf_ref, rf_ref, w2_ref[0], b_ref[0], tri_ref.at[0], False),
            _gla_prep(kb_ref, qb_ref, rb_ref, w2_ref[1], b_ref[1], tri_ref.at[1], True)]
    jobs = [(d, h) for d in range(2) for h in range(heads)]
    for d, h in jobs:
        q_slabs, k_in_heads, k4_heads, _ = prep[d]
        v_h = v_refs[d][:, h * B_DV:(h + 1) * B_DV]
        a_ref[d * heads + h] = _dot_nt(q_slabs[h], k_in_heads[h])
        u_ref[d * heads + h] = _dot_tn(v_h, k4_heads[h])
    for d, h in jobs:
        vcols = slice(h * B_DV, (h + 1) * B_DV)
        attn = jnp.where(keep_ref[d] > 0.5, a_ref[d * heads + h], 0.0).astype(BF16)
        o_refs[d][:, vcols] = _dot(attn, v_refs[d][:, vcols])
    for d, h in jobs:
        q_s, dec = prep[d][0][h], prep[d][3]
        slab = slice((h // 2) * LANES, (h // 2 + 1) * LANES)
        vcols = slice(h * B_DV, (h + 1) * B_DV)
        st = st_refs[d][h]
        for c in (range(nchunk - 1, -1, -1) if d else range(nchunk)):
            rows = slice(c * B_CHUNK, (c + 1) * B_CHUNK)
            o_refs[d][rows, vcols] += _dot_nt(q_s[rows], st.astype(BF16))
            st = st * dec[c][:, slab] + u_ref[d * heads + h, :, c * LANES:(c + 1) * LANES]
        st_refs[d][h] = st


def _chunk_triangles(tb):
    t = np.arange(tb)[:, None]
    s = np.arange(tb)[None, :]
    same = (t // B_CHUNK) == (s // B_CHUNK)
    return np.stack([same & (s <= t), same & (s >= t)]).astype(np.float32)


def _gla(z, r, w2pad, bias, batch, seq, ctx_len, tb):
    n_rows = z.shape[0]
    heads = bias.shape[-1] // B_DK
    assert heads % 2 == 0 and 2 * B_DK == LANES
    kw, vw = heads * B_DK, heads * B_DV
    nlat, nctx = seq // tb, ctx_len // tb
    base = batch * nlat
    tri = _chunk_triangles(tb)

    def fwd(b, j):
        return jnp.where(j < nctx, base + b * nctx + j, b * nlat + (j - nctx))

    def bwd(b, j):
        return jnp.where(j < nctx, base + b * nctx + (nctx - 1 - j), b * nlat + (nlat - 1 - (j - nctx)))

    def specs(blk):
        return [pl.BlockSpec((tb, kw), lambda b, j: (blk(b, j), Z_BK // kw)),
                pl.BlockSpec((tb, vw), lambda b, j: (blk(b, j), Z_BV // vw)),
                pl.BlockSpec((tb, kw), lambda b, j: (blk(b, j), Z_BQ // kw)),
                pl.BlockSpec((tb, LANES), lambda b, j: (blk(b, j), 0))]

    return pl.pallas_call(
        _gla_kernel,
        out_shape=(jax.ShapeDtypeStruct((n_rows, vw), F32), jax.ShapeDtypeStruct((n_rows, vw), F32)),
        grid=(batch, nctx + nlat),
        in_specs=specs(fwd) + specs(bwd) + [
            pl.BlockSpec((2, LANES, kw), lambda b, j: (0, 0, 0)),
            pl.BlockSpec((2, 1, kw), lambda b, j: (0, 0, 0)),
            pl.BlockSpec((2, tb, tb), lambda b, j: (0, 0, 0)),
            pl.BlockSpec((2, tb, tb), lambda b, j: (0, 0, 0)),
        ],
        out_specs=(pl.BlockSpec((tb, vw), lambda b, j: (fwd(b, j), 0)),
                   pl.BlockSpec((tb, vw), lambda b, j: (bwd(b, j), 0))),
        scratch_shapes=[pltpu.VMEM((heads, B_DV, LANES), F32), pltpu.VMEM((heads, B_DV, LANES), F32),
                        pltpu.VMEM((2 * heads, tb, tb), F32),
                        pltpu.VMEM((2 * heads, B_DV, (tb // B_CHUNK) * LANES), F32)],
        compiler_params=_cparams(("parallel", "arbitrary")),
        name="gla",
    )(z, z, z, r, z, z, z, r, w2pad.astype(BF16), bias, jnp.asarray(tri, BF16), jnp.asarray(tri, F32))


def _softmax_pv(s_ref, row0, bias_ref, sink, v_all, group, tq, o_ref, head0):
    for g in range(group):
        sg = s_ref[row0 + g * tq:row0 + (g + 1) * tq, :]
        if bias_ref is not None:
            sg = sg + bias_ref[...]
        m = jnp.maximum(jnp.max(sg, axis=-1, keepdims=True), sink[g])
        e = jnp.exp(sg - m)
        den = jnp.sum(e, axis=-1, keepdims=True) + jnp.exp(sink[g] - m)
        h = head0 + g
        o_ref[:, h * HEAD_DIM:(h + 1) * HEAD_DIM] = (_dot(e.astype(BF16), v_all) / den).astype(o_ref.dtype)


def _attn_kernel(sink_ref, q_ref, kp_ref, kc_ref, kn_ref, kx_ref, vp_ref, vc_ref, vn_ref, vx_ref,
                 bias_ref, o_ref, s_ref):
    tq = q_ref.shape[0]
    n_heads = q_ref.shape[1] // HEAD_DIM
    kv_heads = kc_ref.shape[1] // HEAD_DIM
    group = n_heads // kv_heads
    for kh in range(kv_heads):
        cols = slice(kh * HEAD_DIM, (kh + 1) * HEAD_DIM)
        k_all = jnp.concatenate([kp_ref[:, cols], kc_ref[:, cols], kn_ref[:, cols], kx_ref[:, cols]], axis=0)
        q = jnp.concatenate([q_ref[:, (kh * group + g) * HEAD_DIM:(kh * group + g + 1) * HEAD_DIM]
                             for g in range(group)], axis=0)
        s_ref[kh * group * tq:(kh + 1) * group * tq, :] = _dot_nt(q, k_all)
    for kh in range(kv_heads):
        cols = slice(kh * HEAD_DIM, (kh + 1) * HEAD_DIM)
        v_all = jnp.concatenate([vp_ref[:, cols], vc_ref[:, cols], vn_ref[:, cols], vx_ref[:, cols]], axis=0)
        sink = [sink_ref[kh * group + g] for g in range(group)]
        _softmax_pv(s_ref, kh * group * tq, bias_ref, sink, v_all, group, tq, o_ref, kh * group)


def _window_bias(tq, ctx_len):
    i = np.arange(tq)[:, None]
    j = np.arange(3 * tq + ctx_len)[None, :]
    band = ((j >= i) & (j <= i + 2 * tq)) | (j >= 3 * tq)
    cases = [band & (j >= tq), band, band & ((j < 2 * tq) | (j >= 3 * tq))]
    return jnp.asarray(np.where(np.stack(cases), 0.0, NEG_BIG), F32)


def _attn(z, sink, n_out, batch, seq, ctx_len):
    n_heads = sink.shape[0]
    qw = n_heads * HEAD_DIM
    kvw = C_KVW
    tq = C_BLOCK
    nb = seq // tq
    assert nb >= 2
    ctx_base = batch * seq // ctx_len

    def cur(b, n):
        return b * nb + n

    def prev(b, n):
        return b * nb + jnp.maximum(n - 1, 0)

    def nxt(b, n):
        return b * nb + jnp.minimum(n + 1, nb - 1)

    def zspec(blk, col0):
        return pl.BlockSpec((tq, kvw), lambda b, n: (blk(b, n), col0 // kvw))

    def bias_case(b, n):
        return jnp.where(n == 0, 0, jnp.where(n == nb - 1, 2, 1))

    return pl.pallas_call(
        _attn_kernel,
        out_shape=jax.ShapeDtypeStruct((n_out, qw), BF16),
        grid=(batch, nb),
        in_specs=[
            pl.BlockSpec(memory_space=pltpu.SMEM),
            pl.BlockSpec((tq, qw), lambda b, n: (cur(b, n), Z_CQ // qw)),
            zspec(prev, Z_CK), zspec(cur, Z_CK), zspec(nxt, Z_CK),
            pl.BlockSpec((ctx_len, kvw), lambda b, n: (ctx_base + b, Z_CK // kvw)),
            zspec(prev, Z_CV), zspec(cur, Z_CV), zspec(nxt, Z_CV),
            pl.BlockSpec((ctx_len, kvw), lambda b, n: (ctx_base + b, Z_CV // kvw)),
            pl.BlockSpec((None, tq, 3 * tq + ctx_len), lambda b, n: (bias_case(b, n), 0, 0)),
        ],
        out_specs=pl.BlockSpec((tq, qw), lambda b, n: (cur(b, n), 0)),
        scratch_shapes=[pltpu.VMEM((n_heads * tq, 3 * tq + ctx_len), F32)],
        compiler_params=_cparams(("parallel", "parallel")),
        name="window_attn",
    )(sink, z, z, z, z, z, z, z, z, z, _window_bias(tq, ctx_len))


def _ctx_attn_kernel(sink_ref, q_ref, k_ref, v_ref, c_in_ref, o_ref, s_ref):
    del c_in_ref
    tq = q_ref.shape[0]
    n_heads = q_ref.shape[1] // HEAD_DIM
    kv_heads = k_ref.shape[1] // HEAD_DIM
    group = n_heads // kv_heads
    for kh in range(kv_heads):
        cols = slice(kh * HEAD_DIM, (kh + 1) * HEAD_DIM)
        q = jnp.concatenate([q_ref[:, (kh * group + g) * HEAD_DIM:(kh * group + g + 1) * HEAD_DIM]
                             for g in range(group)], axis=0)
        s_ref[kh * group * tq:(kh + 1) * group * tq, :] = _dot_nt(q, k_ref[:, cols])
    for kh in range(kv_heads):
        cols = slice(kh * HEAD_DIM, (kh + 1) * HEAD_DIM)
        sink = [sink_ref[kh * group + g] for g in range(group)]
        _softmax_pv(s_ref, kh * group * tq, None, sink, v_ref[:, cols], group, tq, o_ref, kh * group)


def _ctx_attn(z, sink, c_out, batch, seq, ctx_len):
    n_heads = sink.shape[0]
    qw = n_heads * HEAD_DIM
    kvw = C_KVW
    ctx_base = batch * seq // ctx_len
    return pl.pallas_call(
        _ctx_attn_kernel,
        out_shape=jax.ShapeDtypeStruct(c_out.shape, c_out.dtype),
        grid=(batch,),
        in_specs=[
            pl.BlockSpec(memory_space=pltpu.SMEM),
            pl.BlockSpec((ctx_len, qw), lambda b: (ctx_base + b, Z_CQ // qw)),
            pl.BlockSpec((ctx_len, kvw), lambda b: (ctx_base + b, Z_CK // kvw)),
            pl.BlockSpec((ctx_len, kvw), lambda b: (ctx_base + b, Z_CV // kvw)),
            pl.BlockSpec(memory_space=pl.ANY),
        ],
        out_specs=pl.BlockSpec((ctx_len, qw), lambda b: (ctx_base + b, 0)),
        input_output_aliases={4: 0},
        scratch_shapes=[pltpu.VMEM((n_heads * ctx_len, ctx_len), F32)],
        compiler_params=_cparams(("parallel",)),
        name="ctx_attn",
    )(sink, z, z, z, c_out)


def _merge_kernel(x_ref, gt_ref, a_ref, of_ref, ob_ref, og_ref, c_ref, ga_ref, gb_ref, gc_ref,
                  bg_ref, wa_ref, wb_ref, wc_ref, wo_ref, o_ref):
    heads = of_ref.shape[1] // B_DV
    o = of_ref[...] + ob_ref[...]
    bn = jnp.concatenate([_rms_head(o[:, h * B_DV:(h + 1) * B_DV], bg_ref[...]) for h in range(heads)], axis=1)
    b = (bn * _silu(og_ref[...].astype(F32))).astype(BF16)
    merged = (jax.nn.sigmoid(ga_ref[...].astype(F32)) * _dot(a_ref[...], wa_ref[...])
              + jax.nn.sigmoid(gb_ref[...].astype(F32)) * _dot(b, wb_ref[...])
              + jax.nn.sigmoid(gc_ref[...].astype(F32)) * _dot(c_ref[...], wc_ref[...]))
    mix = _dot(merged.astype(BF16), wo_ref[...])
    o_ref[...] = x_ref[...] + gt_ref[...] * mix


def _merge(x, n_rows, mod, layer, row_fn, z, a, o_f, o_b, c, b_norm_g, wa, wb, wc, wo, tm):
    d = x.shape[1]
    aw, bw, cw = a.shape[1], o_f.shape[1], c.shape[1]

    def const(rows):
        return pl.BlockSpec((None, rows, d), lambda i: (layer, 0, 0), pipeline_mode=pl.Buffered(1))

    return pl.pallas_call(
        _merge_kernel,
        out_shape=jax.ShapeDtypeStruct((n_rows, d), F32),
        grid=(n_rows // tm,),
        in_specs=[
            pl.BlockSpec((tm, d), lambda i: (i, 0)),
            _mod_spec(layer, 5, row_fn, d),
            pl.BlockSpec((tm, aw), lambda i: (i, 0)),
            pl.BlockSpec((tm, bw), lambda i: (i, 0)),
            pl.BlockSpec((tm, bw), lambda i: (i, 0)),
            pl.BlockSpec((tm, bw), lambda i: (i, Z_BG // bw)),
            pl.BlockSpec((tm, cw), lambda i: (i, 0)),
            pl.BlockSpec((tm, d), lambda i: (i, Z_GATE // d)),
            pl.BlockSpec((tm, d), lambda i: (i, Z_GATE // d + 1)),
            pl.BlockSpec((tm, d), lambda i: (i, Z_GATE // d + 2)),
            pl.BlockSpec((1, B_DV), lambda i: (0, 0)),
            const(aw), const(bw), const(cw), const(d),
        ],
        out_specs=pl.BlockSpec((tm, d), lambda i: (i, 0)),
        compiler_params=_cparams(("parallel",)),
        name="merge",
    )(x, mod, a, o_f, o_b, z, c, z, z, z, b_norm_g.reshape(1, B_DV), wa, wb, wc, wo)


def _rope_tables(seq, pad_rows):
    half = HEAD_DIM // 4
    t = np.arange(seq)
    pos = np.stack([t // GRID_W, t % GRID_W], axis=1).astype(np.float32)
    inv_freq = jnp.asarray(ROPE_BASE, F32) ** (-jnp.arange(half, dtype=F32) / half)
    ang = jnp.asarray(pos)[:, :, None] * inv_freq[None, None, :]
    cos = jnp.cos(ang)
    sin = jnp.sin(ang)
    cos_t = jnp.stack([cos, cos], axis=1).reshape(seq, HEAD_DIM)
    sin_t = jnp.stack([-sin, sin], axis=1).reshape(seq, HEAD_DIM)
    cos_t = jnp.concatenate([cos_t, jnp.ones((pad_rows, HEAD_DIM), F32)], axis=0)
    sin_t = jnp.concatenate([sin_t, jnp.zeros((pad_rows, HEAD_DIM), F32)], axis=0)
    return cos_t, sin_t


def _rotary_lane_order(w):
    lead = w.shape[:-1]
    nh = w.shape[-1] // HEAD_DIM
    w = w.reshape(lead + (nh, 2, 2, HEAD_DIM // 4))
    w = jnp.swapaxes(w, -3, -2)
    return w.reshape(lead + (nh * HEAD_DIM,))


def _reorder_w_in(w, d):
    bkw, bvw, ckvw, aw, cqw = 256, 512, 256, 512, 1024
    o_bk = 0
    o_bv = o_bk + bkw
    o_ck = o_bv + bvw
    o_cv = o_ck + ckvw
    o_au = o_cv + ckvw
    o_av = o_au + aw
    o_bq = o_av + aw
    o_bg = o_bq + bkw
    o_cq = o_bg + bvw
    o_gate = o_cq + cqw
    seg = lambda s, width: w[..., s:s + width]
    return jnp.concatenate([_rotary_lane_order(seg(o_cq, cqw)), _rotary_lane_order(seg(o_ck, ckvw)),
                            seg(o_cv, ckvw), seg(o_bk, bkw), seg(o_bq, bkw), seg(o_bv, bvw),
                            seg(o_au, aw), seg(o_av, aw), seg(o_bg, bvw), seg(o_gate, 3 * d)], axis=-1)


def kernel(x, c, ctx, c_ctx, w_ada, b_ada, norm_g, w_ffn_up, w_ffn_down, w_in, a_v_gain, a_ws, a_bs,
           b_decay_w1, b_decay_w2, b_decay_b, b_norm_g, c_q_gain, c_k_gain, c_sink,
           w_br_a, w_br_b, w_br_c, w_out):
    batch, seq, d = x.shape
    ctx_len = ctx.shape[1]
    depth = w_ada.shape[0]
    assert d == 2048 and batch < MOD_ROWS and seq % 256 == 0 and ctx_len == 256
    n_lat, n_ctx = batch * seq, batch * ctx_len
    n_all = n_lat + n_ctx
    tm_ffn, tm_in, tm_merge = 512, 512, 256
    for t in (tm_ffn, tm_in, tm_merge):
        assert seq % t == 0 and n_ctx % t == 0

    def row_fn(t):
        return lambda i: jnp.minimum(i * t // seq, batch)

    c8 = jnp.zeros((MOD_ROWS, d), F32).at[:batch].set(c).at[batch].set(c_ctx)
    mod = _mod_table(c8, w_ada, b_ada)
    cos_t, sin_t = _rope_tables(seq, tm_in)

    w_up = w_ffn_up.astype(BF16)
    w_dn = w_ffn_down.astype(BF16)
    w_in_r = _reorder_w_in(w_in, d).astype(BF16)
    w1 = jnp.zeros((depth, d, LANES), F32).at[:, :, :B_RANK].set(b_decay_w1[:, 0]).at[
        :, :, B_RANK:2 * B_RANK].set(b_decay_w1[:, 1]).astype(BF16)
    kw = b_decay_w2.shape[-1]
    w2pad = jnp.zeros((depth, 2, LANES, kw), F32).at[:, 0, :B_RANK].set(b_decay_w2[:, 0]).at[
        :, 1, B_RANK:2 * B_RANK].set(b_decay_w2[:, 1])
    n_qh = c_sink.shape[1]
    qk_gain = jnp.concatenate([jnp.tile(_rotary_lane_order(c_q_gain) * HEAD_DIM ** -0.5, (1, n_qh)),
                               jnp.tile(_rotary_lane_order(c_k_gain), (1, C_KVW // HEAD_DIM))], axis=1)
    wa, wb, wc, wo = (w.astype(BF16) for w in (w_br_a, w_br_b, w_br_c, w_out))

    xs = jnp.concatenate([x.reshape(n_lat, d), ctx.reshape(n_ctx, d)], axis=0)
    for l in range(depth):
        last = l == depth - 1
        gains = norm_g[l].reshape(3, 1, d)
        n_mix = n_lat if last else n_all
        xs = _ffn(xs, n_all, mod, l, 0, row_fn(tm_ffn), gains[0], w_up, w_dn, tm_ffn, 512)
        z, r = _inproj(xs, mod, l, row_fn(tm_in), gains[1], w_in_r, w1, qk_gain[l:l + 1], cos_t, sin_t,
                       n_lat, seq, tm_in)
        a_out = _gmlp(z, n_mix, a_v_gain[l], a_ws[l], a_bs[l], 512)
        o_f, o_b = _gla(z, r, w2pad[l], b_decay_b[l].reshape(2, 1, kw), batch, seq, ctx_len, 256)
        c_out = _attn(z, c_sink[l], n_mix, batch, seq, ctx_len)
        if not last:
            c_out = _ctx_attn(z, c_sink[l], c_out, batch, seq, ctx_len)
        xs = _merge(xs, n_mix, mod, l, row_fn(tm_merge), z, a_out, o_f, o_b, c_out, b_norm_g[l],
                    wa, wb, wc, wo, tm_merge)
        xs = _ffn(xs, n_mix, mod, l, 1, row_fn(tm_ffn), gains[2], w_up, w_dn, tm_ffn, 512)
    return xs.reshape(batch, seq, d)
```

```python
import functools

import jax
import jax.numpy as jnp
import numpy as np
from jax import lax
from jax.experimental import pallas as pl
from jax.experimental.pallas import tpu as pltpu

F32 = jnp.float32
BF16 = jnp.bfloat16

HEAD_DIM = 128
EPS = 1e-6
N_MOD = 9
GRID_W = 64
A_CHUNK = 128
B_DK = 64
B_DV = 128
B_RANK = 16
B_TAU = 16.0
B_CHUNK = 64
C_BLOCK = 128
ROPE_BASE = 10000.0
MOD_ROWS = 8
LANES = 128
NEG_BIG = -1e30
VMEM_LIMIT = 56 * 1024 * 1024

Z_CQ, Z_CK, Z_CV, Z_BK, Z_BQ, Z_BV, Z_AU, Z_AV, Z_BG, Z_GATE = (
    0, 1024, 1280, 1536, 1792, 2048, 2560, 3072, 3584, 4096)
QK_COLS = Z_CV
C_KVW = Z_CV - Z_CK


def _cparams(sem):
    return pltpu.CompilerParams(dimension_semantics=sem, vmem_limit_bytes=VMEM_LIMIT)


def _dot(a, b):
    return jnp.dot(a, b, preferred_element_type=F32)


def _dot_nt(a, b):
    return lax.dot_general(a, b, (((1,), (1,)), ((), ())), preferred_element_type=F32)


def _dot_tn(a, b):
    return lax.dot_general(a, b, (((0,), (0,)), ((), ())), preferred_element_type=F32)


def _dot_f32(a, b):
    return jnp.dot(a, b, preferred_element_type=F32, precision=lax.Precision.HIGHEST)


def _silu(x):
    return x * jax.nn.sigmoid(x)


def _rms_mod(x, gain, shift, scale):
    ms = jnp.mean(x * x, axis=-1, keepdims=True)
    y = x * lax.rsqrt(ms + EPS) * gain
    return y * (1.0 + scale) + shift


def _rms_head(xh, gain):
    ms = jnp.mean(xh * xh, axis=-1, keepdims=True)
    return xh * lax.rsqrt(ms + EPS) * gain


def _mod_kernel(c_ref, w_ref, b_ref, o_ref):
    act = _silu(c_ref[...])
    o_ref[...] = _dot(act.astype(BF16), w_ref[...].astype(BF16)) + b_ref[...]


def _mod_table(c8, w_ada, b_ada):
    depth, d, nd = w_ada.shape
    tn = 1024
    nj = d // tn
    out = pl.pallas_call(
        _mod_kernel,
        out_shape=jax.ShapeDtypeStruct((depth, N_MOD, MOD_ROWS, d), F32),
        grid=(depth, N_MOD, nj),
        in_specs=[
            pl.BlockSpec((MOD_ROWS, d), lambda l, k, j: (0, 0)),
            pl.BlockSpec((None, d, tn), lambda l, k, j: (l, 0, k * nj + j)),
            pl.BlockSpec((None, 1, tn), lambda l, k, j: (l, 0, k * nj + j)),
        ],
        out_specs=pl.BlockSpec((None, None, MOD_ROWS, tn), lambda l, k, j: (l, k, 0, j)),
        compiler_params=_cparams(("parallel", "parallel", "parallel")),
        name="mod_table",
    )(c8, w_ada, b_ada.reshape(depth, 1, nd))
    return out.reshape(depth, N_MOD, MOD_ROWS, 1, d)


def _mod_spec(layer, k, row_fn, d):
    return pl.BlockSpec((None, None, None, 1, d), lambda i, *_: (layer, k, row_fn(i), 0, 0))


def _mod3_spec(layer, k0, row_fn, d):
    assert k0 % 3 == 0
    return pl.BlockSpec((None, 3, None, 1, d), lambda i, *_: (layer, k0 // 3, row_fn(i), 0, 0))


def _ffn_kernel(*refs, tiles_per_source):
    n_src = len(tiles_per_source)
    x_refs = refs[:n_src]
    g_ref, m_ref, wg_ref, wv_ref, wd_ref, o_ref, h_ref, acc_ref = refs[n_src:]
    i, f = pl.program_id(0), pl.program_id(1)

    def with_own_source(fn):
        start = 0
        for x_ref, n_tiles in zip(x_refs, tiles_per_source):
            if n_src == 1:
                fn(x_ref)
            else:
                pl.when(jnp.logical_and(i >= start, i < start + n_tiles))(lambda x_ref=x_ref: fn(x_ref))
            start += n_tiles

    def prologue(x_ref):
        h_ref[...] = _rms_mod(x_ref[...], g_ref[...], m_ref[0], m_ref[1]).astype(BF16)

    def epilogue(x_ref):
        o_ref[...] = x_ref[...] + 0.5 * m_ref[2] * acc_ref[...]

    @pl.when(f == 0)
    def _():
        with_own_source(prologue)
        acc_ref[...] = jnp.zeros_like(acc_ref)

    h = h_ref[...]
    gate = _dot(h, wg_ref[...])
    val = _dot(h, wv_ref[...])
    act = (_silu(gate) * val).astype(BF16)
    acc_ref[...] += _dot(act, wd_ref[...])

    @pl.when(f == pl.num_programs(1) - 1)
    def _():
        with_own_source(epilogue)


def _ffn(xs, n_rows, mod, layer, which, row_fn, gain, w_up, w_down, tm, tf):
    d = xs[0].shape[1]
    ffn = w_down.shape[2]
    nf = ffn // tf
    k0 = 6 * which
    nt = n_rows // tm
    tiles, starts = [], []
    for x in xs:
        starts.append(sum(tiles))
        tiles.append(min(x.shape[0] // tm, nt - sum(tiles)))

    def src_spec(start, n_tiles):
        return pl.BlockSpec((tm, d), lambda i, f: (jnp.clip(i - start, 0, n_tiles - 1), 0))

    return pl.pallas_call(
        functools.partial(_ffn_kernel, tiles_per_source=tuple(tiles)),
        out_shape=jax.ShapeDtypeStruct((n_rows, d), F32),
        grid=(nt, nf),
        in_specs=[src_spec(s, t) for s, t in zip(starts, tiles)] + [
            pl.BlockSpec((1, d), lambda i, f: (0, 0)),
            _mod3_spec(layer, k0, row_fn, d),
            pl.BlockSpec((None, None, d, tf), lambda i, f: (layer, which, 0, f)),
            pl.BlockSpec((None, None, d, tf), lambda i, f: (layer, which, 0, nf + f)),
            pl.BlockSpec((None, None, tf, d), lambda i, f: (layer, which, f, 0)),
        ],
        out_specs=pl.BlockSpec((tm, d), lambda i, f: (i, 0)),
        scratch_shapes=[pltpu.VMEM((tm, d), BF16), pltpu.VMEM((tm, d), F32)],
        compiler_params=_cparams(("parallel", "arbitrary")),
        name="ffn",
    )(*xs, gain, mod, w_up, w_up, w_down)


def _inproj_kernel(x_ref, g_ref, m_ref, w_ref, w1_ref, qkg_ref, cos_ref, sin_ref, z_ref, r_ref, h_ref):
    n = pl.program_id(1)

    @pl.when(n == 0)
    def _():
        h = _rms_mod(x_ref[...], g_ref[...], m_ref[0], m_ref[1]).astype(BF16)
        h_ref[...] = h
        r_ref[...] = _dot(h, w1_ref[...])

    zt = _dot(h_ref[...], w_ref[...])

    @pl.when(n == 0)
    def _():
        cos, sin = cos_ref[...], sin_ref[...]
        for hd in range(QK_COLS // HEAD_DIM):
            cols = slice(hd * HEAD_DIM, (hd + 1) * HEAD_DIM)
            y = _rms_head(zt[:, cols], qkg_ref[:, cols])
            z_ref[:, cols] = (y * cos + pltpu.roll(y, HEAD_DIM // 2, 1) * sin).astype(z_ref.dtype)
        if zt.shape[1] > QK_COLS:
            z_ref[:, QK_COLS:] = zt[:, QK_COLS:].astype(z_ref.dtype)

    @pl.when(n != 0)
    def _():
        z_ref[...] = zt.astype(z_ref.dtype)


def _inproj(x, mod, layer, row_fn, gain, w_in, w1, qk_gain, cos_t, sin_t, n_lat, seq, tm, tn):
    n_rows, d = x.shape
    ncols = w_in.shape[2]
    assert tn % QK_COLS == 0 and ncols % tn == 0
    rope_blocks = seq // tm

    def rope_blk(i, n):
        return jnp.where(i * tm < n_lat, (i % rope_blocks), rope_blocks)

    return pl.pallas_call(
        _inproj_kernel,
        out_shape=(jax.ShapeDtypeStruct((n_rows, ncols), BF16),
                   jax.ShapeDtypeStruct((n_rows, LANES), F32)),
        grid=(n_rows // tm, ncols // tn),
        in_specs=[
            pl.BlockSpec((tm, d), lambda i, n: (i, 0)),
            pl.BlockSpec((1, d), lambda i, n: (0, 0)),
            _mod3_spec(layer, 3, row_fn, d),
            pl.BlockSpec((None, d, tn), lambda i, n: (layer, 0, n)),
            pl.BlockSpec((None, d, LANES), lambda i, n: (layer, 0, 0)),
            pl.BlockSpec((1, QK_COLS), lambda i, n: (0, 0)),
            pl.BlockSpec((tm, HEAD_DIM), lambda i, n: (rope_blk(i, n), 0)),
            pl.BlockSpec((tm, HEAD_DIM), lambda i, n: (rope_blk(i, n), 0)),
        ],
        out_specs=(pl.BlockSpec((tm, tn), lambda i, n: (i, n)),
                   pl.BlockSpec((tm, LANES), lambda i, n: (i, 0))),
        scratch_shapes=[pltpu.VMEM((tm, d), BF16)],
        compiler_params=_cparams(("parallel", "arbitrary")),
        name="inproj",
    )(x, gain, mod, w_in, w1, qk_gain, cos_t, sin_t)


def _gmlp_kernel(u_ref, v_ref, gain_ref, ws_ref, bs_ref, o_ref):
    tg, width = u_ref.shape
    groups = width // HEAD_DIM
    u = jax.nn.gelu(u_ref[...].astype(F32))
    v = jax.nn.gelu(v_ref[...].astype(F32))
    mu = jnp.mean(v, axis=-1, keepdims=True)
    vc = v - mu
    var = jnp.mean(vc * vc, axis=-1, keepdims=True)
    vn = (vc * lax.rsqrt(var + EPS) * gain_ref[...]).astype(BF16)
    for c in range(tg // A_CHUNK):
        rows = slice(c * A_CHUNK, (c + 1) * A_CHUNK)
        for g in range(groups):
            cols = slice(g * HEAD_DIM, (g + 1) * HEAD_DIM)
            mixed = _dot(ws_ref[g], vn[rows, cols]) + bs_ref[g]
            o_ref[rows, cols] = (u[rows, cols] * mixed).astype(o_ref.dtype)


def _gmlp(z, n_rows, a_v_gain, a_ws, a_bs, tg):
    groups = a_ws.shape[0]
    width = groups * HEAD_DIM
    bs_full = jnp.broadcast_to(a_bs[:, :, None], (groups, A_CHUNK, HEAD_DIM)).astype(F32)
    return pl.pallas_call(
        _gmlp_kernel,
        out_shape=jax.ShapeDtypeStruct((n_rows, width), BF16),
        grid=(n_rows // tg,),
        in_specs=[
            pl.BlockSpec((tg, width), lambda i: (i, Z_AU // width)),
            pl.BlockSpec((tg, width), lambda i: (i, Z_AV // width)),
            pl.BlockSpec((1, width), lambda i: (0, 0)),
            pl.BlockSpec((groups, A_CHUNK, A_CHUNK), lambda i: (0, 0, 0)),
            pl.BlockSpec((groups, A_CHUNK, HEAD_DIM), lambda i: (0, 0, 0)),
        ],
        out_specs=pl.BlockSpec((tg, width), lambda i: (i, 0)),
        compiler_params=_cparams(("parallel",)),
        name="gmlp",
    )(z, z, a_v_gain.reshape(1, width), a_ws.astype(BF16), bs_full)


def _gla_prep(k_ref, q_ref, r_ref, w2, bias, tri_ref, reverse):
    tb = k_ref.shape[0]
    heads = k_ref.shape[1] // B_DK
    nchunk = tb // B_CHUNK
    logit = _dot(r_ref[...].astype(BF16), w2) + bias
    g = (jnp.minimum(logit, 0.0) - jnp.log(1.0 + jnp.exp(-jnp.abs(logit)))) * (1.0 / B_TAU)
    g_hi = g.astype(BF16)
    g_lo = (g - g_hi.astype(F32)).astype(BF16)
    tri = tri_ref[...]
    bc = _dot(tri, g_hi) + _dot(tri, g_lo)
    last = [c * B_CHUNK + (0 if reverse else B_CHUNK - 1) for c in range(nchunk)]
    tots = [bc[r:r + 1, :] for r in last]
    tot_rows = jnp.concatenate([jnp.broadcast_to(t, (B_CHUNK, t.shape[1])) for t in tots], axis=0)
    kf = k_ref[...].astype(F32)
    q_in = (q_ref[...].astype(F32) * jnp.exp(bc) * (B_DK ** -0.5)).astype(BF16)
    k_in = kf * jnp.exp(-bc)
    k_out = kf * jnp.exp(tot_rows - bc)
    dec = [jnp.exp(t) for t in tots]
    lane = lax.broadcasted_iota(jnp.int32, (tb, LANES), 1)
    zeros = jnp.zeros((B_CHUNK, LANES), BF16)
    q_slabs, k_in_heads, k4_heads = [], [], []
    for h in range(heads):
        slab = slice((h // 2) * LANES, (h // 2 + 1) * LANES)
        mine = (lane >= B_DK) if h % 2 else (lane < B_DK)
        q_slabs.append(q_in[:, slab])
        k_in_heads.append(jnp.where(mine, k_in[:, slab], 0.0).astype(BF16))
        k_out_h = jnp.where(mine, k_out[:, slab], 0.0).astype(BF16)
        k4_heads.append(jnp.concatenate([
            jnp.concatenate([zeros] * c + [k_out_h[c * B_CHUNK:(c + 1) * B_CHUNK]] + [zeros] * (nchunk - 1 - c),
                            axis=0) for c in range(nchunk)], axis=1))
    return q_slabs, k_in_heads, k4_heads, dec


def _gla_kernel(kf_ref, vf_ref, qf_ref, rf_ref, kb_ref, vb_ref, qb_ref, rb_ref, w2_ref, b_ref,
                tri_ref, keep_ref, of_ref, ob_ref, sf_ref, sb_ref, a_ref, u_ref):
    @pl.when(pl.program_id(1) == 0)
    def _():
        sf_ref[...] = jnp.zeros_like(sf_ref)
        sb_ref[...] = jnp.zeros_like(sb_ref)

    heads = sf_ref.shape[0]
    nchunk = kf_ref.shape[0] // B_CHUNK
    v_refs, o_refs, st_refs = (vf_ref, vb_ref), (of_ref, ob_ref), (sf_ref, sb_ref)
    prep = [_gla_prep(kf_ref, qf_ref, rf_ref, w2_ref[0], b_ref[0], tri_ref.at[0], False),
            _gla_prep(kb_ref, qb_ref, rb_ref, w2_ref[1], b_ref[1], tri_ref.at[1], True)]
    jobs = [(d, h) for d in range(2) for h in range(heads)]
    for d, h in jobs:
        q_slabs, k_in_heads, k4_heads, _ = prep[d]
        v_h = v_refs[d][:, h * B_DV:(h + 1) * B_DV]
        a_ref[d * heads + h] = _dot_nt(q_slabs[h], k_in_heads[h])
        u_ref[d * heads + h] = _dot_tn(v_h, k4_heads[h])
    for d, h in jobs:
        vcols = slice(h * B_DV, (h + 1) * B_DV)
        attn = jnp.where(keep_ref[d] > 0.5, a_ref[d * heads + h], 0.0).astype(BF16)
        o_refs[d][:, vcols] = _dot(attn, v_refs[d][:, vcols])
    for d, h in jobs:
        q_s, dec = prep[d][0][h], prep[d][3]
        slab = slice((h // 2) * LANES, (h // 2 + 1) * LANES)
        vcols = slice(h * B_DV, (h + 1) * B_DV)
        st = st_refs[d][h]
        for c in (range(nchunk - 1, -1, -1) if d else range(nchunk)):
            rows = slice(c * B_CHUNK, (c + 1) * B_CHUNK)
            o_refs[d][rows, vcols] += _dot_nt(q_s[rows], st.astype(BF16))
            st = st * dec[c][:, slab] + u_ref[d * heads + h, :, c * LANES:(c + 1) * LANES]
        st_refs[d][h] = st


def _chunk_triangles(tb):
    t = np.arange(tb)[:, None]
    s = np.arange(tb)[None, :]
    same = (t // B_CHUNK) == (s // B_CHUNK)
    return np.stack([same & (s <= t), same & (s >= t)]).astype(np.float32)


def _gla(z, r, w2pad, bias, batch, seq, ctx_len, tb):
    n_rows = z.shape[0]
    heads = bias.shape[-1] // B_DK
    assert heads % 2 == 0 and 2 * B_DK == LANES
    kw, vw = heads * B_DK, heads * B_DV
    nlat, nctx = seq // tb, ctx_len // tb
    base = batch * nlat
    tri = _chunk_triangles(tb)

    def fwd(b, j):
        return jnp.where(j < nctx, base + b * nctx + j, b * nlat + (j - nctx))

    def bwd(b, j):
        return jnp.where(j < nctx, base + b * nctx + (nctx - 1 - j), b * nlat + (nlat - 1 - (j - nctx)))

    def specs(blk):
        return [pl.BlockSpec((tb, kw), lambda b, j: (blk(b, j), Z_BK // kw)),
                pl.BlockSpec((tb, vw), lambda b, j: (blk(b, j), Z_BV // vw)),
                pl.BlockSpec((tb, kw), lambda b, j: (blk(b, j), Z_BQ // kw)),
                pl.BlockSpec((tb, LANES), lambda b, j: (blk(b, j), 0))]

    return pl.pallas_call(
        _gla_kernel,
        out_shape=(jax.ShapeDtypeStruct((n_rows, vw), F32), jax.ShapeDtypeStruct((n_rows, vw), F32)),
        grid=(batch, nctx + nlat),
        in_specs=specs(fwd) + specs(bwd) + [
            pl.BlockSpec((2, LANES, kw), lambda b, j: (0, 0, 0)),
            pl.BlockSpec((2, 1, kw), lambda b, j: (0, 0, 0)),
            pl.BlockSpec((2, tb, tb), lambda b, j: (0, 0, 0)),
            pl.BlockSpec((2, tb, tb), lambda b, j: (0, 0, 0)),
        ],
        out_specs=(pl.BlockSpec((tb, vw), lambda b, j: (fwd(b, j), 0)),
                   pl.BlockSpec((tb, vw), lambda b, j: (bwd(b, j), 0))),
        scratch_shapes=[pltpu.VMEM((heads, B_DV, LANES), F32), pltpu.VMEM((heads, B_DV, LANES), F32),
                        pltpu.VMEM((2 * heads, tb, tb), F32),
                        pltpu.VMEM((2 * heads, B_DV, (tb // B_CHUNK) * LANES), F32)],
        compiler_params=_cparams(("parallel", "arbitrary")),
        name="gla",
    )(z, z, z, r, z, z, z, r, w2pad.astype(BF16), bias, jnp.asarray(tri, BF16), jnp.asarray(tri, F32))


def _softmax_pv(s_ref, row0, bias_ref, sink, v_all, group, tq, o_ref, head0):
    for g in range(group):
        sg = s_ref[row0 + g * tq:row0 + (g + 1) * tq, :] + bias_ref[...]
        m = jnp.maximum(jnp.max(sg, axis=-1, keepdims=True), sink[g])
        e = jnp.exp(sg - m)
        den = jnp.sum(e, axis=-1, keepdims=True) + jnp.exp(sink[g] - m)
        h = head0 + g
        o_ref[:, h * HEAD_DIM:(h + 1) * HEAD_DIM] = (_dot(e.astype(BF16), v_all) / den).astype(o_ref.dtype)


def _attn_kernel(sink_ref, q_ref, kp_ref, kc_ref, kn_ref, kx_ref, vp_ref, vc_ref, vn_ref, vx_ref,
                 bias_ref, o_ref, s_ref):
    tq = q_ref.shape[0]
    n_heads = q_ref.shape[1] // HEAD_DIM
    kv_heads = kc_ref.shape[1] // HEAD_DIM
    group = n_heads // kv_heads
    for kh in range(kv_heads):
        cols = slice(kh * HEAD_DIM, (kh + 1) * HEAD_DIM)
        k_all = jnp.concatenate([kp_ref[:, cols], kc_ref[:, cols], kn_ref[:, cols], kx_ref[:, cols]], axis=0)
        q = jnp.concatenate([q_ref[:, (kh * group + g) * HEAD_DIM:(kh * group + g + 1) * HEAD_DIM]
                             for g in range(group)], axis=0)
        s_ref[kh * group * tq:(kh + 1) * group * tq, :] = _dot_nt(q, k_all)
    for kh in range(kv_heads):
        cols = slice(kh * HEAD_DIM, (kh + 1) * HEAD_DIM)
        v_all = jnp.concatenate([vp_ref[:, cols], vc_ref[:, cols], vn_ref[:, cols], vx_ref[:, cols]], axis=0)
        sink = [sink_ref[kh * group + g] for g in range(group)]
        _softmax_pv(s_ref, kh * group * tq, bias_ref, sink, v_all, group, tq, o_ref, kh * group)


def _window_bias(tq, ctx_len):
    i = np.arange(tq)[:, None]
    j = np.arange(3 * tq + ctx_len)[None, :]
    is_ctx = np.broadcast_to(j >= 3 * tq, (tq, 3 * tq + ctx_len))
    band = ((j >= i) & (j <= i + 2 * tq)) | is_ctx
    cases = [band & (j >= tq), band, band & ((j < 2 * tq) | is_ctx), is_ctx]
    return jnp.asarray(np.where(np.stack(cases), 0.0, NEG_BIG), F32)


def _attn(z, sink, batch, seq, ctx_len, with_ctx):
    n_heads = sink.shape[0]
    qw = n_heads * HEAD_DIM
    kvw = C_KVW
    tq = C_BLOCK
    nb = seq // tq
    cpb = ctx_len // tq if with_ctx else 0
    assert nb >= 2 and ctx_len % tq == 0
    ctx_base = batch * seq // ctx_len

    def own(b, n):
        return jnp.where(n < nb, b * nb + n, batch * nb + b * cpb + (n - nb))

    def prev(b, n):
        return b * nb + jnp.clip(n - 1, 0, nb - 1)

    def cur(b, n):
        return b * nb + jnp.minimum(n, nb - 1)

    def nxt(b, n):
        return b * nb + jnp.minimum(n + 1, nb - 1)

    def zspec(blk, col0):
        return pl.BlockSpec((tq, kvw), lambda b, n: (blk(b, n), col0 // kvw))

    def bias_case(b, n):
        return jnp.where(n >= nb, 3, jnp.where(n == 0, 0, jnp.where(n == nb - 1, 2, 1)))

    return pl.pallas_call(
        _attn_kernel,
        out_shape=jax.ShapeDtypeStruct((batch * (nb + cpb) * tq, qw), BF16),
        grid=(batch, nb + cpb),
        in_specs=[
            pl.BlockSpec(memory_space=pltpu.SMEM),
            pl.BlockSpec((tq, qw), lambda b, n: (own(b, n), Z_CQ // qw)),
            zspec(prev, Z_CK), zspec(cur, Z_CK), zspec(nxt, Z_CK),
            pl.BlockSpec((ctx_len, kvw), lambda b, n: (ctx_base + b, Z_CK // kvw)),
            zspec(prev, Z_CV), zspec(cur, Z_CV), zspec(nxt, Z_CV),
            pl.BlockSpec((ctx_len, kvw), lambda b, n: (ctx_base + b, Z_CV // kvw)),
            pl.BlockSpec((None, tq, 3 * tq + ctx_len), lambda b, n: (bias_case(b, n), 0, 0)),
        ],
        out_specs=pl.BlockSpec((tq, qw), lambda b, n: (own(b, n), 0)),
        scratch_shapes=[pltpu.VMEM((n_heads * tq, 3 * tq + ctx_len), F32)],
        compiler_params=_cparams(("parallel", "parallel")),
        name="window_attn",
    )(sink, z, z, z, z, z, z, z, z, z, _window_bias(tq, ctx_len))


def _merge_kernel(x_ref, gt_ref, a_ref, of_ref, ob_ref, og_ref, c_ref, ga_ref, gb_ref, gc_ref,
                  bg_ref, wa_ref, wb_ref, wc_ref, wo_ref, o_ref):
    heads = of_ref.shape[1] // B_DV
    o = of_ref[...] + ob_ref[...]
    bn = jnp.concatenate([_rms_head(o[:, h * B_DV:(h + 1) * B_DV], bg_ref[...]) for h in range(heads)], axis=1)
    b = (bn * _silu(og_ref[...].astype(F32))).astype(BF16)
    merged = (jax.nn.sigmoid(ga_ref[...].astype(F32)) * _dot(a_ref[...], wa_ref[...])
              + jax.nn.sigmoid(gb_ref[...].astype(F32)) * _dot(b, wb_ref[...])
              + jax.nn.sigmoid(gc_ref[...].astype(F32)) * _dot(c_ref[...], wc_ref[...]))
    mix = _dot(merged.astype(BF16), wo_ref[...])
    o_ref[...] = x_ref[...] + gt_ref[...] * mix


def _merge(x, n_rows, mod, layer, row_fn, z, a, o_f, o_b, c, b_norm_g, wa, wb, wc, wo, tm):
    d = x.shape[1]
    aw, bw, cw = a.shape[1], o_f.shape[1], c.shape[1]

    def const(rows):
        return pl.BlockSpec((None, rows, d), lambda i: (layer, 0, 0), pipeline_mode=pl.Buffered(1))

    return pl.pallas_call(
        _merge_kernel,
        out_shape=jax.ShapeDtypeStruct((n_rows, d), F32),
        grid=(n_rows // tm,),
        in_specs=[
            pl.BlockSpec((tm, d), lambda i: (i, 0)),
            _mod_spec(layer, 5, row_fn, d),
            pl.BlockSpec((tm, aw), lambda i: (i, 0)),
            pl.BlockSpec((tm, bw), lambda i: (i, 0)),
            pl.BlockSpec((tm, bw), lambda i: (i, 0)),
            pl.BlockSpec((tm, bw), lambda i: (i, Z_BG // bw)),
            pl.BlockSpec((tm, cw), lambda i: (i, 0)),
            pl.BlockSpec((tm, d), lambda i: (i, Z_GATE // d)),
            pl.BlockSpec((tm, d), lambda i: (i, Z_GATE // d + 1)),
            pl.BlockSpec((tm, d), lambda i: (i, Z_GATE // d + 2)),
            pl.BlockSpec((1, B_DV), lambda i: (0, 0)),
            const(aw), const(bw), const(cw), const(d),
        ],
        out_specs=pl.BlockSpec((tm, d), lambda i: (i, 0)),
        compiler_params=_cparams(("parallel",)),
        name="merge",
    )(x, mod, a, o_f, o_b, z, c, z, z, z, b_norm_g.reshape(1, B_DV), wa, wb, wc, wo)


def _rope_tables(seq, pad_rows):
    half = HEAD_DIM // 4
    t = np.arange(seq)
    pos = np.stack([t // GRID_W, t % GRID_W], axis=1).astype(np.float32)
    inv_freq = jnp.asarray(ROPE_BASE, F32) ** (-jnp.arange(half, dtype=F32) / half)
    ang = jnp.asarray(pos)[:, :, None] * inv_freq[None, None, :]
    cos = jnp.cos(ang)
    sin = jnp.sin(ang)
    cos_t = jnp.stack([cos, cos], axis=1).reshape(seq, HEAD_DIM)
    sin_t = jnp.stack([-sin, sin], axis=1).reshape(seq, HEAD_DIM)
    cos_t = jnp.concatenate([cos_t, jnp.ones((pad_rows, HEAD_DIM), F32)], axis=0)
    sin_t = jnp.concatenate([sin_t, jnp.zeros((pad_rows, HEAD_DIM), F32)], axis=0)
    return cos_t, sin_t


def _rotary_lane_order(w):
    lead = w.shape[:-1]
    nh = w.shape[-1] // HEAD_DIM
    w = w.reshape(lead + (nh, 2, 2, HEAD_DIM // 4))
    w = jnp.swapaxes(w, -3, -2)
    return w.reshape(lead + (nh * HEAD_DIM,))


def _reorder_w_in(w, d):
    bkw, bvw, ckvw, aw, cqw = 256, 512, 256, 512, 1024
    o_bk = 0
    o_bv = o_bk + bkw
    o_ck = o_bv + bvw
    o_cv = o_ck + ckvw
    o_au = o_cv + ckvw
    o_av = o_au + aw
    o_bq = o_av + aw
    o_bg = o_bq + bkw
    o_cq = o_bg + bvw
    o_gate = o_cq + cqw
    seg = lambda s, width: w[..., s:s + width]
    return jnp.concatenate([_rotary_lane_order(seg(o_cq, cqw)), _rotary_lane_order(seg(o_ck, ckvw)),
                            seg(o_cv, ckvw), seg(o_bk, bkw), seg(o_bq, bkw), seg(o_bv, bvw),
                            seg(o_au, aw), seg(o_av, aw), seg(o_bg, bvw), seg(o_gate, 3 * d)], axis=-1)


def kernel(x, c, ctx, c_ctx, w_ada, b_ada, norm_g, w_ffn_up, w_ffn_down, w_in, a_v_gain, a_ws, a_bs,
           b_decay_w1, b_decay_w2, b_decay_b, b_norm_g, c_q_gain, c_k_gain, c_sink,
           w_br_a, w_br_b, w_br_c, w_out):
    batch, seq, d = x.shape
    ctx_len = ctx.shape[1]
    depth = w_ada.shape[0]
    assert d == 2048 and batch < MOD_ROWS and seq % 256 == 0 and ctx_len == 256
    n_lat, n_ctx = batch * seq, batch * ctx_len
    n_all = n_lat + n_ctx
    tm_ffn, tm_in, tm_merge = 512, 512, 256
    for t in (tm_ffn, tm_in, tm_merge):
        assert seq % t == 0 and n_ctx % t == 0

    def row_fn(t):
        return lambda i: jnp.minimum(i * t // seq, batch)

    c8 = jnp.zeros((MOD_ROWS, d), F32).at[:batch].set(c).at[batch].set(c_ctx)
    mod = _mod_table(c8, w_ada, b_ada)
    cos_t, sin_t = _rope_tables(seq, tm_in)

    w_up = w_ffn_up.astype(BF16)
    w_dn = w_ffn_down.astype(BF16)
    w_in_r = _reorder_w_in(w_in, d).astype(BF16)
    w1 = jnp.zeros((depth, d, LANES), F32).at[:, :, :B_RANK].set(b_decay_w1[:, 0]).at[
        :, :, B_RANK:2 * B_RANK].set(b_decay_w1[:, 1]).astype(BF16)
    kw = b_decay_w2.shape[-1]
    w2pad = jnp.zeros((depth, 2, LANES, kw), F32).at[:, 0, :B_RANK].set(b_decay_w2[:, 0]).at[
        :, 1, B_RANK:2 * B_RANK].set(b_decay_w2[:, 1])
    n_qh = c_sink.shape[1]
    qk_gain = jnp.concatenate([jnp.tile(_rotary_lane_order(c_q_gain) * HEAD_DIM ** -0.5, (1, n_qh)),
                               jnp.tile(_rotary_lane_order(c_k_gain), (1, C_KVW // HEAD_DIM))], axis=1)
    wa, wb, wc, wo = (w.astype(BF16) for w in (w_br_a, w_br_b, w_br_c, w_out))

    srcs = [x.reshape(n_lat, d), ctx.reshape(n_ctx, d)]
    for l in range(depth):
        last = l == depth - 1
        gains = norm_g[l].reshape(3, 1, d)
        n_mix = n_lat if last else n_all
        xs = _ffn(srcs, n_all, mod, l, 0, row_fn(tm_ffn), gains[0], w_up, w_dn, tm_ffn, 512)
        z, r = _inproj(xs, mod, l, row_fn(tm_in), gains[1], w_in_r, w1, qk_gain[l:l + 1], cos_t, sin_t,
                       n_lat, seq, tm_in, 2 * QK_COLS)
        a_out = _gmlp(z, n_mix, a_v_gain[l], a_ws[l], a_bs[l], 512)
        o_f, o_b = _gla(z, r, w2pad[l], b_decay_b[l].reshape(2, 1, kw), batch, seq, ctx_len, 256)
        c_out = _attn(z, c_sink[l], batch, seq, ctx_len, with_ctx=not last)
        xs = _merge(xs, n_mix, mod, l, row_fn(tm_merge), z, a_out, o_f, o_b, c_out, b_norm_g[l],
                    wa, wb, wc, wo, tm_merge)
        xs = _ffn([xs], n_mix, mod, l, 1, row_fn(tm_ffn), gains[2], w_up, w_dn, tm_ffn, 512)
        srcs = [xs]
    return xs.reshape(batch, seq, d)
```

```python
import functools

import jax
import jax.numpy as jnp
import numpy as np
from jax import lax
from jax.experimental import pallas as pl
from jax.experimental.pallas import tpu as pltpu

F32 = jnp.float32
BF16 = jnp.bfloat16

HEAD_DIM = 128
EPS = 1e-6
N_MOD = 9
GRID_W = 64
A_CHUNK = 128
B_DK = 64
B_DV = 128
B_RANK = 16
B_TAU = 16.0
B_CHUNK = 64
C_BLOCK = 128
ROPE_BASE = 10000.0
MOD_ROWS = 8
LANES = 128
NEG_BIG = -1e30
LOG2E = 1.4426950408889634
VMEM_LIMIT = 56 * 1024 * 1024

Z_CQ, Z_CK, Z_CV, Z_BK, Z_BQ, Z_BV, Z_AU, Z_AV, Z_BG, Z_GATE = (
    0, 1024, 1280, 1536, 1792, 2048, 2560, 3072, 3584, 4096)
QK_COLS = Z_CV
C_KVW = Z_CV - Z_CK


def _cparams(sem):
    return pltpu.CompilerParams(dimension_semantics=sem, vmem_limit_bytes=VMEM_LIMIT)


def _dot(a, b):
    return jnp.dot(a, b, preferred_element_type=F32)


def _dot_nt(a, b):
    return lax.dot_general(a, b, (((1,), (1,)), ((), ())), preferred_element_type=F32)


def _dot_tn(a, b):
    return lax.dot_general(a, b, (((0,), (0,)), ((), ())), preferred_element_type=F32)


def _dot_f32(a, b):
    return jnp.dot(a, b, preferred_element_type=F32, precision=lax.Precision.HIGHEST)


def _silu(x):
    return x * jax.nn.sigmoid(x)


def _rms_mod(x, gain, shift, scale):
    ms = jnp.mean(x * x, axis=-1, keepdims=True)
    y = x * lax.rsqrt(ms + EPS) * gain
    return y * (1.0 + scale) + shift


NORM_ROWS = 64


def _rms_mod_to(x_ref, g_ref, shift, scale, h_ref):
    gain = g_ref[...] * (1.0 + scale)

    def one_pass(c, carry):
        rows = pl.ds(pl.multiple_of(c * NORM_ROWS, NORM_ROWS), NORM_ROWS)
        x = x_ref[rows, :]
        rs = lax.rsqrt(jnp.mean(x * x, axis=-1, keepdims=True) + EPS)
        h_ref[rows, :] = (x * rs * gain + shift).astype(BF16)
        return carry

    lax.fori_loop(0, x_ref.shape[0] // NORM_ROWS, one_pass, 0, unroll=2)


def _rms_head(xh, gain):
    ms = jnp.mean(xh * xh, axis=-1, keepdims=True)
    return xh * lax.rsqrt(ms + EPS) * gain


def _mod_kernel(c_ref, w_ref, b_ref, o_ref):
    act = _silu(c_ref[...])
    o_ref[...] = _dot(act.astype(BF16), w_ref[...].astype(BF16)) + b_ref[...]


def _mod_table(c8, w_ada, b_ada):
    depth, d, nd = w_ada.shape
    tn = 1024
    nj = d // tn
    out = pl.pallas_call(
        _mod_kernel,
        out_shape=jax.ShapeDtypeStruct((depth, N_MOD, MOD_ROWS, d), F32),
        grid=(depth, N_MOD, nj),
        in_specs=[
            pl.BlockSpec((MOD_ROWS, d), lambda l, k, j: (0, 0)),
            pl.BlockSpec((None, d, tn), lambda l, k, j: (l, 0, k * nj + j)),
            pl.BlockSpec((None, 1, tn), lambda l, k, j: (l, 0, k * nj + j)),
        ],
        out_specs=pl.BlockSpec((None, None, MOD_ROWS, tn), lambda l, k, j: (l, k, 0, j)),
        compiler_params=_cparams(("parallel", "parallel", "parallel")),
        name="mod_table",
    )(c8, w_ada, b_ada.reshape(depth, 1, nd))
    return out.reshape(depth, N_MOD, MOD_ROWS, 1, d)


def _mod_spec(layer, k, row_fn, d):
    return pl.BlockSpec((None, None, None, 1, d), lambda i, *_: (layer, k, row_fn(i), 0, 0))


def _mod3_spec(layer, k0, row_fn, d):
    assert k0 % 3 == 0
    return pl.BlockSpec((None, 3, None, 1, d), lambda i, *_: (layer, k0 // 3, row_fn(i), 0, 0))


def _ffn_kernel(*refs, tiles_per_source):
    n_src = len(tiles_per_source)
    x_refs = refs[:n_src]
    g_ref, m_ref, wg_ref, wv_ref, wd_ref, o_ref, h_ref, acc_ref = refs[n_src:]
    i, f = pl.program_id(0), pl.program_id(1)

    def with_own_source(fn):
        start = 0
        for x_ref, n_tiles in zip(x_refs, tiles_per_source):
            if n_src == 1:
                fn(x_ref)
            else:
                pl.when(jnp.logical_and(i >= start, i < start + n_tiles))(lambda x_ref=x_ref: fn(x_ref))
            start += n_tiles

    def prologue(x_ref):
        _rms_mod_to(x_ref, g_ref, m_ref[0], m_ref[1], h_ref)

    def epilogue(x_ref):
        o_ref[...] = x_ref[...] + 0.5 * m_ref[2] * acc_ref[...]

    @pl.when(f == 0)
    def _():
        with_own_source(prologue)
        acc_ref[...] = jnp.zeros_like(acc_ref)

    h = h_ref[...]
    gate = _dot(h, wg_ref[...])
    val = _dot(h, wv_ref[...])
    act = (_silu(gate) * val).astype(BF16)
    acc_ref[...] += _dot(act, wd_ref[...])

    @pl.when(f == pl.num_programs(1) - 1)
    def _():
        with_own_source(epilogue)


def _ffn(xs, n_rows, mod, layer, which, row_fn, gain, w_up, w_down, tm, tf):
    d = xs[0].shape[1]
    ffn = w_down.shape[2]
    nf = ffn // tf
    k0 = 6 * which
    nt = n_rows // tm
    tiles, starts = [], []
    for x in xs:
        starts.append(sum(tiles))
        tiles.append(min(x.shape[0] // tm, nt - sum(tiles)))

    def src_spec(start, n_tiles):
        return pl.BlockSpec((tm, d), lambda i, f: (jnp.clip(i - start, 0, n_tiles - 1), 0))

    return pl.pallas_call(
        functools.partial(_ffn_kernel, tiles_per_source=tuple(tiles)),
        out_shape=jax.ShapeDtypeStruct((n_rows, d), F32),
        grid=(nt, nf),
        in_specs=[src_spec(s, t) for s, t in zip(starts, tiles)] + [
            pl.BlockSpec((1, d), lambda i, f: (0, 0)),
            _mod3_spec(layer, k0, row_fn, d),
            pl.BlockSpec((None, None, d, tf), lambda i, f: (layer, which, 0, f)),
            pl.BlockSpec((None, None, d, tf), lambda i, f: (layer, which, 0, nf + f)),
            pl.BlockSpec((None, None, tf, d), lambda i, f: (layer, which, f, 0)),
        ],
        out_specs=pl.BlockSpec((tm, d), lambda i, f: (i, 0)),
        scratch_shapes=[pltpu.VMEM((tm, d), BF16), pltpu.VMEM((tm, d), F32)],
        compiler_params=_cparams(("parallel", "arbitrary")),
        name="ffn",
    )(*xs, gain, mod, w_up, w_up, w_down)


def _inproj_kernel(x_ref, g_ref, m_ref, w_ref, w1_ref, qkg_ref, cos_ref, sin_ref, z_ref, r_ref, h_ref):
    n = pl.program_id(1)

    @pl.when(n == 0)
    def _():
        _rms_mod_to(x_ref, g_ref, m_ref[0], m_ref[1], h_ref)
        r_ref[...] = _dot(h_ref[...], w1_ref[...])

    zt = _dot(h_ref[...], w_ref[...])

    @pl.when(n == 0)
    def _():
        pair = 2 * HEAD_DIM
        r = lax.broadcasted_iota(jnp.int32, (pair, pair), 0)
        c = lax.broadcasted_iota(jnp.int32, (pair, pair), 1)
        head_sum = jnp.where(jnp.bitwise_and(jnp.bitwise_xor(r, c), HEAD_DIM) == 0, 1.0, 0.0).astype(BF16)
        swap = jnp.where(c == jnp.bitwise_xor(r, HEAD_DIM // 2), 1.0, 0.0).astype(BF16)
        cos = jnp.concatenate([cos_ref[...]] * 2, axis=1)
        sin = jnp.concatenate([sin_ref[...]] * 2, axis=1)
        for p in range(QK_COLS // pair):
            cols = slice(p * pair, (p + 1) * pair)
            y = zt[:, cols]
            ms = _dot((y * y).astype(BF16), head_sum) * (1.0 / HEAD_DIM)
            y = y * lax.rsqrt(ms + EPS) * qkg_ref[:, cols]
            z_ref[:, cols] = (y * cos + _dot(y.astype(BF16), swap) * sin).astype(z_ref.dtype)
        if zt.shape[1] > QK_COLS:
            z_ref[:, QK_COLS:] = zt[:, QK_COLS:].astype(z_ref.dtype)

    @pl.when(n != 0)
    def _():
        z_ref[...] = zt.astype(z_ref.dtype)


def _inproj(x, mod, layer, row_fn, gain, w_in, w1, qk_gain, cos_t, sin_t, n_lat, seq, tm, tn):
    n_rows, d = x.shape
    ncols = w_in.shape[2]
    assert tn % QK_COLS == 0 and ncols % tn == 0
    rope_blocks = seq // tm

    def rope_blk(i, n):
        return jnp.where(i * tm < n_lat, (i % rope_blocks), rope_blocks)

    return pl.pallas_call(
        _inproj_kernel,
        out_shape=(jax.ShapeDtypeStruct((n_rows, ncols), BF16),
                   jax.ShapeDtypeStruct((n_rows, LANES), F32)),
        grid=(n_rows // tm, ncols // tn),
        in_specs=[
            pl.BlockSpec((tm, d), lambda i, n: (i, 0)),
            pl.BlockSpec((1, d), lambda i, n: (0, 0)),
            _mod3_spec(layer, 3, row_fn, d),
            pl.BlockSpec((None, d, tn), lambda i, n: (layer, 0, n)),
            pl.BlockSpec((None, d, LANES), lambda i, n: (layer, 0, 0)),
            pl.BlockSpec((1, QK_COLS), lambda i, n: (0, 0)),
            pl.BlockSpec((tm, HEAD_DIM), lambda i, n: (rope_blk(i, n), 0)),
            pl.BlockSpec((tm, HEAD_DIM), lambda i, n: (rope_blk(i, n), 0)),
        ],
        out_specs=(pl.BlockSpec((tm, tn), lambda i, n: (i, n)),
                   pl.BlockSpec((tm, LANES), lambda i, n: (i, 0))),
        scratch_shapes=[pltpu.VMEM((tm, d), BF16)],
        compiler_params=_cparams(("parallel", "arbitrary")),
        name="inproj",
    )(x, gain, mod, w_in, w1, qk_gain, cos_t, sin_t)


def _gmlp_kernel(u_ref, v_ref, gain_ref, ws_ref, bs_ref, o_ref):
    tg, width = u_ref.shape
    groups = width // HEAD_DIM
    u = jax.nn.gelu(u_ref[...].astype(F32))
    v = jax.nn.gelu(v_ref[...].astype(F32))
    mu = jnp.mean(v, axis=-1, keepdims=True)
    vc = v - mu
    var = jnp.mean(vc * vc, axis=-1, keepdims=True)
    vn = (vc * lax.rsqrt(var + EPS) * gain_ref[...]).astype(BF16)
    nchunk = tg // A_CHUNK
    for g in range(groups):
        cols = slice(g * HEAD_DIM, (g + 1) * HEAD_DIM)
        v_g = jnp.concatenate([vn[c * A_CHUNK:(c + 1) * A_CHUNK, cols] for c in range(nchunk)], axis=1)
        mixed = _dot(ws_ref[g], v_g)
        for c in range(nchunk):
            rows = slice(c * A_CHUNK, (c + 1) * A_CHUNK)
            o_ref[rows, cols] = (u[rows, cols] * (mixed[:, c * HEAD_DIM:(c + 1) * HEAD_DIM] + bs_ref[g])
                                 ).astype(o_ref.dtype)


def _gmlp(z, n_rows, a_v_gain, a_ws, a_bs, tg):
    groups = a_ws.shape[0]
    width = groups * HEAD_DIM
    bs_full = jnp.broadcast_to(a_bs[:, :, None], (groups, A_CHUNK, HEAD_DIM)).astype(F32)
    return pl.pallas_call(
        _gmlp_kernel,
        out_shape=jax.ShapeDtypeStruct((n_rows, width), BF16),
        grid=(n_rows // tg,),
        in_specs=[
            pl.BlockSpec((tg, width), lambda i: (i, Z_AU // width)),
            pl.BlockSpec((tg, width), lambda i: (i, Z_AV // width)),
            pl.BlockSpec((1, width), lambda i: (0, 0)),
            pl.BlockSpec((groups, A_CHUNK, A_CHUNK), lambda i: (0, 0, 0)),
            pl.BlockSpec((groups, A_CHUNK, HEAD_DIM), lambda i: (0, 0, 0)),
        ],
        out_specs=pl.BlockSpec((tg, width), lambda i: (i, 0)),
        compiler_params=_cparams(("parallel",)),
        name="gmlp",
    )(z, z, a_v_gain.reshape(1, width), a_ws.astype(BF16), bs_full)


def _gla_prep(k_ref, q_ref, r_ref, w2, bias, tri_ref, reverse):
    tb = k_ref.shape[0]
    heads = k_ref.shape[1] // B_DK
    nchunk = tb // B_CHUNK
    logit = _dot(r_ref[...].astype(BF16), w2) + bias
    g = (jnp.minimum(logit, 0.0) - jnp.log(1.0 + jnp.exp(-jnp.abs(logit)))) * (1.0 / B_TAU)
    g_hi = g.astype(BF16)
    g_lo = (g - g_hi.astype(F32)).astype(BF16)
    tri = tri_ref[...]
    bc = _dot(tri, g_hi) + _dot(tri, g_lo)
    last = [c * B_CHUNK + (0 if reverse else B_CHUNK - 1) for c in range(nchunk)]
    tots = [bc[r:r + 1, :] for r in last]
    tot_rows = jnp.concatenate([jnp.broadcast_to(t, (B_CHUNK, t.shape[1])) for t in tots], axis=0)
    kf = k_ref[...].astype(F32)
    q_in = (q_ref[...].astype(F32) * jnp.exp(bc) * (B_DK ** -0.5)).astype(BF16)
    k_in = kf * jnp.exp(-bc)
    k_out = kf * jnp.exp(tot_rows - bc)
    dec = [jnp.exp(t) for t in tots]
    lane = lax.broadcasted_iota(jnp.int32, (tb, LANES), 1)
    zeros = jnp.zeros((B_CHUNK, LANES), BF16)
    q_slabs, k_in_heads, k4_heads = [], [], []
    for h in range(heads):
        slab = slice((h // 2) * LANES, (h // 2 + 1) * LANES)
        mine = (lane >= B_DK) if h % 2 else (lane < B_DK)
        q_slabs.append(q_in[:, slab])
        k_in_heads.append(jnp.where(mine, k_in[:, slab], 0.0).astype(BF16))
        k_out_h = jnp.where(mine, k_out[:, slab], 0.0).astype(BF16)
        k4_heads.append(jnp.concatenate([
            jnp.concatenate([zeros] * c + [k_out_h[c * B_CHUNK:(c + 1) * B_CHUNK]] + [zeros] * (nchunk - 1 - c),
                            axis=0) for c in range(nchunk)], axis=1))
    return q_slabs, k_in_heads, k4_heads, dec


def _gla_kernel(kf_ref, vf_ref, qf_ref, rf_ref, kb_ref, vb_ref, qb_ref, rb_ref, w2_ref, b_ref,
                tri_ref, keep_ref, of_ref, ob_ref, sf_ref, sb_ref, a_ref, u_ref):
    @pl.when(pl.program_id(1) == 0)
    def _():
        sf_ref[...] = jnp.zeros_like(sf_ref)
        sb_ref[...] = jnp.zeros_like(sb_ref)

    heads = sf_ref.shape[0]
    nchunk = kf_ref.shape[0] // B_CHUNK
    v_refs, o_refs, st_refs = (vf_ref, vb_ref), (of_ref, ob_ref), (sf_ref, sb_ref)
    prep = [_gla_prep(kf_ref, qf_ref, rf_ref, w2_ref[0], b_ref[0], tri_ref.at[0], False),
            _gla_prep(kb_ref, qb_ref, rb_ref, w2_ref[1], b_ref[1], tri_ref.at[1], True)]
    jobs = [(d, h) for d in range(2) for h in range(heads)]
    for d, h in jobs:
        q_slabs, k_in_heads, k4_heads, _ = prep[d]
        v_h = v_refs[d][:, h * B_DV:(h + 1) * B_DV]
        a_ref[d * heads + h] = _dot_nt(q_slabs[h], k_in_heads[h])
        u_ref[d * heads + h] = _dot_tn(v_h, k4_heads[h])
    for d, h in jobs:
        vcols = slice(h * B_DV, (h + 1) * B_DV)
        attn = jnp.where(keep_ref[d] > 0.5, a_ref[d * heads + h], 0.0).astype(BF16)
        o_refs[d][:, vcols] = _dot(attn, v_refs[d][:, vcols])
    for d, h in jobs:
        q_s, dec = prep[d][0][h], prep[d][3]
        slab = slice((h // 2) * LANES, (h // 2 + 1) * LANES)
        vcols = slice(h * B_DV, (h + 1) * B_DV)
        st = st_refs[d][h]
        for c in (range(nchunk - 1, -1, -1) if d else range(nchunk)):
            rows = slice(c * B_CHUNK, (c + 1) * B_CHUNK)
            o_refs[d][rows, vcols] += _dot_nt(q_s[rows], st.astype(BF16))
            st = st * dec[c][:, slab] + u_ref[d * heads + h, :, c * LANES:(c + 1) * LANES]
        st_refs[d][h] = st


def _chunk_triangles(tb):
    t = np.arange(tb)[:, None]
    s = np.arange(tb)[None, :]
    same = (t // B_CHUNK) == (s // B_CHUNK)
    return np.stack([same & (s <= t), same & (s >= t)]).astype(np.float32)


def _gla(z, r, w2pad, bias, batch, seq, ctx_len, tb):
    n_rows = z.shape[0]
    heads = bias.shape[-1] // B_DK
    assert heads % 2 == 0 and 2 * B_DK == LANES
    kw, vw = heads * B_DK, heads * B_DV
    nlat, nctx = seq // tb, ctx_len // tb
    base = batch * nlat
    tri = _chunk_triangles(tb)

    def fwd(b, j):
        return jnp.where(j < nctx, base + b * nctx + j, b * nlat + (j - nctx))

    def bwd(b, j):
        return jnp.where(j < nctx, base + b * nctx + (nctx - 1 - j), b * nlat + (nlat - 1 - (j - nctx)))

    def specs(blk):
        return [pl.BlockSpec((tb, kw), lambda b, j: (blk(b, j), Z_BK // kw)),
                pl.BlockSpec((tb, vw), lambda b, j: (blk(b, j), Z_BV // vw)),
                pl.BlockSpec((tb, kw), lambda b, j: (blk(b, j), Z_BQ // kw)),
                pl.BlockSpec((tb, LANES), lambda b, j: (blk(b, j), 0))]

    return pl.pallas_call(
        _gla_kernel,
        out_shape=(jax.ShapeDtypeStruct((n_rows, vw), F32), jax.ShapeDtypeStruct((n_rows, vw), F32)),
        grid=(batch, nctx + nlat),
        in_specs=specs(fwd) + specs(bwd) + [
            pl.BlockSpec((2, LANES, kw), lambda b, j: (0, 0, 0)),
            pl.BlockSpec((2, 1, kw), lambda b, j: (0, 0, 0)),
            pl.BlockSpec((2, tb, tb), lambda b, j: (0, 0, 0)),
            pl.BlockSpec((2, tb, tb), lambda b, j: (0, 0, 0)),
        ],
        out_specs=(pl.BlockSpec((tb, vw), lambda b, j: (fwd(b, j), 0)),
                   pl.BlockSpec((tb, vw), lambda b, j: (bwd(b, j), 0))),
        scratch_shapes=[pltpu.VMEM((heads, B_DV, LANES), F32), pltpu.VMEM((heads, B_DV, LANES), F32),
                        pltpu.VMEM((2 * heads, tb, tb), F32),
                        pltpu.VMEM((2 * heads, B_DV, (tb // B_CHUNK) * LANES), F32)],
        compiler_params=_cparams(("parallel", "arbitrary")),
        name="gla",
    )(z, z, z, r, z, z, z, r, w2pad.astype(BF16), bias, jnp.asarray(tri, BF16), jnp.asarray(tri, F32))


def _softmax_pv(s_ref, row0, bias_ref, sink, v_all, group, tq, o_ref, head0):
    for g in range(group):
        sg = s_ref[row0 + g * tq:row0 + (g + 1) * tq, :] + bias_ref[...]
        m = jnp.maximum(jnp.max(sg, axis=-1, keepdims=True), sink[g])
        e = jnp.exp2(sg - m)
        den = jnp.sum(e, axis=-1, keepdims=True) + jnp.exp2(sink[g] - m)
        h = head0 + g
        o_ref[:, h * HEAD_DIM:(h + 1) * HEAD_DIM] = (_dot(e.astype(BF16), v_all) / den).astype(o_ref.dtype)


def _attn_kernel(sink_ref, q_ref, kp_ref, kc_ref, kn_ref, kx_ref, vp_ref, vc_ref, vn_ref, vx_ref,
                 bias_ref, o_ref, s_ref):
    tq = q_ref.shape[0]
    n_heads = q_ref.shape[1] // HEAD_DIM
    kv_heads = kc_ref.shape[1] // HEAD_DIM
    group = n_heads // kv_heads
    for kh in range(kv_heads):
        cols = slice(kh * HEAD_DIM, (kh + 1) * HEAD_DIM)
        k_all = jnp.concatenate([kp_ref[:, cols], kc_ref[:, cols], kn_ref[:, cols], kx_ref[:, cols]], axis=0)
        q = jnp.concatenate([q_ref[:, (kh * group + g) * HEAD_DIM:(kh * group + g + 1) * HEAD_DIM]
                             for g in range(group)], axis=0)
        s_ref[kh * group * tq:(kh + 1) * group * tq, :] = _dot_nt(q, k_all)
    for kh in range(kv_heads):
        cols = slice(kh * HEAD_DIM, (kh + 1) * HEAD_DIM)
        v_all = jnp.concatenate([vp_ref[:, cols], vc_ref[:, cols], vn_ref[:, cols], vx_ref[:, cols]], axis=0)
        sink = [sink_ref[kh * group + g] * LOG2E for g in range(group)]
        _softmax_pv(s_ref, kh * group * tq, bias_ref, sink, v_all, group, tq, o_ref, kh * group)


def _window_bias(tq, ctx_len):
    i = np.arange(tq)[:, None]
    j = np.arange(3 * tq + ctx_len)[None, :]
    is_ctx = np.broadcast_to(j >= 3 * tq, (tq, 3 * tq + ctx_len))
    band = ((j >= i) & (j <= i + 2 * tq)) | is_ctx
    cases = [band & (j >= tq), band, band & ((j < 2 * tq) | is_ctx), is_ctx]
    return jnp.asarray(np.where(np.stack(cases), 0.0, NEG_BIG), F32)


def _attn(z, sink, batch, seq, ctx_len, with_ctx):
    n_heads = sink.shape[0]
    qw = n_heads * HEAD_DIM
    kvw = C_KVW
    tq = C_BLOCK
    nb = seq // tq
    cpb = ctx_len // tq if with_ctx else 0
    assert nb >= 2 and ctx_len % tq == 0
    ctx_base = batch * seq // ctx_len

    def own(b, n):
        return jnp.where(n < nb, b * nb + n, batch * nb + b * cpb + (n - nb))

    def prev(b, n):
        return b * nb + jnp.clip(n - 1, 0, nb - 1)

    def cur(b, n):
        return b * nb + jnp.minimum(n, nb - 1)

    def nxt(b, n):
        return b * nb + jnp.minimum(n + 1, nb - 1)

    def zspec(blk, col0):
        return pl.BlockSpec((tq, kvw), lambda b, n: (blk(b, n), col0 // kvw))

    def bias_case(b, n):
        return jnp.where(n >= nb, 3, jnp.where(n == 0, 0, jnp.where(n == nb - 1, 2, 1)))

    return pl.pallas_call(
        _attn_kernel,
        out_shape=jax.ShapeDtypeStruct((batch * (nb + cpb) * tq, qw), BF16),
        grid=(batch, nb + cpb),
        in_specs=[
            pl.BlockSpec(memory_space=pltpu.SMEM),
            pl.BlockSpec((tq, qw), lambda b, n: (own(b, n), Z_CQ // qw)),
            zspec(prev, Z_CK), zspec(cur, Z_CK), zspec(nxt, Z_CK),
            pl.BlockSpec((ctx_len, kvw), lambda b, n: (ctx_base + b, Z_CK // kvw)),
            zspec(prev, Z_CV), zspec(cur, Z_CV), zspec(nxt, Z_CV),
            pl.BlockSpec((ctx_len, kvw), lambda b, n: (ctx_base + b, Z_CV // kvw)),
            pl.BlockSpec((None, tq, 3 * tq + ctx_len), lambda b, n: (bias_case(b, n), 0, 0)),
        ],
        out_specs=pl.BlockSpec((tq, qw), lambda b, n: (own(b, n), 0)),
        scratch_shapes=[pltpu.VMEM((n_heads * tq, 3 * tq + ctx_len), F32)],
        compiler_params=_cparams(("parallel", "parallel")),
        name="window_attn",
    )(sink, z, z, z, z, z, z, z, z, z, _window_bias(tq, ctx_len))


def _merge_kernel(x_ref, gt_ref, a_ref, of_ref, ob_ref, og_ref, c_ref, ga_ref, gb_ref, gc_ref,
                  bg_ref, wa_ref, wb_ref, wc_ref, wo_ref, o_ref):
    heads = of_ref.shape[1] // B_DV
    o = of_ref[...] + ob_ref[...]
    bn = jnp.concatenate([_rms_head(o[:, h * B_DV:(h + 1) * B_DV], bg_ref[...]) for h in range(heads)], axis=1)
    b = (bn * _silu(og_ref[...].astype(F32))).astype(BF16)
    merged = (jax.nn.sigmoid(ga_ref[...].astype(F32)) * _dot(a_ref[...], wa_ref[...])
              + jax.nn.sigmoid(gb_ref[...].astype(F32)) * _dot(b, wb_ref[...])
              + jax.nn.sigmoid(gc_ref[...].astype(F32)) * _dot(c_ref[...], wc_ref[...]))
    mix = _dot(merged.astype(BF16), wo_ref[...])
    o_ref[...] = x_ref[...] + gt_ref[...] * mix


def _merge(x, n_rows, mod, layer, row_fn, z, a, o_f, o_b, c, b_norm_g, wa, wb, wc, wo, tm):
    d = x.shape[1]
    aw, bw, cw = a.shape[1], o_f.shape[1], c.shape[1]

    def const(rows):
        return pl.BlockSpec((None, rows, d), lambda i: (layer, 0, 0), pipeline_mode=pl.Buffered(1))

    return pl.pallas_call(
        _merge_kernel,
        out_shape=jax.ShapeDtypeStruct((n_rows, d), F32),
        grid=(n_rows // tm,),
        in_specs=[
            pl.BlockSpec((tm, d), lambda i: (i, 0)),
            _mod_spec(layer, 5, row_fn, d),
            pl.BlockSpec((tm, aw), lambda i: (i, 0)),
            pl.BlockSpec((tm, bw), lambda i: (i, 0)),
            pl.BlockSpec((tm, bw), lambda i: (i, 0)),
            pl.BlockSpec((tm, bw), lambda i: (i, Z_BG // bw)),
            pl.BlockSpec((tm, cw), lambda i: (i, 0)),
            pl.BlockSpec((tm, d), lambda i: (i, Z_GATE // d)),
            pl.BlockSpec((tm, d), lambda i: (i, Z_GATE // d + 1)),
            pl.BlockSpec((tm, d), lambda i: (i, Z_GATE // d + 2)),
            pl.BlockSpec((1, B_DV), lambda i: (0, 0)),
            const(aw), const(bw), const(cw), const(d),
        ],
        out_specs=pl.BlockSpec((tm, d), lambda i: (i, 0)),
        compiler_params=_cparams(("parallel",)),
        name="merge",
    )(x, mod, a, o_f, o_b, z, c, z, z, z, b_norm_g.reshape(1, B_DV), wa, wb, wc, wo)


def _rope_tables(seq, pad_rows):
    half = HEAD_DIM // 4
    t = np.arange(seq)
    pos = np.stack([t // GRID_W, t % GRID_W], axis=1).astype(np.float32)
    inv_freq = jnp.asarray(ROPE_BASE, F32) ** (-jnp.arange(half, dtype=F32) / half)
    ang = jnp.asarray(pos)[:, :, None] * inv_freq[None, None, :]
    cos = jnp.cos(ang)
    sin = jnp.sin(ang)
    cos_t = jnp.stack([cos, cos], axis=1).reshape(seq, HEAD_DIM)
    sin_t = jnp.stack([-sin, sin], axis=1).reshape(seq, HEAD_DIM)
    cos_t = jnp.concatenate([cos_t, jnp.ones((pad_rows, HEAD_DIM), F32)], axis=0)
    sin_t = jnp.concatenate([sin_t, jnp.zeros((pad_rows, HEAD_DIM), F32)], axis=0)
    return cos_t, sin_t


def _rotary_lane_order(w):
    lead = w.shape[:-1]
    nh = w.shape[-1] // HEAD_DIM
    w = w.reshape(lead + (nh, 2, 2, HEAD_DIM // 4))
    w = jnp.swapaxes(w, -3, -2)
    return w.reshape(lead + (nh * HEAD_DIM,))


def _reorder_w_in(w, d):
    bkw, bvw, ckvw, aw, cqw = 256, 512, 256, 512, 1024
    o_bk = 0
    o_bv = o_bk + bkw
    o_ck = o_bv + bvw
    o_cv = o_ck + ckvw
    o_au = o_cv + ckvw
    o_av = o_au + aw
    o_bq = o_av + aw
    o_bg = o_bq + bkw
    o_cq = o_bg + bvw
    o_gate = o_cq + cqw
    seg = lambda s, width: w[..., s:s + width]
    return jnp.concatenate([_rotary_lane_order(seg(o_cq, cqw)), _rotary_lane_order(seg(o_ck, ckvw)),
                            seg(o_cv, ckvw), seg(o_bk, bkw), seg(o_bq, bkw), seg(o_bv, bvw),
                            seg(o_au, aw), seg(o_av, aw), seg(o_bg, bvw), seg(o_gate, 3 * d)], axis=-1)


def kernel(x, c, ctx, c_ctx, w_ada, b_ada, norm_g, w_ffn_up, w_ffn_down, w_in, a_v_gain, a_ws, a_bs,
           b_decay_w1, b_decay_w2, b_decay_b, b_norm_g, c_q_gain, c_k_gain, c_sink,
           w_br_a, w_br_b, w_br_c, w_out):
    batch, seq, d = x.shape
    ctx_len = ctx.shape[1]
    depth = w_ada.shape[0]
    assert d == 2048 and batch < MOD_ROWS and seq % 256 == 0 and ctx_len == 256
    n_lat, n_ctx = batch * seq, batch * ctx_len
    n_all = n_lat + n_ctx
    tm_ffn, tm_in, tm_merge = 512, 512, 256
    for t in (tm_ffn, tm_in, tm_merge):
        assert seq % t == 0 and n_ctx % t == 0

    def row_fn(t):
        return lambda i: jnp.minimum(i * t // seq, batch)

    c8 = jnp.zeros((MOD_ROWS, d), F32).at[:batch].set(c).at[batch].set(c_ctx)
    mod = _mod_table(c8, w_ada, b_ada)
    cos_t, sin_t = _rope_tables(seq, tm_in)

    w_up = w_ffn_up.astype(BF16)
    w_dn = w_ffn_down.astype(BF16)
    w_in_r = _reorder_w_in(w_in, d).astype(BF16)
    w1 = jnp.zeros((depth, d, LANES), F32).at[:, :, :B_RANK].set(b_decay_w1[:, 0]).at[
        :, :, B_RANK:2 * B_RANK].set(b_decay_w1[:, 1]).astype(BF16)
    kw = b_decay_w2.shape[-1]
    w2pad = jnp.zeros((depth, 2, LANES, kw), F32).at[:, 0, :B_RANK].set(b_decay_w2[:, 0]).at[
        :, 1, B_RANK:2 * B_RANK].set(b_decay_w2[:, 1])
    n_qh = c_sink.shape[1]
    qk_gain = jnp.concatenate([jnp.tile(_rotary_lane_order(c_q_gain) * (HEAD_DIM ** -0.5 * LOG2E), (1, n_qh)),
                               jnp.tile(_rotary_lane_order(c_k_gain), (1, C_KVW // HEAD_DIM))], axis=1)
    wa, wb, wc, wo = (w.astype(BF16) for w in (w_br_a, w_br_b, w_br_c, w_out))

    srcs = [x.reshape(n_lat, d), ctx.reshape(n_ctx, d)]
    for l in range(depth):
        last = l == depth - 1
        gains = norm_g[l].reshape(3, 1, d)
        n_mix = n_lat if last else n_all
        xs = _ffn(srcs, n_all, mod, l, 0, row_fn(tm_ffn), gains[0], w_up, w_dn, tm_ffn, 512)
        z, r = _inproj(xs, mod, l, row_fn(tm_in), gains[1], w_in_r, w1, qk_gain[l:l + 1], cos_t, sin_t,
                       n_lat, seq, tm_in, 2 * QK_COLS)
        a_out = _gmlp(z, n_mix, a_v_gain[l], a_ws[l], a_bs[l], 512)
        o_f, o_b = _gla(z, r, w2pad[l], b_decay_b[l].reshape(2, 1, kw), batch, seq, ctx_len, 256)
        c_out = _attn(z, c_sink[l], batch, seq, ctx_len, with_ctx=not last)
        xs = _merge(xs, n_mix, mod, l, row_fn(tm_merge), z, a_out, o_f, o_b, c_out, b_norm_g[l],
                    wa, wb, wc, wo, tm_merge)
        xs = _ffn([xs], n_mix, mod, l, 1, row_fn(tm_ffn), gains[2], w_up, w_dn, tm_ffn, 512)
        srcs = [xs]
    return xs.reshape(batch, seq, d)
```

```python
import functools

import jax
import jax.numpy as jnp
import numpy as np
from jax import lax
from jax.experimental import pallas as pl
from jax.experimental.pallas import tpu as pltpu

F32 = jnp.float32
BF16 = jnp.bfloat16

HEAD_DIM = 128
EPS = 1e-6
N_MOD = 9
GRID_W = 64
A_CHUNK = 128
B_DK = 64
B_DV = 128
B_RANK = 16
B_TAU = 16.0
B_CHUNK = 64
C_BLOCK = 128
ROPE_BASE = 10000.0
MOD_ROWS = 8
LANES = 128
NEG_BIG = -1e30
LOG2E = 1.4426950408889634
LN2 = 0.6931471805599453
VMEM_LIMIT = 56 * 1024 * 1024

Z_CQ, Z_CK, Z_CV, Z_BK, Z_BQ, Z_BV, Z_AU, Z_AV, Z_BG, Z_GATE = (
    0, 1024, 1280, 1536, 1792, 2048, 2560, 3072, 3584, 4096)
QK_COLS = Z_CV
C_KVW = Z_CV - Z_CK


def _cparams(sem):
    return pltpu.CompilerParams(dimension_semantics=sem, vmem_limit_bytes=VMEM_LIMIT)


def _dot(a, b):
    return jnp.dot(a, b, preferred_element_type=F32)


def _dot_nt(a, b):
    return lax.dot_general(a, b, (((1,), (1,)), ((), ())), preferred_element_type=F32)


def _dot_tn(a, b):
    return lax.dot_general(a, b, (((0,), (0,)), ((), ())), preferred_element_type=F32)


def _dot_f32(a, b):
    return jnp.dot(a, b, preferred_element_type=F32, precision=lax.Precision.HIGHEST)


def _silu(x):
    return x * jax.nn.sigmoid(x)


def _rms_mod(x, gain, shift, scale):
    ms = jnp.mean(x * x, axis=-1, keepdims=True)
    y = x * lax.rsqrt(ms + EPS) * gain
    return y * (1.0 + scale) + shift


NORM_ROWS = 64


def _rms_mod_to(x_ref, m_ref, h_ref):
    shift = m_ref[0]
    gain = m_ref[3] * (1.0 + m_ref[1])

    def one_pass(c, carry):
        rows = pl.ds(pl.multiple_of(c * NORM_ROWS, NORM_ROWS), NORM_ROWS)
        x = x_ref[rows, :]
        rs = lax.rsqrt(jnp.mean(x * x, axis=-1, keepdims=True) + EPS)
        h_ref[rows, :] = (x * rs * gain + shift).astype(BF16)
        return carry

    lax.fori_loop(0, x_ref.shape[0] // NORM_ROWS, one_pass, 0, unroll=2)


def _rms_head(xh, gain):
    ms = jnp.mean(xh * xh, axis=-1, keepdims=True)
    return xh * lax.rsqrt(ms + EPS) * gain


def _mod_kernel(c_ref, w_ref, b_ref, o_ref):
    act = _silu(c_ref[...])
    o_ref[...] = _dot(act.astype(BF16), w_ref[...].astype(BF16)) + b_ref[...]


def _mod_table(c8, w_ada, b_ada):
    depth, d, nd = w_ada.shape
    tn = 1024
    nj = d // tn
    out = pl.pallas_call(
        _mod_kernel,
        out_shape=jax.ShapeDtypeStruct((depth, N_MOD, MOD_ROWS, d), F32),
        grid=(depth, N_MOD, nj),
        in_specs=[
            pl.BlockSpec((MOD_ROWS, d), lambda l, k, j: (0, 0)),
            pl.BlockSpec((None, d, tn), lambda l, k, j: (l, 0, k * nj + j)),
            pl.BlockSpec((None, 1, tn), lambda l, k, j: (l, 0, k * nj + j)),
        ],
        out_specs=pl.BlockSpec((None, None, MOD_ROWS, tn), lambda l, k, j: (l, k, 0, j)),
        compiler_params=_cparams(("parallel", "parallel", "parallel")),
        name="mod_table",
    )(c8, w_ada, b_ada.reshape(depth, 1, nd))
    return out


MOD_GROUP = 4


def _with_norm_gains(mod, norm_g):
    depth, _, rows, d = mod.shape
    subs = norm_g.shape[1]
    gains = jnp.broadcast_to(norm_g[:, :, None, None, :], (depth, subs, 1, rows, d))
    table = jnp.concatenate([mod.reshape(depth, subs, N_MOD // subs, rows, d), gains], axis=2)
    return table.reshape(depth, subs * MOD_GROUP, rows, 1, d)


def _mod_spec(layer, sub, row_fn, d):
    return pl.BlockSpec((None, MOD_GROUP, None, 1, d), lambda i, *_: (layer, sub, row_fn(i), 0, 0))


def _ffn_kernel(*refs, tiles_per_source):
    n_src = len(tiles_per_source)
    x_refs = refs[:n_src]
    m_ref, wg_ref, wv_ref, wd_ref, o_ref, h_ref, acc_ref = refs[n_src:]
    i, f = pl.program_id(0), pl.program_id(1)

    def with_own_source(fn):
        start = 0
        for x_ref, n_tiles in zip(x_refs, tiles_per_source):
            if n_src == 1:
                fn(x_ref)
            else:
                pl.when(jnp.logical_and(i >= start, i < start + n_tiles))(lambda x_ref=x_ref: fn(x_ref))
            start += n_tiles

    def prologue(x_ref):
        _rms_mod_to(x_ref, m_ref, h_ref)

    def epilogue(x_ref):
        o_ref[...] = x_ref[...] + 0.5 * m_ref[2] * acc_ref[...]

    @pl.when(f == 0)
    def _():
        with_own_source(prologue)
        acc_ref[...] = jnp.zeros_like(acc_ref)

    h = h_ref[...]
    gate = _dot(h, wg_ref[...])
    val = _dot(h, wv_ref[...])
    act = (_silu(gate) * val).astype(BF16)
    acc_ref[...] += _dot(act, wd_ref[...])

    @pl.when(f == pl.num_programs(1) - 1)
    def _():
        with_own_source(epilogue)


def _ffn(xs, n_rows, mod, layer, which, row_fn, w_up, w_down, tm, tf):
    d = xs[0].shape[1]
    ffn = w_down.shape[2]
    nf = ffn // tf
    nt = n_rows // tm
    tiles, starts = [], []
    for x in xs:
        starts.append(sum(tiles))
        tiles.append(min(x.shape[0] // tm, nt - sum(tiles)))

    def src_spec(start, n_tiles):
        return pl.BlockSpec((tm, d), lambda i, f: (jnp.clip(i - start, 0, n_tiles - 1), 0))

    return pl.pallas_call(
        functools.partial(_ffn_kernel, tiles_per_source=tuple(tiles)),
        out_shape=jax.ShapeDtypeStruct((n_rows, d), F32),
        grid=(nt, nf),
        in_specs=[src_spec(s, t) for s, t in zip(starts, tiles)] + [
            _mod_spec(layer, 2 * which, row_fn, d),
            pl.BlockSpec((None, None, d, tf), lambda i, f: (layer, which, 0, f)),
            pl.BlockSpec((None, None, d, tf), lambda i, f: (layer, which, 0, nf + f)),
            pl.BlockSpec((None, None, tf, d), lambda i, f: (layer, which, f, 0)),
        ],
        out_specs=pl.BlockSpec((tm, d), lambda i, f: (i, 0)),
        scratch_shapes=[pltpu.VMEM((tm, d), BF16), pltpu.VMEM((tm, d), F32)],
        compiler_params=_cparams(("parallel", "arbitrary")),
        name="ffn",
    )(*xs, mod, w_up, w_up, w_down)


def _inproj_kernel(x_ref, m_ref, w_ref, w1_ref, qkg_ref, cos_ref, sin_ref, z_ref, r_ref, h_ref):
    n = pl.program_id(1)

    @pl.when(n == 0)
    def _():
        _rms_mod_to(x_ref, m_ref, h_ref)
        r_ref[...] = _dot(h_ref[...], w1_ref[...])

    zt = _dot(h_ref[...], w_ref[...])

    @pl.when(n == 0)
    def _():
        pair = 2 * HEAD_DIM
        r = lax.broadcasted_iota(jnp.int32, (pair, pair), 0)
        c = lax.broadcasted_iota(jnp.int32, (pair, pair), 1)
        head_sum = jnp.where(jnp.bitwise_and(jnp.bitwise_xor(r, c), HEAD_DIM) == 0, 1.0, 0.0).astype(BF16)
        swap = jnp.where(c == jnp.bitwise_xor(r, HEAD_DIM // 2), 1.0, 0.0).astype(BF16)
        cos = jnp.concatenate([cos_ref[...]] * 2, axis=1)
        sin = jnp.concatenate([sin_ref[...]] * 2, axis=1)
        for p in range(QK_COLS // pair):
            cols = slice(p * pair, (p + 1) * pair)
            y = zt[:, cols]
            ms = _dot((y * y).astype(BF16), head_sum) * (1.0 / HEAD_DIM)
            y = y * lax.rsqrt(ms + EPS) * qkg_ref[:, cols]
            z_ref[:, cols] = (y * cos + _dot(y.astype(BF16), swap) * sin).astype(z_ref.dtype)
        if zt.shape[1] > QK_COLS:
            z_ref[:, QK_COLS:] = zt[:, QK_COLS:].astype(z_ref.dtype)

    @pl.when(n != 0)
    def _():
        z_ref[...] = zt.astype(z_ref.dtype)


def _inproj(x, mod, layer, row_fn, w_in, w1, qk_gain, cos_t, sin_t, n_lat, seq, tm, tn):
    n_rows, d = x.shape
    ncols = w_in.shape[2]
    assert tn % QK_COLS == 0 and ncols % tn == 0
    rope_blocks = seq // tm

    def rope_blk(i, n):
        return jnp.where(i * tm < n_lat, (i % rope_blocks), rope_blocks)

    return pl.pallas_call(
        _inproj_kernel,
        out_shape=(jax.ShapeDtypeStruct((n_rows, ncols), BF16),
                   jax.ShapeDtypeStruct((n_rows, LANES), F32)),
        grid=(n_rows // tm, ncols // tn),
        in_specs=[
            pl.BlockSpec((tm, d), lambda i, n: (i, 0)),
            _mod_spec(layer, 1, row_fn, d),
            pl.BlockSpec((None, d, tn), lambda i, n: (layer, 0, n)),
            pl.BlockSpec((None, d, LANES), lambda i, n: (layer, 0, 0)),
            pl.BlockSpec((1, QK_COLS), lambda i, n: (0, 0)),
            pl.BlockSpec((tm, HEAD_DIM), lambda i, n: (rope_blk(i, n), 0)),
            pl.BlockSpec((tm, HEAD_DIM), lambda i, n: (rope_blk(i, n), 0)),
        ],
        out_specs=(pl.BlockSpec((tm, tn), lambda i, n: (i, n)),
                   pl.BlockSpec((tm, LANES), lambda i, n: (i, 0))),
        scratch_shapes=[pltpu.VMEM((tm, d), BF16)],
        compiler_params=_cparams(("parallel", "arbitrary")),
        name="inproj",
    )(x, mod, w_in, w1, qk_gain, cos_t, sin_t)


def _gmlp_kernel(u_ref, v_ref, gain_ref, ws_ref, bs_ref, o_ref):
    tg, width = u_ref.shape
    groups = width // HEAD_DIM
    u = jax.nn.gelu(u_ref[...].astype(F32))
    v = jax.nn.gelu(v_ref[...].astype(F32))
    mu = jnp.mean(v, axis=-1, keepdims=True)
    vc = v - mu
    var = jnp.mean(vc * vc, axis=-1, keepdims=True)
    vn = (vc * lax.rsqrt(var + EPS) * gain_ref[...]).astype(BF16)
    nchunk = tg // A_CHUNK
    for g in range(groups):
        cols = slice(g * HEAD_DIM, (g + 1) * HEAD_DIM)
        v_g = jnp.concatenate([vn[c * A_CHUNK:(c + 1) * A_CHUNK, cols] for c in range(nchunk)], axis=1)
        mixed = _dot(ws_ref[g], v_g)
        for c in range(nchunk):
            rows = slice(c * A_CHUNK, (c + 1) * A_CHUNK)
            o_ref[rows, cols] = (u[rows, cols] * (mixed[:, c * HEAD_DIM:(c + 1) * HEAD_DIM] + bs_ref[g])
                                 ).astype(o_ref.dtype)


def _gmlp(z, n_rows, a_v_gain, a_ws, a_bs, tg):
    groups = a_ws.shape[0]
    width = groups * HEAD_DIM
    bs_full = jnp.broadcast_to(a_bs[:, :, None], (groups, A_CHUNK, HEAD_DIM)).astype(F32)
    return pl.pallas_call(
        _gmlp_kernel,
        out_shape=jax.ShapeDtypeStruct((n_rows, width), BF16),
        grid=(n_rows // tg,),
        in_specs=[
            pl.BlockSpec((tg, width), lambda i: (i, Z_AU // width)),
            pl.BlockSpec((tg, width), lambda i: (i, Z_AV // width)),
            pl.BlockSpec((1, width), lambda i: (0, 0)),
            pl.BlockSpec((groups, A_CHUNK, A_CHUNK), lambda i: (0, 0, 0)),
            pl.BlockSpec((groups, A_CHUNK, HEAD_DIM), lambda i: (0, 0, 0)),
        ],
        out_specs=pl.BlockSpec((tg, width), lambda i: (i, 0)),
        compiler_params=_cparams(("parallel",)),
        name="gmlp",
    )(z, z, a_v_gain.reshape(1, width), a_ws.astype(BF16), bs_full)


def _gla_prep(k_ref, q_ref, r_ref, w2, bias, tri_ref, reverse):
    tb = k_ref.shape[0]
    heads = k_ref.shape[1] // B_DK
    nchunk = tb // B_CHUNK
    logit = _dot(r_ref[...].astype(BF16), w2) + bias
    soft = jnp.log2(1.0 + jnp.exp2(jnp.abs(logit) * -LOG2E))
    g = jnp.minimum(logit, 0.0) * (1.0 / B_TAU) - soft * (LN2 / B_TAU)
    g_hi = g.astype(BF16)
    g_lo = (g - g_hi.astype(F32)).astype(BF16)
    tri = tri_ref[...]
    bc = _dot(tri, g_hi) + _dot(tri, g_lo)
    last = [c * B_CHUNK + (0 if reverse else B_CHUNK - 1) for c in range(nchunk)]
    tots = [bc[r:r + 1, :] for r in last]
    tot_rows = jnp.concatenate([jnp.broadcast_to(t, (B_CHUNK, t.shape[1])) for t in tots], axis=0)
    kf = k_ref[...].astype(F32)
    q_in = (q_ref[...].astype(F32) * jnp.exp(bc) * (B_DK ** -0.5)).astype(BF16)
    k_in = kf * jnp.exp(-bc)
    k_out = kf * jnp.exp(tot_rows - bc)
    dec = [jnp.exp(t) for t in tots]
    lane = lax.broadcasted_iota(jnp.int32, (tb, LANES), 1)
    zeros = jnp.zeros((B_CHUNK, LANES), BF16)
    q_slabs, k_in_heads, k4_heads = [], [], []
    for h in range(heads):
        slab = slice((h // 2) * LANES, (h // 2 + 1) * LANES)
        mine = (lane >= B_DK) if h % 2 else (lane < B_DK)
        q_slabs.append(q_in[:, slab])
        k_in_heads.append(jnp.where(mine, k_in[:, slab], 0.0).astype(BF16))
        k_out_h = jnp.where(mine, k_out[:, slab], 0.0).astype(BF16)
        k4_heads.append(jnp.concatenate([
            jnp.concatenate([zeros] * c + [k_out_h[c * B_CHUNK:(c + 1) * B_CHUNK]] + [zeros] * (nchunk - 1 - c),
                            axis=0) for c in range(nchunk)], axis=1))
    return q_slabs, k_in_heads, k4_heads, dec


def _gla_kernel(kf_ref, vf_ref, qf_ref, rf_ref, kb_ref, vb_ref, qb_ref, rb_ref, w2_ref, b_ref,
                tri_ref, keep_ref, of_ref, ob_ref, sf_ref, sb_ref, a_ref, u_ref):
    @pl.when(pl.program_id(1) == 0)
    def _():
        sf_ref[...] = jnp.zeros_like(sf_ref)
        sb_ref[...] = jnp.zeros_like(sb_ref)

    heads = sf_ref.shape[0]
    nchunk = kf_ref.shape[0] // B_CHUNK
    v_refs, o_refs, st_refs = (vf_ref, vb_ref), (of_ref, ob_ref), (sf_ref, sb_ref)
    prep = [_gla_prep(kf_ref, qf_ref, rf_ref, w2_ref[0], b_ref[0], tri_ref.at[0], False),
            _gla_prep(kb_ref, qb_ref, rb_ref, w2_ref[1], b_ref[1], tri_ref.at[1], True)]
    jobs = [(d, h) for d in range(2) for h in range(heads)]
    for d, h in jobs:
        q_slabs, k_in_heads, k4_heads, _ = prep[d]
        v_h = v_refs[d][:, h * B_DV:(h + 1) * B_DV]
        a_ref[d * heads + h] = _dot_nt(q_slabs[h], k_in_heads[h])
        u_ref[d * heads + h] = _dot_tn(v_h, k4_heads[h])
    for d, h in jobs:
        vcols = slice(h * B_DV, (h + 1) * B_DV)
        attn = jnp.where(keep_ref[d] > 0.5, a_ref[d * heads + h], 0.0).astype(BF16)
        o_refs[d][:, vcols] = _dot(attn, v_refs[d][:, vcols])
    for d, h in jobs:
        q_s, dec = prep[d][0][h], prep[d][3]
        slab = slice((h // 2) * LANES, (h // 2 + 1) * LANES)
        vcols = slice(h * B_DV, (h + 1) * B_DV)
        st = st_refs[d][h]
        for c in (range(nchunk - 1, -1, -1) if d else range(nchunk)):
            rows = slice(c * B_CHUNK, (c + 1) * B_CHUNK)
            o_refs[d][rows, vcols] += _dot_nt(q_s[rows], st.astype(BF16))
            st = st * dec[c][:, slab] + u_ref[d * heads + h, :, c * LANES:(c + 1) * LANES]
        st_refs[d][h] = st


def _chunk_triangles(tb):
    t = np.arange(tb)[:, None]
    s = np.arange(tb)[None, :]
    same = (t // B_CHUNK) == (s // B_CHUNK)
    return np.stack([same & (s <= t), same & (s >= t)]).astype(np.float32)


def _gla(z, r, w2pad, bias, batch, seq, ctx_len, tb):
    n_rows = z.shape[0]
    heads = bias.shape[-1] // B_DK
    assert heads % 2 == 0 and 2 * B_DK == LANES
    kw, vw = heads * B_DK, heads * B_DV
    nlat, nctx = seq // tb, ctx_len // tb
    base = batch * nlat
    tri = _chunk_triangles(tb)

    def fwd(b, j):
        return jnp.where(j < nctx, base + b * nctx + j, b * nlat + (j - nctx))

    def bwd(b, j):
        return jnp.where(j < nctx, base + b * nctx + (nctx - 1 - j), b * nlat + (nlat - 1 - (j - nctx)))

    def specs(blk):
        return [pl.BlockSpec((tb, kw), lambda b, j: (blk(b, j), Z_BK // kw)),
                pl.BlockSpec((tb, vw), lambda b, j: (blk(b, j), Z_BV // vw)),
                pl.BlockSpec((tb, kw), lambda b, j: (blk(b, j), Z_BQ // kw)),
                pl.BlockSpec((tb, LANES), lambda b, j: (blk(b, j), 0))]

    return pl.pallas_call(
        _gla_kernel,
        out_shape=(jax.ShapeDtypeStruct((n_rows, vw), F32), jax.ShapeDtypeStruct((n_rows, vw), F32)),
        grid=(batch, nctx + nlat),
        in_specs=specs(fwd) + specs(bwd) + [
            pl.BlockSpec((2, LANES, kw), lambda b, j: (0, 0, 0)),
            pl.BlockSpec((2, 1, kw), lambda b, j: (0, 0, 0)),
            pl.BlockSpec((2, tb, tb), lambda b, j: (0, 0, 0)),
            pl.BlockSpec((2, tb, tb), lambda b, j: (0, 0, 0)),
        ],
        out_specs=(pl.BlockSpec((tb, vw), lambda b, j: (fwd(b, j), 0)),
                   pl.BlockSpec((tb, vw), lambda b, j: (bwd(b, j), 0))),
        scratch_shapes=[pltpu.VMEM((heads, B_DV, LANES), F32), pltpu.VMEM((heads, B_DV, LANES), F32),
                        pltpu.VMEM((2 * heads, tb, tb), F32),
                        pltpu.VMEM((2 * heads, B_DV, (tb // B_CHUNK) * LANES), F32)],
        compiler_params=_cparams(("parallel", "arbitrary")),
        name="gla",
    )(z, z, z, r, z, z, z, r, w2pad.astype(BF16), bias, jnp.asarray(tri, BF16), jnp.asarray(tri, F32))


def _softmax_pv(s_ref, row0, bias_ref, sink, v_all, group, tq, o_ref, head0):
    for g in range(group):
        rows = slice(row0 + g * tq, row0 + (g + 1) * tq)
        n_loc = bias_ref.shape[1]
        sg = jnp.concatenate([s_ref[rows, :n_loc] + bias_ref[...], s_ref[rows, n_loc:]], axis=1)
        m = jnp.maximum(jnp.max(sg, axis=-1, keepdims=True), sink[g])
        e = jnp.exp2(sg - m)
        den = jnp.sum(e, axis=-1, keepdims=True) + jnp.exp2(sink[g] - m)
        h = head0 + g
        o_ref[:, h * HEAD_DIM:(h + 1) * HEAD_DIM] = (_dot(e.astype(BF16), v_all) / den).astype(o_ref.dtype)


def _attn_kernel(sink_ref, q_ref, kp_ref, kc_ref, kn_ref, kx_ref, vp_ref, vc_ref, vn_ref, vx_ref,
                 bias_ref, o_ref, s_ref):
    tq = q_ref.shape[0]
    n_heads = q_ref.shape[1] // HEAD_DIM
    kv_heads = kc_ref.shape[1] // HEAD_DIM
    group = n_heads // kv_heads
    for kh in range(kv_heads):
        cols = slice(kh * HEAD_DIM, (kh + 1) * HEAD_DIM)
        k_all = jnp.concatenate([kp_ref[:, cols], kc_ref[:, cols], kn_ref[:, cols], kx_ref[:, cols]], axis=0)
        q = jnp.concatenate([q_ref[:, (kh * group + g) * HEAD_DIM:(kh * group + g + 1) * HEAD_DIM]
                             for g in range(group)], axis=0)
        s_ref[kh * group * tq:(kh + 1) * group * tq, :] = _dot_nt(q, k_all)
    for kh in range(kv_heads):
        cols = slice(kh * HEAD_DIM, (kh + 1) * HEAD_DIM)
        v_all = jnp.concatenate([vp_ref[:, cols], vc_ref[:, cols], vn_ref[:, cols], vx_ref[:, cols]], axis=0)
        sink = [sink_ref[kh * group + g] * LOG2E for g in range(group)]
        _softmax_pv(s_ref, kh * group * tq, bias_ref, sink, v_all, group, tq, o_ref, kh * group)


def _window_bias(tq):
    i = np.arange(tq)[:, None]
    j = np.arange(3 * tq)[None, :]
    band = (j >= i) & (j <= i + 2 * tq)
    cases = [band & (j >= tq), band, band & (j < 2 * tq), np.zeros_like(band)]
    return jnp.asarray(np.where(np.stack(cases), 0.0, NEG_BIG), F32)


def _attn(z, sink, batch, seq, ctx_len, with_ctx):
    n_heads = sink.shape[0]
    qw = n_heads * HEAD_DIM
    kvw = C_KVW
    tq = C_BLOCK
    nb = seq // tq
    cpb = ctx_len // tq if with_ctx else 0
    assert nb >= 2 and ctx_len % tq == 0
    ctx_base = batch * seq // ctx_len

    def own(b, n):
        return jnp.where(n < nb, b * nb + n, batch * nb + b * cpb + (n - nb))

    def prev(b, n):
        return b * nb + jnp.clip(n - 1, 0, nb - 1)

    def cur(b, n):
        return b * nb + jnp.minimum(n, nb - 1)

    def nxt(b, n):
        return b * nb + jnp.minimum(n + 1, nb - 1)

    def zspec(blk, col0):
        return pl.BlockSpec((tq, kvw), lambda b, n: (blk(b, n), col0 // kvw))

    def bias_case(b, n):
        return jnp.where(n >= nb, 3, jnp.where(n == 0, 0, jnp.where(n == nb - 1, 2, 1)))

    return pl.pallas_call(
        _attn_kernel,
        out_shape=jax.ShapeDtypeStruct((batch * (nb + cpb) * tq, qw), BF16),
        grid=(batch, nb + cpb),
        in_specs=[
            pl.BlockSpec(memory_space=pltpu.SMEM),
            pl.BlockSpec((tq, qw), lambda b, n: (own(b, n), Z_CQ // qw)),
            zspec(prev, Z_CK), zspec(cur, Z_CK), zspec(nxt, Z_CK),
            pl.BlockSpec((ctx_len, kvw), lambda b, n: (ctx_base + b, Z_CK // kvw)),
            zspec(prev, Z_CV), zspec(cur, Z_CV), zspec(nxt, Z_CV),
            pl.BlockSpec((ctx_len, kvw), lambda b, n: (ctx_base + b, Z_CV // kvw)),
            pl.BlockSpec((None, tq, 3 * tq), lambda b, n: (bias_case(b, n), 0, 0)),
        ],
        out_specs=pl.BlockSpec((tq, qw), lambda b, n: (own(b, n), 0)),
        scratch_shapes=[pltpu.VMEM((n_heads * tq, 3 * tq + ctx_len), F32)],
        compiler_params=_cparams(("parallel", "parallel")),
        name="window_attn",
    )(sink, z, z, z, z, z, z, z, z, z, _window_bias(tq))


def _merge_kernel(x_ref, m_ref, a_ref, of_ref, ob_ref, og_ref, c_ref, ga_ref, gb_ref, gc_ref,
                  bg_ref, wa_ref, wb_ref, wc_ref, wo_ref, o_ref):
    heads = of_ref.shape[1] // B_DV
    o = of_ref[...] + ob_ref[...]
    bn = jnp.concatenate([_rms_head(o[:, h * B_DV:(h + 1) * B_DV], bg_ref[...]) for h in range(heads)], axis=1)
    b = (bn * _silu(og_ref[...].astype(F32))).astype(BF16)
    merged = (jax.nn.sigmoid(ga_ref[...].astype(F32)) * _dot(a_ref[...], wa_ref[...])
              + jax.nn.sigmoid(gb_ref[...].astype(F32)) * _dot(b, wb_ref[...])
              + jax.nn.sigmoid(gc_ref[...].astype(F32)) * _dot(c_ref[...], wc_ref[...]))
    mix = _dot(merged.astype(BF16), wo_ref[...])
    o_ref[...] = x_ref[...] + m_ref[2] * mix


def _merge(x, n_rows, mod, layer, row_fn, z, a, o_f, o_b, c, b_norm_g, wa, wb, wc, wo, tm):
    d = x.shape[1]
    aw, bw, cw = a.shape[1], o_f.shape[1], c.shape[1]

    def const(rows):
        return pl.BlockSpec((None, rows, d), lambda i: (layer, 0, 0), pipeline_mode=pl.Buffered(1))

    return pl.pallas_call(
        _merge_kernel,
        out_shape=jax.ShapeDtypeStruct((n_rows, d), F32),
        grid=(n_rows // tm,),
        in_specs=[
            pl.BlockSpec((tm, d), lambda i: (i, 0)),
            _mod_spec(layer, 1, row_fn, d),
            pl.BlockSpec((tm, aw), lambda i: (i, 0)),
            pl.BlockSpec((tm, bw), lambda i: (i, 0)),
            pl.BlockSpec((tm, bw), lambda i: (i, 0)),
            pl.BlockSpec((tm, bw), lambda i: (i, Z_BG // bw)),
            pl.BlockSpec((tm, cw), lambda i: (i, 0)),
            pl.BlockSpec((tm, d), lambda i: (i, Z_GATE // d)),
            pl.BlockSpec((tm, d), lambda i: (i, Z_GATE // d + 1)),
            pl.BlockSpec((tm, d), lambda i: (i, Z_GATE // d + 2)),
            pl.BlockSpec((1, B_DV), lambda i: (0, 0)),
            const(aw), const(bw), const(cw), const(d),
        ],
        out_specs=pl.BlockSpec((tm, d), lambda i: (i, 0)),
        compiler_params=_cparams(("parallel",)),
        name="merge",
    )(x, mod, a, o_f, o_b, z, c, z, z, z, b_norm_g.reshape(1, B_DV), wa, wb, wc, wo)


def _rope_tables(seq, pad_rows):
    half = HEAD_DIM // 4
    t = np.arange(seq)
    pos = np.stack([t // GRID_W, t % GRID_W], axis=1).astype(np.float32)
    inv_freq = jnp.asarray(ROPE_BASE, F32) ** (-jnp.arange(half, dtype=F32) / half)
    ang = jnp.asarray(pos)[:, :, None] * inv_freq[None, None, :]
    cos = jnp.cos(ang)
    sin = jnp.sin(ang)
    cos_t = jnp.stack([cos, cos], axis=1).reshape(seq, HEAD_DIM)
    sin_t = jnp.stack([-sin, sin], axis=1).reshape(seq, HEAD_DIM)
    cos_t = jnp.concatenate([cos_t, jnp.ones((pad_rows, HEAD_DIM), F32)], axis=0)
    sin_t = jnp.concatenate([sin_t, jnp.zeros((pad_rows, HEAD_DIM), F32)], axis=0)
    return cos_t, sin_t


def _rotary_lane_order(w):
    lead = w.shape[:-1]
    nh = w.shape[-1] // HEAD_DIM
    w = w.reshape(lead + (nh, 2, 2, HEAD_DIM // 4))
    w = jnp.swapaxes(w, -3, -2)
    return w.reshape(lead + (nh * HEAD_DIM,))


def _reorder_w_in(w, d):
    bkw, bvw, ckvw, aw, cqw = 256, 512, 256, 512, 1024
    o_bk = 0
    o_bv = o_bk + bkw
    o_ck = o_bv + bvw
    o_cv = o_ck + ckvw
    o_au = o_cv + ckvw
    o_av = o_au + aw
    o_bq = o_av + aw
    o_bg = o_bq + bkw
    o_cq = o_bg + bvw
    o_gate = o_cq + cqw
    seg = lambda s, width: w[..., s:s + width]
    return jnp.concatenate([_rotary_lane_order(seg(o_cq, cqw)), _rotary_lane_order(seg(o_ck, ckvw)),
                            seg(o_cv, ckvw), seg(o_bk, bkw), seg(o_bq, bkw), seg(o_bv, bvw),
                            seg(o_au, aw), seg(o_av, aw), seg(o_bg, bvw), seg(o_gate, 3 * d)], axis=-1)


def kernel(x, c, ctx, c_ctx, w_ada, b_ada, norm_g, w_ffn_up, w_ffn_down, w_in, a_v_gain, a_ws, a_bs,
           b_decay_w1, b_decay_w2, b_decay_b, b_norm_g, c_q_gain, c_k_gain, c_sink,
           w_br_a, w_br_b, w_br_c, w_out):
    batch, seq, d = x.shape
    ctx_len = ctx.shape[1]
    depth = w_ada.shape[0]
    assert d == 2048 and batch < MOD_ROWS and seq % 256 == 0 and ctx_len == 256
    n_lat, n_ctx = batch * seq, batch * ctx_len
    n_all = n_lat + n_ctx
    tm_ffn, tm_in, tm_merge = 512, 512, 256
    for t in (tm_ffn, tm_in, tm_merge):
        assert seq % t == 0 and n_ctx % t == 0

    def row_fn(t):
        return lambda i: jnp.minimum(i * t // seq, batch)

    c8 = jnp.zeros((MOD_ROWS, d), F32).at[:batch].set(c).at[batch].set(c_ctx)
    mod = _with_norm_gains(_mod_table(c8, w_ada, b_ada), norm_g)
    cos_t, sin_t = _rope_tables(seq, tm_in)

    w_up = w_ffn_up.astype(BF16)
    w_dn = w_ffn_down.astype(BF16)
    w_in_r = _reorder_w_in(w_in, d).astype(BF16)
    w1 = jnp.zeros((depth, d, LANES), F32).at[:, :, :B_RANK].set(b_decay_w1[:, 0]).at[
        :, :, B_RANK:2 * B_RANK].set(b_decay_w1[:, 1]).astype(BF16)
    kw = b_decay_w2.shape[-1]
    w2pad = jnp.zeros((depth, 2, LANES, kw), F32).at[:, 0, :B_RANK].set(b_decay_w2[:, 0]).at[
        :, 1, B_RANK:2 * B_RANK].set(b_decay_w2[:, 1])
    n_qh = c_sink.shape[1]
    qk_gain = jnp.concatenate([jnp.tile(_rotary_lane_order(c_q_gain) * (HEAD_DIM ** -0.5 * LOG2E), (1, n_qh)),
                               jnp.tile(_rotary_lane_order(c_k_gain), (1, C_KVW // HEAD_DIM))], axis=1)
    wa, wb, wc, wo = (w.astype(BF16) for w in (w_br_a, w_br_b, w_br_c, w_out))

    srcs = [x.reshape(n_lat, d), ctx.reshape(n_ctx, d)]
    for l in range(depth):
        last = l == depth - 1
        n_mix = n_lat if last else n_all
        xs = _ffn(srcs, n_all, mod, l, 0, row_fn(tm_ffn), w_up, w_dn, tm_ffn, 512)
        z, r = _inproj(xs, mod, l, row_fn(tm_in), w_in_r, w1, qk_gain[l:l + 1], cos_t, sin_t,
                       n_lat, seq, tm_in, 2 * QK_COLS)
        a_out = _gmlp(z, n_mix, a_v_gain[l], a_ws[l], a_bs[l], 512)
        o_f, o_b = _gla(z, r, w2pad[l], b_decay_b[l].reshape(2, 1, kw), batch, seq, ctx_len, 256)
        c_out = _attn(z, c_sink[l], batch, seq, ctx_len, with_ctx=not last)
        xs = _merge(xs, n_mix, mod, l, row_fn(tm_merge), z, a_out, o_f, o_b, c_out, b_norm_g[l],
                    wa, wb, wc, wo, tm_merge)
        xs = _ffn([xs], n_mix, mod, l, 1, row_fn(tm_ffn), w_up, w_dn, tm_ffn, 512)
        srcs = [xs]
    return xs.reshape(batch, seq, d)
```

```python
import functools

import jax
import jax.numpy as jnp
import numpy as np
from jax import lax
from jax.experimental import pallas as pl
from jax.experimental.pallas import tpu as pltpu

F32 = jnp.float32
BF16 = jnp.bfloat16

HEAD_DIM = 128
EPS = 1e-6
N_MOD = 9
GRID_W = 64
A_CHUNK = 128
B_DK = 64
B_DV = 128
B_RANK = 16
B_TAU = 16.0
B_CHUNK = 64
C_BLOCK = 128
ROPE_BASE = 10000.0
MOD_ROWS = 8
LANES = 128
NEG_BIG = -1e30
LOG2E = 1.4426950408889634
LN2 = 0.6931471805599453
VMEM_LIMIT = 56 * 1024 * 1024

Z_CQ, Z_CK, Z_CV, Z_BK, Z_BQ, Z_BV, Z_AU, Z_AV, Z_BG, Z_GATE = (
    0, 1024, 1280, 1536, 1792, 2048, 2560, 3072, 3584, 4096)
QK_COLS = Z_CV
C_KVW = Z_CV - Z_CK


def _cparams(sem):
    return pltpu.CompilerParams(dimension_semantics=sem, vmem_limit_bytes=VMEM_LIMIT)


def _dot(a, b):
    return jnp.dot(a, b, preferred_element_type=F32)


def _dot_nt(a, b):
    return lax.dot_general(a, b, (((1,), (1,)), ((), ())), preferred_element_type=F32)


def _dot_tn(a, b):
    return lax.dot_general(a, b, (((0,), (0,)), ((), ())), preferred_element_type=F32)


def _dot_f32(a, b):
    return jnp.dot(a, b, preferred_element_type=F32, precision=lax.Precision.HIGHEST)


def _silu(x):
    return x * jax.nn.sigmoid(x)


def _rms_mod(x, gain, shift, scale):
    ms = jnp.mean(x * x, axis=-1, keepdims=True)
    y = x * lax.rsqrt(ms + EPS) * gain
    return y * (1.0 + scale) + shift


NORM_ROWS = 64


def _rms_mod_to(x_ref, m_ref, h_ref):
    shift = m_ref[0]
    gain = m_ref[3] * (1.0 + m_ref[1])

    def one_pass(c, carry):
        rows = pl.ds(pl.multiple_of(c * NORM_ROWS, NORM_ROWS), NORM_ROWS)
        x = x_ref[rows, :]
        rs = lax.rsqrt(jnp.mean(x * x, axis=-1, keepdims=True) + EPS)
        h_ref[rows, :] = (x * rs * gain + shift).astype(BF16)
        return carry

    lax.fori_loop(0, x_ref.shape[0] // NORM_ROWS, one_pass, 0, unroll=2)


def _rms_head(xh, gain):
    ms = jnp.mean(xh * xh, axis=-1, keepdims=True)
    return xh * lax.rsqrt(ms + EPS) * gain


def _mod_kernel(c_ref, w_ref, b_ref, o_ref):
    act = _silu(c_ref[...])
    o_ref[...] = _dot(act.astype(BF16), w_ref[...].astype(BF16)) + b_ref[...]


def _mod_table(c8, w_ada, b_ada):
    depth, d, nd = w_ada.shape
    tn = 1024
    nj = d // tn
    out = pl.pallas_call(
        _mod_kernel,
        out_shape=jax.ShapeDtypeStruct((depth, N_MOD, MOD_ROWS, d), F32),
        grid=(depth, N_MOD, nj),
        in_specs=[
            pl.BlockSpec((MOD_ROWS, d), lambda l, k, j: (0, 0)),
            pl.BlockSpec((None, d, tn), lambda l, k, j: (l, 0, k * nj + j)),
            pl.BlockSpec((None, 1, tn), lambda l, k, j: (l, 0, k * nj + j)),
        ],
        out_specs=pl.BlockSpec((None, None, MOD_ROWS, tn), lambda l, k, j: (l, k, 0, j)),
        compiler_params=_cparams(("parallel", "parallel", "parallel")),
        name="mod_table",
    )(c8, w_ada, b_ada.reshape(depth, 1, nd))
    return out


MOD_GROUP = 4
TF = 512


def _with_norm_gains(mod, norm_g):
    depth, _, rows, d = mod.shape
    subs = norm_g.shape[1]
    gains = jnp.broadcast_to(norm_g[:, :, None, None, :], (depth, subs, 1, rows, d))
    table = jnp.concatenate([mod.reshape(depth, subs, N_MOD // subs, rows, d), gains], axis=2)
    return table.reshape(depth, subs * MOD_GROUP, rows, 1, d)


def _mod_spec(layer, sub, row_fn, d):
    return pl.BlockSpec((None, MOD_GROUP, None, 1, d), lambda i, *_: (layer, sub, row_fn(i), 0, 0))


def _ffn_kernel(*refs, tiles_per_source):
    n_src = len(tiles_per_source)
    x_refs = refs[:n_src]
    m_ref, wg_ref, wv_ref, wd_ref, o_ref, h_ref, acc_ref = refs[n_src:]
    i, f = pl.program_id(0), pl.program_id(1)

    def with_own_source(fn):
        start = 0
        for x_ref, n_tiles in zip(x_refs, tiles_per_source):
            if n_src == 1:
                fn(x_ref)
            else:
                pl.when(jnp.logical_and(i >= start, i < start + n_tiles))(lambda x_ref=x_ref: fn(x_ref))
            start += n_tiles

    last = pl.num_programs(1) - 1

    def prologue(x_ref):
        _rms_mod_to(x_ref, m_ref, h_ref)

    def chunk():
        h = h_ref[...]
        act = (_silu(_dot(h, wg_ref[...])) * _dot(h, wv_ref[...])).astype(BF16)
        return _dot(act, wd_ref[...])

    @pl.when(f == 0)
    def _():
        with_own_source(prologue)
        acc_ref[...] = chunk()

    if n_src == 1:
        @pl.when(jnp.logical_and(f > 0, f < last))
        def _():
            acc_ref[...] += chunk()

        @pl.when(f == last)
        def _():
            o_ref[...] = x_refs[0][...] + 0.5 * m_ref[2] * (acc_ref[...] + chunk())
    else:
        @pl.when(f > 0)
        def _():
            acc_ref[...] += chunk()

        def epilogue(x_ref):
            o_ref[...] = x_ref[...] + 0.5 * m_ref[2] * acc_ref[...]

        @pl.when(f == last)
        def _():
            with_own_source(epilogue)


def _ffn(xs, n_rows, mod, layer, which, row_fn, w_up, w_down, tm, tf):
    d = xs[0].shape[1]
    nf = w_down.shape[2] // tf
    nt = n_rows // tm
    tiles, starts = [], []
    for x in xs:
        starts.append(sum(tiles))
        tiles.append(min(x.shape[0] // tm, nt - sum(tiles)))

    def src_spec(start, n_tiles):
        return pl.BlockSpec((tm, d), lambda i, f: (jnp.clip(i - start, 0, n_tiles - 1), 0))

    return pl.pallas_call(
        functools.partial(_ffn_kernel, tiles_per_source=tuple(tiles)),
        out_shape=jax.ShapeDtypeStruct((n_rows, d), F32),
        grid=(nt, nf),
        in_specs=[src_spec(s, t) for s, t in zip(starts, tiles)] + [
            _mod_spec(layer, 2 * which, row_fn, d),
            pl.BlockSpec((None, None, d, tf), lambda i, f: (layer, which, 0, f)),
            pl.BlockSpec((None, None, d, tf), lambda i, f: (layer, which, 0, nf + f)),
            pl.BlockSpec((None, None, tf, d), lambda i, f: (layer, which, f, 0)),
        ],
        out_specs=pl.BlockSpec((tm, d), lambda i, f: (i, 0)),
        scratch_shapes=[pltpu.VMEM((tm, d), BF16), pltpu.VMEM((tm, d), F32)],
        compiler_params=_cparams(("parallel", "arbitrary")),
        name="ffn",
    )(*xs, mod, w_up, w_up, w_down)


def _inproj_kernel(x_ref, m_ref, w_ref, w1_ref, qkg_ref, cos_ref, sin_ref, z_ref, r_ref, h_ref):
    n = pl.program_id(1)

    @pl.when(n != 0)
    def _():
        z_ref[...] = _dot(h_ref[...], w_ref[...]).astype(z_ref.dtype)

    @pl.when(n == 0)
    def _():
        _rms_mod_to(x_ref, m_ref, h_ref)
        h = h_ref[...]
        r_ref[...] = _dot(h, w1_ref[...])
        zt = _dot(h, w_ref[:, :QK_COLS])
        if w_ref.shape[1] > QK_COLS:
            z_ref[:, QK_COLS:] = _dot(h, w_ref[:, QK_COLS:]).astype(z_ref.dtype)
        pair = 2 * HEAD_DIM
        r = lax.broadcasted_iota(jnp.int32, (pair, pair), 0)
        c = lax.broadcasted_iota(jnp.int32, (pair, pair), 1)
        head_sum = jnp.where(jnp.bitwise_and(jnp.bitwise_xor(r, c), HEAD_DIM) == 0, 1.0, 0.0).astype(BF16)
        swap = jnp.where(c == jnp.bitwise_xor(r, HEAD_DIM // 2), 1.0, 0.0).astype(BF16)
        cos = jnp.concatenate([cos_ref[...]] * 2, axis=1)
        sin = jnp.concatenate([sin_ref[...]] * 2, axis=1)
        for p in range(QK_COLS // pair):
            cols = slice(p * pair, (p + 1) * pair)
            y = zt[:, cols]
            ms = _dot((y * y).astype(BF16), head_sum) * (1.0 / HEAD_DIM)
            y = y * lax.rsqrt(ms + EPS) * qkg_ref[:, cols]
            z_ref[:, cols] = (y * cos + _dot(y.astype(BF16), swap) * sin).astype(z_ref.dtype)


def _inproj(x, mod, layer, row_fn, w_in, w1, qk_gain, cos_t, sin_t, n_lat, seq, tm, tn):
    n_rows, d = x.shape
    ncols = w_in.shape[2]
    assert tn % QK_COLS == 0 and ncols % tn == 0
    rope_blocks = seq // tm

    def rope_blk(i, n):
        return jnp.where(i * tm < n_lat, (i % rope_blocks), rope_blocks)

    return pl.pallas_call(
        _inproj_kernel,
        out_shape=(jax.ShapeDtypeStruct((n_rows, ncols), BF16),
                   jax.ShapeDtypeStruct((n_rows, LANES), F32)),
        grid=(n_rows // tm, ncols // tn),
        in_specs=[
            pl.BlockSpec((tm, d), lambda i, n: (i, 0)),
            _mod_spec(layer, 1, row_fn, d),
            pl.BlockSpec((None, d, tn), lambda i, n: (layer, 0, n)),
            pl.BlockSpec((None, d, LANES), lambda i, n: (layer, 0, 0)),
            pl.BlockSpec((1, QK_COLS), lambda i, n: (0, 0)),
            pl.BlockSpec((tm, HEAD_DIM), lambda i, n: (rope_blk(i, n), 0)),
            pl.BlockSpec((tm, HEAD_DIM), lambda i, n: (rope_blk(i, n), 0)),
        ],
        out_specs=(pl.BlockSpec((tm, tn), lambda i, n: (i, n)),
                   pl.BlockSpec((tm, LANES), lambda i, n: (i, 0))),
        scratch_shapes=[pltpu.VMEM((tm, d), BF16)],
        compiler_params=_cparams(("parallel", "arbitrary")),
        name="inproj",
    )(x, mod, w_in, w1, qk_gain, cos_t, sin_t)


def _gmlp_kernel(u_ref, v_ref, gain_ref, ws_ref, bs_ref, o_ref):
    tg, width = u_ref.shape
    groups = width // HEAD_DIM
    u = jax.nn.gelu(u_ref[...].astype(F32))
    v = jax.nn.gelu(v_ref[...].astype(F32))
    mu = jnp.mean(v, axis=-1, keepdims=True)
    vc = v - mu
    var = jnp.mean(vc * vc, axis=-1, keepdims=True)
    vn = (vc * lax.rsqrt(var + EPS) * gain_ref[...]).astype(BF16)
    nchunk = tg // A_CHUNK
    for g in range(groups):
        cols = slice(g * HEAD_DIM, (g + 1) * HEAD_DIM)
        v_g = jnp.concatenate([vn[c * A_CHUNK:(c + 1) * A_CHUNK, cols] for c in range(nchunk)], axis=1)
        mixed = _dot(ws_ref[g], v_g)
        for c in range(nchunk):
            rows = slice(c * A_CHUNK, (c + 1) * A_CHUNK)
            o_ref[rows, cols] = (u[rows, cols] * (mixed[:, c * HEAD_DIM:(c + 1) * HEAD_DIM] + bs_ref[g])
                                 ).astype(o_ref.dtype)


def _gmlp(z, n_rows, a_v_gain, a_ws, a_bs, tg):
    groups = a_ws.shape[0]
    width = groups * HEAD_DIM
    bs_full = jnp.broadcast_to(a_bs[:, :, None], (groups, A_CHUNK, HEAD_DIM)).astype(F32)
    return pl.pallas_call(
        _gmlp_kernel,
        out_shape=jax.ShapeDtypeStruct((n_rows, width), BF16),
        grid=(n_rows // tg,),
        in_specs=[
            pl.BlockSpec((tg, width), lambda i: (i, Z_AU // width)),
            pl.BlockSpec((tg, width), lambda i: (i, Z_AV // width)),
            pl.BlockSpec((1, width), lambda i: (0, 0)),
            pl.BlockSpec((groups, A_CHUNK, A_CHUNK), lambda i: (0, 0, 0)),
            pl.BlockSpec((groups, A_CHUNK, HEAD_DIM), lambda i: (0, 0, 0)),
        ],
        out_specs=pl.BlockSpec((tg, width), lambda i: (i, 0)),
        compiler_params=_cparams(("parallel",)),
        name="gmlp",
    )(z, z, a_v_gain.reshape(1, width), a_ws.astype(BF16), bs_full)


def _gla_prep(k_ref, q_ref, r_ref, w2, bias, tri_ref, reverse):
    tb = k_ref.shape[0]
    heads = k_ref.shape[1] // B_DK
    nchunk = tb // B_CHUNK
    logit = _dot(r_ref[...].astype(BF16), w2) + bias
    soft = jnp.log2(1.0 + jnp.exp2(jnp.abs(logit) * -LOG2E))
    g = jnp.minimum(logit, 0.0) * (1.0 / B_TAU) - soft * (LN2 / B_TAU)
    g_hi = g.astype(BF16)
    g_lo = (g - g_hi.astype(F32)).astype(BF16)
    tri = tri_ref[...]
    bc = _dot(tri, g_hi) + _dot(tri, g_lo)
    last = [c * B_CHUNK + (0 if reverse else B_CHUNK - 1) for c in range(nchunk)]
    tots = [bc[r:r + 1, :] for r in last]
    tot_rows = jnp.concatenate([jnp.broadcast_to(t, (B_CHUNK, t.shape[1])) for t in tots], axis=0)
    kf = k_ref[...].astype(F32)
    q_in = (q_ref[...].astype(F32) * jnp.exp(bc) * (B_DK ** -0.5)).astype(BF16)
    k_in = kf * jnp.exp(-bc)
    k_out = kf * jnp.exp(tot_rows - bc)
    dec = [jnp.exp(t) for t in tots]
    lane = lax.broadcasted_iota(jnp.int32, (tb, LANES), 1)
    zeros = jnp.zeros((B_CHUNK, LANES), BF16)
    q_slabs, k_in_heads, k4_heads = [], [], []
    for h in range(heads):
        slab = slice((h // 2) * LANES, (h // 2 + 1) * LANES)
        mine = (lane >= B_DK) if h % 2 else (lane < B_DK)
        q_slabs.append(q_in[:, slab])
        k_in_heads.append(jnp.where(mine, k_in[:, slab], 0.0).astype(BF16))
        k_out_h = jnp.where(mine, k_out[:, slab], 0.0).astype(BF16)
        k4_heads.append(jnp.concatenate([
            jnp.concatenate([zeros] * c + [k_out_h[c * B_CHUNK:(c + 1) * B_CHUNK]] + [zeros] * (nchunk - 1 - c),
                            axis=0) for c in range(nchunk)], axis=1))
    return q_slabs, k_in_heads, k4_heads, dec


def _gla_kernel(kf_ref, vf_ref, qf_ref, rf_ref, kb_ref, vb_ref, qb_ref, rb_ref, w2_ref, b_ref,
                tri_ref, keep_ref, of_ref, ob_ref, sf_ref, sb_ref, a_ref, u_ref):
    @pl.when(pl.program_id(1) == 0)
    def _():
        sf_ref[...] = jnp.zeros_like(sf_ref)
        sb_ref[...] = jnp.zeros_like(sb_ref)

    heads = sf_ref.shape[0]
    nchunk = kf_ref.shape[0] // B_CHUNK
    v_refs, o_refs, st_refs = (vf_ref, vb_ref), (of_ref, ob_ref), (sf_ref, sb_ref)
    prep = [_gla_prep(kf_ref, qf_ref, rf_ref, w2_ref[0], b_ref[0], tri_ref.at[0], False),
            _gla_prep(kb_ref, qb_ref, rb_ref, w2_ref[1], b_ref[1], tri_ref.at[1], True)]
    jobs = [(d, h) for d in range(2) for h in range(heads)]
    for d, h in jobs:
        q_slabs, k_in_heads, k4_heads, _ = prep[d]
        v_h = v_refs[d][:, h * B_DV:(h + 1) * B_DV]
        a_ref[d * heads + h] = _dot_nt(q_slabs[h], k_in_heads[h])
        u_ref[d * heads + h] = _dot_tn(v_h, k4_heads[h])
    for d, h in jobs:
        vcols = slice(h * B_DV, (h + 1) * B_DV)
        attn = jnp.where(keep_ref[d] > 0.5, a_ref[d * heads + h], 0.0).astype(BF16)
        o_refs[d][:, vcols] = _dot(attn, v_refs[d][:, vcols])
    for d, h in jobs:
        q_s, dec = prep[d][0][h], prep[d][3]
        slab = slice((h // 2) * LANES, (h // 2 + 1) * LANES)
        vcols = slice(h * B_DV, (h + 1) * B_DV)
        st = st_refs[d][h]
        for c in (range(nchunk - 1, -1, -1) if d else range(nchunk)):
            rows = slice(c * B_CHUNK, (c + 1) * B_CHUNK)
            o_refs[d][rows, vcols] += _dot_nt(q_s[rows], st.astype(BF16))
            st = st * dec[c][:, slab] + u_ref[d * heads + h, :, c * LANES:(c + 1) * LANES]
        st_refs[d][h] = st


def _chunk_triangles(tb):
    t = np.arange(tb)[:, None]
    s = np.arange(tb)[None, :]
    same = (t // B_CHUNK) == (s // B_CHUNK)
    return np.stack([same & (s <= t), same & (s >= t)]).astype(np.float32)


def _gla(z, r, w2pad, bias, batch, seq, ctx_len, tb):
    n_rows = z.shape[0]
    heads = bias.shape[-1] // B_DK
    assert heads % 2 == 0 and 2 * B_DK == LANES
    kw, vw = heads * B_DK, heads * B_DV
    nlat, nctx = seq // tb, ctx_len // tb
    base = batch * nlat
    tri = _chunk_triangles(tb)

    def fwd(b, j):
        return jnp.where(j < nctx, base + b * nctx + j, b * nlat + (j - nctx))

    def bwd(b, j):
        return jnp.where(j < nctx, base + b * nctx + (nctx - 1 - j), b * nlat + (nlat - 1 - (j - nctx)))

    def specs(blk):
        return [pl.BlockSpec((tb, kw), lambda b, j: (blk(b, j), Z_BK // kw)),
                pl.BlockSpec((tb, vw), lambda b, j: (blk(b, j), Z_BV // vw)),
                pl.BlockSpec((tb, kw), lambda b, j: (blk(b, j), Z_BQ // kw)),
                pl.BlockSpec((tb, LANES), lambda b, j: (blk(b, j), 0))]

    return pl.pallas_call(
        _gla_kernel,
        out_shape=(jax.ShapeDtypeStruct((n_rows, vw), F32), jax.ShapeDtypeStruct((n_rows, vw), F32)),
        grid=(batch, nctx + nlat),
        in_specs=specs(fwd) + specs(bwd) + [
            pl.BlockSpec((2, LANES, kw), lambda b, j: (0, 0, 0)),
            pl.BlockSpec((2, 1, kw), lambda b, j: (0, 0, 0)),
            pl.BlockSpec((2, tb, tb), lambda b, j: (0, 0, 0)),
            pl.BlockSpec((2, tb, tb), lambda b, j: (0, 0, 0)),
        ],
        out_specs=(pl.BlockSpec((tb, vw), lambda b, j: (fwd(b, j), 0)),
                   pl.BlockSpec((tb, vw), lambda b, j: (bwd(b, j), 0))),
        scratch_shapes=[pltpu.VMEM((heads, B_DV, LANES), F32), pltpu.VMEM((heads, B_DV, LANES), F32),
                        pltpu.VMEM((2 * heads, tb, tb), F32),
                        pltpu.VMEM((2 * heads, B_DV, (tb // B_CHUNK) * LANES), F32)],
        compiler_params=_cparams(("parallel", "arbitrary")),
        name="gla",
    )(z, z, z, r, z, z, z, r, w2pad.astype(BF16), bias, jnp.asarray(tri, BF16), jnp.asarray(tri, F32))


ATTN_LOOKAHEAD = 2


def _attn_kernel(sink_ref, q_ref, kp_ref, kc_ref, kn_ref, kx_ref, vp_ref, vc_ref, vn_ref, vx_ref,
                 bias_ref, o_ref, s_ref):
    tq = q_ref.shape[0]
    n_heads = q_ref.shape[1] // HEAD_DIM
    kv_heads = kc_ref.shape[1] // HEAD_DIM
    group = n_heads // kv_heads
    n_loc = bias_ref.shape[1]

    def kv_cols(h):
        kh = h // group
        return slice(kh * HEAD_DIM, (kh + 1) * HEAD_DIM)

    def scores(h):
        cols = kv_cols(h)
        k_all = jnp.concatenate([kp_ref[:, cols], kc_ref[:, cols], kn_ref[:, cols], kx_ref[:, cols]], axis=0)
        s_ref[h * tq:(h + 1) * tq, :] = _dot_nt(q_ref[:, h * HEAD_DIM:(h + 1) * HEAD_DIM], k_all)

    def softmax_pv(h):
        cols = kv_cols(h)
        rows = slice(h * tq, (h + 1) * tq)
        v_all = jnp.concatenate([vp_ref[:, cols], vc_ref[:, cols], vn_ref[:, cols], vx_ref[:, cols]], axis=0)
        sink = sink_ref[h] * LOG2E
        sg = jnp.concatenate([s_ref[rows, :n_loc] + bias_ref[...], s_ref[rows, n_loc:]], axis=1)
        m = jnp.maximum(jnp.max(sg, axis=-1, keepdims=True), sink)
        e = jnp.exp2(sg - m)
        den = jnp.sum(e, axis=-1, keepdims=True) + jnp.exp2(sink - m)
        o_ref[:, h * HEAD_DIM:(h + 1) * HEAD_DIM] = (_dot(e.astype(BF16), v_all) / den).astype(o_ref.dtype)

    for h in range(min(ATTN_LOOKAHEAD, n_heads)):
        scores(h)
    for h in range(n_heads):
        if h + ATTN_LOOKAHEAD < n_heads:
            scores(h + ATTN_LOOKAHEAD)
        softmax_pv(h)


def _window_bias(tq):
    i = np.arange(tq)[:, None]
    j = np.arange(3 * tq)[None, :]
    band = (j >= i) & (j <= i + 2 * tq)
    cases = [band & (j >= tq), band, band & (j < 2 * tq), np.zeros_like(band)]
    return jnp.asarray(np.where(np.stack(cases), 0.0, NEG_BIG), F32)


def _attn(z, sink, batch, seq, ctx_len, with_ctx):
    n_heads = sink.shape[0]
    qw = n_heads * HEAD_DIM
    kvw = C_KVW
    tq = C_BLOCK
    nb = seq // tq
    cpb = ctx_len // tq if with_ctx else 0
    assert nb >= 2 and ctx_len % tq == 0
    ctx_base = batch * seq // ctx_len

    def own(b, n):
        return jnp.where(n < nb, b * nb + n, batch * nb + b * cpb + (n - nb))

    def prev(b, n):
        return b * nb + jnp.clip(n - 1, 0, nb - 1)

    def cur(b, n):
        return b * nb + jnp.minimum(n, nb - 1)

    def nxt(b, n):
        return b * nb + jnp.minimum(n + 1, nb - 1)

    def zspec(blk, col0):
        return pl.BlockSpec((tq, kvw), lambda b, n: (blk(b, n), col0 // kvw))

    def bias_case(b, n):
        return jnp.where(n >= nb, 3, jnp.where(n == 0, 0, jnp.where(n == nb - 1, 2, 1)))

    return pl.pallas_call(
        _attn_kernel,
        out_shape=jax.ShapeDtypeStruct((batch * (nb + cpb) * tq, qw), BF16),
        grid=(batch, nb + cpb),
        in_specs=[
            pl.BlockSpec(memory_space=pltpu.SMEM),
            pl.BlockSpec((tq, qw), lambda b, n: (own(b, n), Z_CQ // qw)),
            zspec(prev, Z_CK), zspec(cur, Z_CK), zspec(nxt, Z_CK),
            pl.BlockSpec((ctx_len, kvw), lambda b, n: (ctx_base + b, Z_CK // kvw)),
            zspec(prev, Z_CV), zspec(cur, Z_CV), zspec(nxt, Z_CV),
            pl.BlockSpec((ctx_len, kvw), lambda b, n: (ctx_base + b, Z_CV // kvw)),
            pl.BlockSpec((None, tq, 3 * tq), lambda b, n: (bias_case(b, n), 0, 0)),
        ],
        out_specs=pl.BlockSpec((tq, qw), lambda b, n: (own(b, n), 0)),
        scratch_shapes=[pltpu.VMEM((n_heads * tq, 3 * tq + ctx_len), F32)],
        compiler_params=_cparams(("parallel", "parallel")),
        name="window_attn",
    )(sink, z, z, z, z, z, z, z, z, z, _window_bias(tq))


MERGE_CHUNKS = 4


def _merge_kernel(x_ref, m_ref, a_ref, of_ref, ob_ref, og_ref, c_ref, ga_ref, gb_ref, gc_ref,
                  bg_ref, wa_ref, wb_ref, wc_ref, wo_ref, o_ref):
    heads = of_ref.shape[1] // B_DV
    o = of_ref[...] + ob_ref[...]
    bn = jnp.concatenate([_rms_head(o[:, h * B_DV:(h + 1) * B_DV], bg_ref[...]) for h in range(heads)], axis=1)
    b = (bn * _silu(og_ref[...].astype(F32))).astype(BF16)
    a, c = a_ref[...], c_ref[...]
    d = o_ref.shape[1]
    cw = d // MERGE_CHUNKS

    def merged_cols(k):
        cols = slice(k * cw, (k + 1) * cw)
        return (jax.nn.sigmoid(ga_ref[:, cols].astype(F32)) * _dot(a, wa_ref[:, cols])
                + jax.nn.sigmoid(gb_ref[:, cols].astype(F32)) * _dot(b, wb_ref[:, cols])
                + jax.nn.sigmoid(gc_ref[:, cols].astype(F32)) * _dot(c, wc_ref[:, cols])).astype(BF16)

    pending = merged_cols(0)
    mix = None
    for k in range(MERGE_CHUNKS):
        ready = pending
        if k + 1 < MERGE_CHUNKS:
            pending = merged_cols(k + 1)
        part = _dot(ready, wo_ref[k * cw:(k + 1) * cw, :])
        mix = part if mix is None else mix + part
    o_ref[...] = x_ref[...] + m_ref[2] * mix


def _merge(x, n_rows, mod, layer, row_fn, z, a, o_f, o_b, c, b_norm_g, wa, wb, wc, wo, tm):
    d = x.shape[1]
    aw, bw, cw = a.shape[1], o_f.shape[1], c.shape[1]

    def const(rows):
        return pl.BlockSpec((None, rows, d), lambda i: (layer, 0, 0), pipeline_mode=pl.Buffered(1))

    return pl.pallas_call(
        _merge_kernel,
        out_shape=jax.ShapeDtypeStruct((n_rows, d), F32),
        grid=(n_rows // tm,),
        in_specs=[
            pl.BlockSpec((tm, d), lambda i: (i, 0)),
            _mod_spec(layer, 1, row_fn, d),
            pl.BlockSpec((tm, aw), lambda i: (i, 0)),
            pl.BlockSpec((tm, bw), lambda i: (i, 0)),
            pl.BlockSpec((tm, bw), lambda i: (i, 0)),
            pl.BlockSpec((tm, bw), lambda i: (i, Z_BG // bw)),
            pl.BlockSpec((tm, cw), lambda i: (i, 0)),
            pl.BlockSpec((tm, d), lambda i: (i, Z_GATE // d)),
            pl.BlockSpec((tm, d), lambda i: (i, Z_GATE // d + 1)),
            pl.BlockSpec((tm, d), lambda i: (i, Z_GATE // d + 2)),
            pl.BlockSpec((1, B_DV), lambda i: (0, 0)),
            const(aw), const(bw), const(cw), const(d),
        ],
        out_specs=pl.BlockSpec((tm, d), lambda i: (i, 0)),
        compiler_params=_cparams(("parallel",)),
        name="merge",
    )(x, mod, a, o_f, o_b, z, c, z, z, z, b_norm_g.reshape(1, B_DV), wa, wb, wc, wo)


def _rope_tables(seq, pad_rows):
    half = HEAD_DIM // 4
    t = np.arange(seq)
    pos = np.stack([t // GRID_W, t % GRID_W], axis=1).astype(np.float32)
    inv_freq = jnp.asarray(ROPE_BASE, F32) ** (-jnp.arange(half, dtype=F32) / half)
    ang = jnp.asarray(pos)[:, :, None] * inv_freq[None, None, :]
    cos = jnp.cos(ang)
    sin = jnp.sin(ang)
    cos_t = jnp.stack([cos, cos], axis=1).reshape(seq, HEAD_DIM)
    sin_t = jnp.stack([-sin, sin], axis=1).reshape(seq, HEAD_DIM)
    cos_t = jnp.concatenate([cos_t, jnp.ones((pad_rows, HEAD_DIM), F32)], axis=0)
    sin_t = jnp.concatenate([sin_t, jnp.zeros((pad_rows, HEAD_DIM), F32)], axis=0)
    return cos_t, sin_t


def _rotary_lane_order(w):
    lead = w.shape[:-1]
    nh = w.shape[-1] // HEAD_DIM
    w = w.reshape(lead + (nh, 2, 2, HEAD_DIM // 4))
    w = jnp.swapaxes(w, -3, -2)
    return w.reshape(lead + (nh * HEAD_DIM,))


def _reorder_w_in(w, d):
    bkw, bvw, ckvw, aw, cqw = 256, 512, 256, 512, 1024
    o_bk = 0
    o_bv = o_bk + bkw
    o_ck = o_bv + bvw
    o_cv = o_ck + ckvw
    o_au = o_cv + ckvw
    o_av = o_au + aw
    o_bq = o_av + aw
    o_bg = o_bq + bkw
    o_cq = o_bg + bvw
    o_gate = o_cq + cqw
    seg = lambda s, width: w[..., s:s + width]
    return jnp.concatenate([_rotary_lane_order(seg(o_cq, cqw)), _rotary_lane_order(seg(o_ck, ckvw)),
                            seg(o_cv, ckvw), seg(o_bk, bkw), seg(o_bq, bkw), seg(o_bv, bvw),
                            seg(o_au, aw), seg(o_av, aw), seg(o_bg, bvw), seg(o_gate, 3 * d)], axis=-1)


def kernel(x, c, ctx, c_ctx, w_ada, b_ada, norm_g, w_ffn_up, w_ffn_down, w_in, a_v_gain, a_ws, a_bs,
           b_decay_w1, b_decay_w2, b_decay_b, b_norm_g, c_q_gain, c_k_gain, c_sink,
           w_br_a, w_br_b, w_br_c, w_out):
    batch, seq, d = x.shape
    ctx_len = ctx.shape[1]
    depth = w_ada.shape[0]
    assert d == 2048 and batch < MOD_ROWS and seq % 256 == 0 and ctx_len == 256
    n_lat, n_ctx = batch * seq, batch * ctx_len
    n_all = n_lat + n_ctx
    tm_ffn, tm_in, tm_merge = 512, 512, 256
    for t in (tm_ffn, tm_in, tm_merge):
        assert seq % t == 0 and n_ctx % t == 0

    def row_fn(t):
        return lambda i: jnp.minimum(i * t // seq, batch)

    c8 = jnp.zeros((MOD_ROWS, d), F32).at[:batch].set(c).at[batch].set(c_ctx)
    mod = _with_norm_gains(_mod_table(c8, w_ada, b_ada), norm_g)
    cos_t, sin_t = _rope_tables(seq, tm_in)

    w_up = w_ffn_up.astype(BF16)
    w_dn = w_ffn_down.astype(BF16)
    w_in_r = _reorder_w_in(w_in, d).astype(BF16)
    w1 = jnp.zeros((depth, d, LANES), F32).at[:, :, :B_RANK].set(b_decay_w1[:, 0]).at[
        :, :, B_RANK:2 * B_RANK].set(b_decay_w1[:, 1]).astype(BF16)
    kw = b_decay_w2.shape[-1]
    w2pad = jnp.zeros((depth, 2, LANES, kw), F32).at[:, 0, :B_RANK].set(b_decay_w2[:, 0]).at[
        :, 1, B_RANK:2 * B_RANK].set(b_decay_w2[:, 1])
    n_qh = c_sink.shape[1]
    qk_gain = jnp.concatenate([jnp.tile(_rotary_lane_order(c_q_gain) * (HEAD_DIM ** -0.5 * LOG2E), (1, n_qh)),
                               jnp.tile(_rotary_lane_order(c_k_gain), (1, C_KVW // HEAD_DIM))], axis=1)
    wa, wb, wc, wo = (w.astype(BF16) for w in (w_br_a, w_br_b, w_br_c, w_out))

    srcs = [x.reshape(n_lat, d), ctx.reshape(n_ctx, d)]
    for l in range(depth):
        last = l == depth - 1
        n_mix = n_lat if last else n_all
        xs = _ffn(srcs, n_all, mod, l, 0, row_fn(tm_ffn), w_up, w_dn, tm_ffn, TF)
        z, r = _inproj(xs, mod, l, row_fn(tm_in), w_in_r, w1, qk_gain[l:l + 1], cos_t, sin_t,
                       n_lat, seq, tm_in, 2 * QK_COLS)
        a_out = _gmlp(z, n_mix, a_v_gain[l], a_ws[l], a_bs[l], 512)
        o_f, o_b = _gla(z, r, w2pad[l], b_decay_b[l].reshape(2, 1, kw), batch, seq, ctx_len, 256)
        c_out = _attn(z, c_sink[l], batch, seq, ctx_len, with_ctx=not last)
        xs = _merge(xs, n_mix, mod, l, row_fn(tm_merge), z, a_out, o_f, o_b, c_out, b_norm_g[l],
                    wa, wb, wc, wo, tm_merge)
        xs = _ffn([xs], n_mix, mod, l, 1, row_fn(tm_ffn), w_up, w_dn, tm_ffn, TF)
        srcs = [xs]
    return xs.reshape(batch, seq, d)
```

```python
import functools

import jax
import jax.numpy as jnp
import numpy as np
from jax import lax
from jax.experimental import pallas as pl
from jax.experimental.pallas import tpu as pltpu

F32 = jnp.float32
BF16 = jnp.bfloat16

HEAD_DIM = 128
EPS = 1e-6
N_MOD = 9
GRID_W = 64
A_CHUNK = 128
B_DK = 64
B_DV = 128
B_RANK = 16
B_TAU = 16.0
B_CHUNK = 64
C_BLOCK = 128
ROPE_BASE = 10000.0
MOD_ROWS = 8
LANES = 128
NEG_BIG = -1e30
LOG2E = 1.4426950408889634
LN2 = 0.6931471805599453
VMEM_LIMIT = 56 * 1024 * 1024

Z_CQ, Z_CK, Z_CV, Z_BK, Z_BQ, Z_BV, Z_AU, Z_AV, Z_BG, Z_GATE = (
    0, 1024, 1280, 1536, 1792, 2048, 2560, 3072, 3584, 4096)
QK_COLS = Z_CV
C_KVW = Z_CV - Z_CK


def _cparams(sem):
    return pltpu.CompilerParams(dimension_semantics=sem, vmem_limit_bytes=VMEM_LIMIT)


def _dot(a, b):
    return jnp.dot(a, b, preferred_element_type=F32)


def _dot_nt(a, b):
    return lax.dot_general(a, b, (((1,), (1,)), ((), ())), preferred_element_type=F32)


def _dot_tn(a, b):
    return lax.dot_general(a, b, (((0,), (0,)), ((), ())), preferred_element_type=F32)


def _dot_f32(a, b):
    return jnp.dot(a, b, preferred_element_type=F32, precision=lax.Precision.HIGHEST)


def _silu(x):
    return x * jax.nn.sigmoid(x)


def _rms_mod(x, gain, shift, scale):
    ms = jnp.mean(x * x, axis=-1, keepdims=True)
    y = x * lax.rsqrt(ms + EPS) * gain
    return y * (1.0 + scale) + shift


NORM_ROWS = 64


def _rms_mod_to(x_ref, m_ref, h_ref):
    shift = m_ref[0]
    gain = m_ref[3] * (1.0 + m_ref[1])

    def one_pass(c, carry):
        rows = pl.ds(pl.multiple_of(c * NORM_ROWS, NORM_ROWS), NORM_ROWS)
        x = x_ref[rows, :]
        rs = lax.rsqrt(jnp.mean(x * x, axis=-1, keepdims=True) + EPS)
        h_ref[rows, :] = (x * rs * gain + shift).astype(BF16)
        return carry

    lax.fori_loop(0, x_ref.shape[0] // NORM_ROWS, one_pass, 0, unroll=2)


def _rms_head(xh, gain):
    ms = jnp.mean(xh * xh, axis=-1, keepdims=True)
    return xh * lax.rsqrt(ms + EPS) * gain


def _mod_kernel(c_ref, w_ref, b_ref, o_ref):
    act = _silu(c_ref[...])
    o_ref[...] = _dot(act.astype(BF16), w_ref[...].astype(BF16)) + b_ref[...]


def _mod_table(c8, w_ada, b_ada):
    depth, d, nd = w_ada.shape
    tn = 1024
    nj = d // tn
    out = pl.pallas_call(
        _mod_kernel,
        out_shape=jax.ShapeDtypeStruct((depth, N_MOD, MOD_ROWS, d), F32),
        grid=(depth, N_MOD, nj),
        in_specs=[
            pl.BlockSpec((MOD_ROWS, d), lambda l, k, j: (0, 0)),
            pl.BlockSpec((None, d, tn), lambda l, k, j: (l, 0, k * nj + j)),
            pl.BlockSpec((None, 1, tn), lambda l, k, j: (l, 0, k * nj + j)),
        ],
        out_specs=pl.BlockSpec((None, None, MOD_ROWS, tn), lambda l, k, j: (l, k, 0, j)),
        compiler_params=_cparams(("parallel", "parallel", "parallel")),
        name="mod_table",
    )(c8, w_ada, b_ada.reshape(depth, 1, nd))
    return out


MOD_GROUP = 4
TF = 512


def _with_norm_gains(mod, norm_g):
    depth, _, rows, d = mod.shape
    subs = norm_g.shape[1]
    gains = jnp.broadcast_to(norm_g[:, :, None, None, :], (depth, subs, 1, rows, d))
    table = jnp.concatenate([mod.reshape(depth, subs, N_MOD // subs, rows, d), gains], axis=2)
    return table.reshape(depth, subs * MOD_GROUP, rows, 1, d)


def _mod_spec(layer, sub, row_fn, d):
    return pl.BlockSpec((None, MOD_GROUP, None, 1, d), lambda i, *_: (layer, sub, row_fn(i), 0, 0))


def _ffn_kernel(*refs, tiles_per_source):
    n_src = len(tiles_per_source)
    x_refs = refs[:n_src]
    m_ref, wg_ref, wv_ref, wd_ref, o_ref, h_ref, acc_ref = refs[n_src:]
    i, f = pl.program_id(0), pl.program_id(1)

    def with_own_source(fn):
        start = 0
        for x_ref, n_tiles in zip(x_refs, tiles_per_source):
            if n_src == 1:
                fn(x_ref)
            else:
                pl.when(jnp.logical_and(i >= start, i < start + n_tiles))(lambda x_ref=x_ref: fn(x_ref))
            start += n_tiles

    last = pl.num_programs(1) - 1

    def prologue(x_ref):
        _rms_mod_to(x_ref, m_ref, h_ref)

    def chunk():
        h = h_ref[...]
        act = (_silu(_dot(h, wg_ref[...])) * _dot(h, wv_ref[...])).astype(BF16)
        return _dot(act, wd_ref[...])

    @pl.when(f == 0)
    def _():
        with_own_source(prologue)
        acc_ref[...] = chunk()

    if n_src == 1:
        @pl.when(jnp.logical_and(f > 0, f < last))
        def _():
            acc_ref[...] += chunk()

        @pl.when(f == last)
        def _():
            o_ref[...] = x_refs[0][...] + 0.5 * m_ref[2] * (acc_ref[...] + chunk())
    else:
        @pl.when(f > 0)
        def _():
            acc_ref[...] += chunk()

        def epilogue(x_ref):
            o_ref[...] = x_ref[...] + 0.5 * m_ref[2] * acc_ref[...]

        @pl.when(f == last)
        def _():
            with_own_source(epilogue)


def _ffn(xs, n_rows, mod, layer, which, row_fn, w_up, w_down, tm, tf):
    d = xs[0].shape[1]
    nf = w_down.shape[2] // tf
    nt = n_rows // tm
    tiles, starts = [], []
    for x in xs:
        starts.append(sum(tiles))
        tiles.append(min(x.shape[0] // tm, nt - sum(tiles)))

    def src_spec(start, n_tiles):
        return pl.BlockSpec((tm, d), lambda i, f: (jnp.clip(i - start, 0, n_tiles - 1), 0))

    return pl.pallas_call(
        functools.partial(_ffn_kernel, tiles_per_source=tuple(tiles)),
        out_shape=jax.ShapeDtypeStruct((n_rows, d), F32),
        grid=(nt, nf),
        in_specs=[src_spec(s, t) for s, t in zip(starts, tiles)] + [
            _mod_spec(layer, 2 * which, row_fn, d),
            pl.BlockSpec((None, None, d, tf), lambda i, f: (layer, which, 0, f)),
            pl.BlockSpec((None, None, d, tf), lambda i, f: (layer, which, 0, nf + f)),
            pl.BlockSpec((None, None, tf, d), lambda i, f: (layer, which, f, 0)),
        ],
        out_specs=pl.BlockSpec((tm, d), lambda i, f: (i, 0)),
        scratch_shapes=[pltpu.VMEM((tm, d), BF16), pltpu.VMEM((tm, d), F32)],
        compiler_params=_cparams(("parallel", "arbitrary")),
        name="ffn",
    )(*xs, mod, w_up, w_up, w_down)


def _inproj_kernel(x_ref, m_ref, w_ref, w1_ref, qkg_ref, cos_ref, sin_ref, z_ref, r_ref, h_ref, *,
                   full_tiles):
    i, n = pl.program_id(0), pl.program_id(1)

    @pl.when(jnp.logical_and(n != 0, i < full_tiles))
    def _():
        z_ref[...] = _dot(h_ref[...], w_ref[...]).astype(z_ref.dtype)

    @pl.when(jnp.logical_and(n != 0, i >= full_tiles))
    def _():
        z_ref[...] = jnp.zeros_like(z_ref)

    @pl.when(n == 0)
    def _():
        _rms_mod_to(x_ref, m_ref, h_ref)
        h = h_ref[...]
        r_ref[...] = _dot(h, w1_ref[...])
        zt = _dot(h, w_ref[:, :QK_COLS])
        if w_ref.shape[1] > QK_COLS:
            z_ref[:, QK_COLS:] = _dot(h, w_ref[:, QK_COLS:]).astype(z_ref.dtype)
        pair = 2 * HEAD_DIM
        r = lax.broadcasted_iota(jnp.int32, (pair, pair), 0)
        c = lax.broadcasted_iota(jnp.int32, (pair, pair), 1)
        head_sum = jnp.where(jnp.bitwise_and(jnp.bitwise_xor(r, c), HEAD_DIM) == 0, 1.0, 0.0).astype(BF16)
        swap = jnp.where(c == jnp.bitwise_xor(r, HEAD_DIM // 4), 1.0, 0.0).astype(BF16)
        cos = jnp.concatenate([cos_ref[...]] * 2, axis=1)
        sin = jnp.concatenate([sin_ref[...]] * 2, axis=1)
        for p in range(QK_COLS // pair):
            cols = slice(p * pair, (p + 1) * pair)
            y = zt[:, cols]
            ms = _dot((y * y).astype(BF16), head_sum) * (1.0 / HEAD_DIM)
            y = y * lax.rsqrt(ms + EPS) * qkg_ref[:, cols]
            z_ref[:, cols] = (y * cos + _dot(y.astype(BF16), swap) * sin).astype(z_ref.dtype)


def _inproj(x, mod, layer, row_fn, w_in, w1, qk_gain, cos_t, sin_t, n_lat, seq, tm, tn, ctx_state_only):
    n_rows, d = x.shape
    ncols = w_in.shape[2]
    assert tn % QK_COLS == 0 and ncols % tn == 0 and Z_BV + 512 <= tn
    rope_blocks = seq // tm
    nt = n_rows // tm
    full_tiles = n_lat // tm if ctx_state_only else nt

    def rope_blk(i, n):
        return jnp.where(i * tm < n_lat, (i % rope_blocks), rope_blocks)

    def w_blk(i, n):
        return jnp.where(i < full_tiles, n, 0)

    return pl.pallas_call(
        functools.partial(_inproj_kernel, full_tiles=full_tiles),
        out_shape=(jax.ShapeDtypeStruct((n_rows, ncols), BF16),
                   jax.ShapeDtypeStruct((n_rows, LANES), F32)),
        grid=(n_rows // tm, ncols // tn),
        in_specs=[
            pl.BlockSpec((tm, d), lambda i, n: (i, 0)),
            _mod_spec(layer, 1, row_fn, d),
            pl.BlockSpec((None, d, tn), lambda i, n: (layer, 0, w_blk(i, n))),
            pl.BlockSpec((None, d, LANES), lambda i, n: (layer, 0, 0)),
            pl.BlockSpec((1, QK_COLS), lambda i, n: (0, 0)),
            pl.BlockSpec((tm, HEAD_DIM), lambda i, n: (rope_blk(i, n), 0)),
            pl.BlockSpec((tm, HEAD_DIM), lambda i, n: (rope_blk(i, n), 0)),
        ],
        out_specs=(pl.BlockSpec((tm, tn), lambda i, n: (i, n)),
                   pl.BlockSpec((tm, LANES), lambda i, n: (i, 0))),
        scratch_shapes=[pltpu.VMEM((tm, d), BF16)],
        compiler_params=_cparams(("parallel", "arbitrary")),
        name="inproj",
    )(x, mod, w_in, w1, qk_gain, cos_t, sin_t)


def _gmlp_kernel(u_ref, v_ref, gain_ref, ws_ref, bs_ref, o_ref):
    tg, width = u_ref.shape
    groups = width // HEAD_DIM
    u = jax.nn.gelu(u_ref[...].astype(F32))
    v = jax.nn.gelu(v_ref[...].astype(F32))
    mu = jnp.mean(v, axis=-1, keepdims=True)
    vc = v - mu
    var = jnp.mean(vc * vc, axis=-1, keepdims=True)
    vn = (vc * lax.rsqrt(var + EPS) * gain_ref[...]).astype(BF16)
    nchunk = tg // A_CHUNK
    for g in range(groups):
        cols = slice(g * HEAD_DIM, (g + 1) * HEAD_DIM)
        v_g = jnp.concatenate([vn[c * A_CHUNK:(c + 1) * A_CHUNK, cols] for c in range(nchunk)], axis=1)
        mixed = _dot(ws_ref[g], v_g)
        for c in range(nchunk):
            rows = slice(c * A_CHUNK, (c + 1) * A_CHUNK)
            o_ref[rows, cols] = (u[rows, cols] * (mixed[:, c * HEAD_DIM:(c + 1) * HEAD_DIM] + bs_ref[g])
                                 ).astype(o_ref.dtype)


def _gmlp(z, n_rows, a_v_gain, a_ws, a_bs, tg):
    groups = a_ws.shape[0]
    width = groups * HEAD_DIM
    bs_full = jnp.broadcast_to(a_bs[:, :, None], (groups, A_CHUNK, HEAD_DIM)).astype(F32)
    return pl.pallas_call(
        _gmlp_kernel,
        out_shape=jax.ShapeDtypeStruct((n_rows, width), BF16),
        grid=(n_rows // tg,),
        in_specs=[
            pl.BlockSpec((tg, width), lambda i: (i, Z_AU // width)),
            pl.BlockSpec((tg, width), lambda i: (i, Z_AV // width)),
            pl.BlockSpec((1, width), lambda i: (0, 0)),
            pl.BlockSpec((groups, A_CHUNK, A_CHUNK), lambda i: (0, 0, 0)),
            pl.BlockSpec((groups, A_CHUNK, HEAD_DIM), lambda i: (0, 0, 0)),
        ],
        out_specs=pl.BlockSpec((tg, width), lambda i: (i, 0)),
        compiler_params=_cparams(("parallel",)),
        name="gmlp",
    )(z, z, a_v_gain.reshape(1, width), a_ws.astype(BF16), bs_full)


def _gla_prep(k_ref, q_ref, r_ref, w2, bias, tri_ref, reverse):
    tb = k_ref.shape[0]
    heads = k_ref.shape[1] // B_DK
    nchunk = tb // B_CHUNK
    logit = _dot(r_ref[...].astype(BF16), w2) + bias
    soft = jnp.log2(1.0 + jnp.exp2(jnp.abs(logit) * -LOG2E))
    g = jnp.minimum(logit, 0.0) * (1.0 / B_TAU) - soft * (LN2 / B_TAU)
    g_hi = g.astype(BF16)
    g_lo = (g - g_hi.astype(F32)).astype(BF16)
    tri = tri_ref[...]
    bc = _dot(tri, g_hi) + _dot(tri, g_lo)
    last = [c * B_CHUNK + (0 if reverse else B_CHUNK - 1) for c in range(nchunk)]
    tots = [bc[r:r + 1, :] for r in last]
    tot_rows = jnp.concatenate([jnp.broadcast_to(t, (B_CHUNK, t.shape[1])) for t in tots], axis=0)
    kf = k_ref[...].astype(F32)
    q_in = (q_ref[...].astype(F32) * jnp.exp(bc) * (B_DK ** -0.5)).astype(BF16)
    k_in = kf * jnp.exp(-bc)
    k_out = kf * jnp.exp(tot_rows - bc)
    dec = [jnp.exp(t) for t in tots]
    lane = lax.broadcasted_iota(jnp.int32, (tb, LANES), 1)
    zeros = jnp.zeros((B_CHUNK, LANES), BF16)
    q_slabs, k_in_heads, k4_heads = [], [], []
    for h in range(heads):
        slab = slice((h // 2) * LANES, (h // 2 + 1) * LANES)
        mine = (lane >= B_DK) if h % 2 else (lane < B_DK)
        q_slabs.append(q_in[:, slab])
        k_in_heads.append(jnp.where(mine, k_in[:, slab], 0.0).astype(BF16))
        k_out_h = jnp.where(mine, k_out[:, slab], 0.0).astype(BF16)
        k4_heads.append(jnp.concatenate([
            jnp.concatenate([zeros] * c + [k_out_h[c * B_CHUNK:(c + 1) * B_CHUNK]] + [zeros] * (nchunk - 1 - c),
                            axis=0) for c in range(nchunk)], axis=1))
    return q_slabs, k_in_heads, k4_heads, dec


def _gla_kernel(kf_ref, vf_ref, qf_ref, rf_ref, kb_ref, vb_ref, qb_ref, rb_ref, w2_ref, b_ref,
                tri_ref, keep_ref, of_ref, ob_ref, sf_ref, sb_ref, a_ref, u_ref):
    @pl.when(pl.program_id(1) == 0)
    def _():
        sf_ref[...] = jnp.zeros_like(sf_ref)
        sb_ref[...] = jnp.zeros_like(sb_ref)

    heads = sf_ref.shape[0]
    nchunk = kf_ref.shape[0] // B_CHUNK
    v_refs, o_refs, st_refs = (vf_ref, vb_ref), (of_ref, ob_ref), (sf_ref, sb_ref)
    prep = [_gla_prep(kf_ref, qf_ref, rf_ref, w2_ref[0], b_ref[0], tri_ref.at[0], False),
            _gla_prep(kb_ref, qb_ref, rb_ref, w2_ref[1], b_ref[1], tri_ref.at[1], True)]
    jobs = [(d, h) for d in range(2) for h in range(heads)]
    for d, h in jobs:
        q_slabs, k_in_heads, k4_heads, _ = prep[d]
        v_h = v_refs[d][:, h * B_DV:(h + 1) * B_DV]
        a_ref[d * heads + h] = _dot_nt(q_slabs[h], k_in_heads[h])
        u_ref[d * heads + h] = _dot_tn(v_h, k4_heads[h])
    for d, h in jobs:
        vcols = slice(h * B_DV, (h + 1) * B_DV)
        attn = jnp.where(keep_ref[d] > 0.5, a_ref[d * heads + h], 0.0).astype(BF16)
        o_refs[d][:, vcols] = _dot(attn, v_refs[d][:, vcols])
    for d, h in jobs:
        q_s, dec = prep[d][0][h], prep[d][3]
        slab = slice((h // 2) * LANES, (h // 2 + 1) * LANES)
        vcols = slice(h * B_DV, (h + 1) * B_DV)
        st = st_refs[d][h]
        for c in (range(nchunk - 1, -1, -1) if d else range(nchunk)):
            rows = slice(c * B_CHUNK, (c + 1) * B_CHUNK)
            o_refs[d][rows, vcols] += _dot_nt(q_s[rows], st.astype(BF16))
            st = st * dec[c][:, slab] + u_ref[d * heads + h, :, c * LANES:(c + 1) * LANES]
        st_refs[d][h] = st


def _chunk_triangles(tb):
    t = np.arange(tb)[:, None]
    s = np.arange(tb)[None, :]
    same = (t // B_CHUNK) == (s // B_CHUNK)
    return np.stack([same & (s <= t), same & (s >= t)]).astype(np.float32)


def _gla(z, r, w2pad, bias, batch, seq, ctx_len, tb):
    n_rows = z.shape[0]
    heads = bias.shape[-1] // B_DK
    assert heads % 2 == 0 and 2 * B_DK == LANES
    kw, vw = heads * B_DK, heads * B_DV
    nlat, nctx = seq // tb, ctx_len // tb
    base = batch * nlat
    tri = _chunk_triangles(tb)

    def fwd(b, j):
        return jnp.where(j < nctx, base + b * nctx + j, b * nlat + (j - nctx))

    def bwd(b, j):
        return jnp.where(j < nctx, base + b * nctx + (nctx - 1 - j), b * nlat + (nlat - 1 - (j - nctx)))

    def specs(blk):
        return [pl.BlockSpec((tb, kw), lambda b, j: (blk(b, j), Z_BK // kw)),
                pl.BlockSpec((tb, vw), lambda b, j: (blk(b, j), Z_BV // vw)),
                pl.BlockSpec((tb, kw), lambda b, j: (blk(b, j), Z_BQ // kw)),
                pl.BlockSpec((tb, LANES), lambda b, j: (blk(b, j), 0))]

    return pl.pallas_call(
        _gla_kernel,
        out_shape=(jax.ShapeDtypeStruct((n_rows, vw), F32), jax.ShapeDtypeStruct((n_rows, vw), F32)),
        grid=(batch, nctx + nlat),
        in_specs=specs(fwd) + specs(bwd) + [
            pl.BlockSpec((2, LANES, kw), lambda b, j: (0, 0, 0)),
            pl.BlockSpec((2, 1, kw), lambda b, j: (0, 0, 0)),
            pl.BlockSpec((2, tb, tb), lambda b, j: (0, 0, 0)),
            pl.BlockSpec((2, tb, tb), lambda b, j: (0, 0, 0)),
        ],
        out_specs=(pl.BlockSpec((tb, vw), lambda b, j: (fwd(b, j), 0)),
                   pl.BlockSpec((tb, vw), lambda b, j: (bwd(b, j), 0))),
        scratch_shapes=[pltpu.VMEM((heads, B_DV, LANES), F32), pltpu.VMEM((heads, B_DV, LANES), F32),
                        pltpu.VMEM((2 * heads, tb, tb), F32),
                        pltpu.VMEM((2 * heads, B_DV, (tb // B_CHUNK) * LANES), F32)],
        compiler_params=_cparams(("parallel", "arbitrary")),
        name="gla",
    )(z, z, z, r, z, z, z, r, w2pad.astype(BF16), bias, jnp.asarray(tri, BF16), jnp.asarray(tri, F32))


def _attn_kernel(sink_ref, q_ref, kp_ref, kc_ref, kn_ref, kx_ref, vp_ref, vc_ref, vn_ref, vx_ref,
                 bias_ref, o_ref, s_ref):
    tq = q_ref.shape[0]
    n_heads = q_ref.shape[1] // HEAD_DIM
    kv_heads = kc_ref.shape[1] // HEAD_DIM
    group = n_heads // kv_heads
    n_loc = bias_ref.shape[1]

    def kv_cols(h):
        kh = h // group
        return slice(kh * HEAD_DIM, (kh + 1) * HEAD_DIM)

    def scores(kh):
        cols = kv_cols(kh * group)
        k_all = jnp.concatenate([kp_ref[:, cols], kc_ref[:, cols], kn_ref[:, cols], kx_ref[:, cols]], axis=0)
        q = jnp.concatenate([q_ref[:, h * HEAD_DIM:(h + 1) * HEAD_DIM]
                             for h in range(kh * group, (kh + 1) * group)], axis=0)
        s_ref[kh * group * tq:(kh + 1) * group * tq, :] = _dot_nt(q, k_all)

    def softmax_pv(h):
        cols = kv_cols(h)
        rows = slice(h * tq, (h + 1) * tq)
        v_all = jnp.concatenate([vp_ref[:, cols], vc_ref[:, cols], vn_ref[:, cols], vx_ref[:, cols]], axis=0)
        sink = sink_ref[h] * LOG2E
        sg = jnp.concatenate([s_ref[rows, :n_loc] + bias_ref[...], s_ref[rows, n_loc:]], axis=1)
        m = jnp.maximum(jnp.max(sg, axis=-1, keepdims=True), sink)
        e = jnp.exp2(sg - m)
        den = jnp.sum(e, axis=-1, keepdims=True) + jnp.exp2(sink - m)
        o_ref[:, h * HEAD_DIM:(h + 1) * HEAD_DIM] = (_dot(e.astype(BF16), v_all) / den).astype(o_ref.dtype)

    for kh in range(kv_heads):
        scores(kh)
    for h in range(n_heads):
        softmax_pv(h)


def _window_bias(tq):
    i = np.arange(tq)[:, None]
    j = np.arange(3 * tq)[None, :]
    band = (j >= i) & (j <= i + 2 * tq)
    cases = [band & (j >= tq), band, band & (j < 2 * tq), np.zeros_like(band)]
    return jnp.asarray(np.where(np.stack(cases), 0.0, NEG_BIG), F32)


def _attn(z, sink, batch, seq, ctx_len, with_ctx):
    n_heads = sink.shape[0]
    qw = n_heads * HEAD_DIM
    kvw = C_KVW
    tq = C_BLOCK
    nb = seq // tq
    cpb = ctx_len // tq if with_ctx else 0
    assert nb >= 2 and ctx_len % tq == 0
    ctx_base = batch * seq // ctx_len

    def own(b, n):
        return jnp.where(n < nb, b * nb + n, batch * nb + b * cpb + (n - nb))

    def prev(b, n):
        return b * nb + jnp.clip(n - 1, 0, nb - 1)

    def cur(b, n):
        return b * nb + jnp.minimum(n, nb - 1)

    def nxt(b, n):
        return b * nb + jnp.minimum(n + 1, nb - 1)

    def zspec(blk, col0):
        return pl.BlockSpec((tq, kvw), lambda b, n: (blk(b, n), col0 // kvw))

    def bias_case(b, n):
        return jnp.where(n >= nb, 3, jnp.where(n == 0, 0, jnp.where(n == nb - 1, 2, 1)))

    return pl.pallas_call(
        _attn_kernel,
        out_shape=jax.ShapeDtypeStruct((batch * (nb + cpb) * tq, qw), BF16),
        grid=(batch, nb + cpb),
        in_specs=[
            pl.BlockSpec(memory_space=pltpu.SMEM),
            pl.BlockSpec((tq, qw), lambda b, n: (own(b, n), Z_CQ // qw)),
            zspec(prev, Z_CK), zspec(cur, Z_CK), zspec(nxt, Z_CK),
            pl.BlockSpec((ctx_len, kvw), lambda b, n: (ctx_base + b, Z_CK // kvw)),
            zspec(prev, Z_CV), zspec(cur, Z_CV), zspec(nxt, Z_CV),
            pl.BlockSpec((ctx_len, kvw), lambda b, n: (ctx_base + b, Z_CV // kvw)),
            pl.BlockSpec((None, tq, 3 * tq), lambda b, n: (bias_case(b, n), 0, 0)),
        ],
        out_specs=pl.BlockSpec((tq, qw), lambda b, n: (own(b, n), 0)),
        scratch_shapes=[pltpu.VMEM((n_heads * tq, 3 * tq + ctx_len), F32)],
        compiler_params=_cparams(("parallel", "parallel")),
        name="window_attn",
    )(sink, z, z, z, z, z, z, z, z, z, _window_bias(tq))


MERGE_CHUNKS = 4


def _merge_kernel(x_ref, m_ref, a_ref, of_ref, ob_ref, og_ref, c_ref, ga_ref, gb_ref, gc_ref,
                  bg_ref, wa_ref, wb_ref, wc_ref, wo_ref, o_ref):
    heads = of_ref.shape[1] // B_DV
    o = of_ref[...] + ob_ref[...]
    bn = jnp.concatenate([_rms_head(o[:, h * B_DV:(h + 1) * B_DV], bg_ref[...]) for h in range(heads)], axis=1)
    b = (bn * _silu(og_ref[...].astype(F32))).astype(BF16)
    a, c = a_ref[...], c_ref[...]
    d = o_ref.shape[1]
    cw = d // MERGE_CHUNKS

    def merged_cols(k):
        cols = slice(k * cw, (k + 1) * cw)
        return (jax.nn.sigmoid(ga_ref[:, cols].astype(F32)) * _dot(a, wa_ref[:, cols])
                + jax.nn.sigmoid(gb_ref[:, cols].astype(F32)) * _dot(b, wb_ref[:, cols])
                + jax.nn.sigmoid(gc_ref[:, cols].astype(F32)) * _dot(c, wc_ref[:, cols])).astype(BF16)

    pending = merged_cols(0)
    mix = None
    for k in range(MERGE_CHUNKS):
        ready = pending
        if k + 1 < MERGE_CHUNKS:
            pending = merged_cols(k + 1)
        part = _dot(ready, wo_ref[k * cw:(k + 1) * cw, :])
        mix = part if mix is None else mix + part
    o_ref[...] = x_ref[...] + m_ref[2] * mix


def _merge(x, n_rows, mod, layer, row_fn, z, a, o_f, o_b, c, b_norm_g, wa, wb, wc, wo, tm):
    d = x.shape[1]
    aw, bw, cw = a.shape[1], o_f.shape[1], c.shape[1]

    def const(rows):
        return pl.BlockSpec((None, rows, d), lambda i: (layer, 0, 0), pipeline_mode=pl.Buffered(1))

    return pl.pallas_call(
        _merge_kernel,
        out_shape=jax.ShapeDtypeStruct((n_rows, d), F32),
        grid=(n_rows // tm,),
        in_specs=[
            pl.BlockSpec((tm, d), lambda i: (i, 0)),
            _mod_spec(layer, 1, row_fn, d),
            pl.BlockSpec((tm, aw), lambda i: (i, 0)),
            pl.BlockSpec((tm, bw), lambda i: (i, 0)),
            pl.BlockSpec((tm, bw), lambda i: (i, 0)),
            pl.BlockSpec((tm, bw), lambda i: (i, Z_BG // bw)),
            pl.BlockSpec((tm, cw), lambda i: (i, 0)),
            pl.BlockSpec((tm, d), lambda i: (i, Z_GATE // d)),
            pl.BlockSpec((tm, d), lambda i: (i, Z_GATE // d + 1)),
            pl.BlockSpec((tm, d), lambda i: (i, Z_GATE // d + 2)),
            pl.BlockSpec((1, B_DV), lambda i: (0, 0)),
            const(aw), const(bw), const(cw), const(d),
        ],
        out_specs=pl.BlockSpec((tm, d), lambda i: (i, 0)),
        compiler_params=_cparams(("parallel",)),
        name="merge",
    )(x, mod, a, o_f, o_b, z, c, z, z, z, b_norm_g.reshape(1, B_DV), wa, wb, wc, wo)


def _rope_tables(seq, pad_rows):
    half = HEAD_DIM // 4
    t = np.arange(seq)
    pos = np.stack([t // GRID_W, t % GRID_W], axis=1).astype(np.float32)
    inv_freq = jnp.asarray(ROPE_BASE, F32) ** (-jnp.arange(half, dtype=F32) / half)
    ang = jnp.asarray(pos)[:, :, None] * inv_freq[None, None, :]
    cos = jnp.cos(ang)
    sin = jnp.sin(ang)
    cos_t = jnp.concatenate([cos, cos], axis=-1).reshape(seq, HEAD_DIM)
    sin_t = jnp.concatenate([-sin, sin], axis=-1).reshape(seq, HEAD_DIM)
    cos_t = jnp.concatenate([cos_t, jnp.ones((pad_rows, HEAD_DIM), F32)], axis=0)
    sin_t = jnp.concatenate([sin_t, jnp.zeros((pad_rows, HEAD_DIM), F32)], axis=0)
    return cos_t, sin_t


def _reorder_w_in(w, d):
    bkw, bvw, ckvw, aw, cqw = 256, 512, 256, 512, 1024
    o_bk = 0
    o_bv = o_bk + bkw
    o_ck = o_bv + bvw
    o_cv = o_ck + ckvw
    o_au = o_cv + ckvw
    o_av = o_au + aw
    o_bq = o_av + aw
    o_bg = o_bq + bkw
    o_cq = o_bg + bvw
    o_gate = o_cq + cqw
    seg = lambda s, width: w[..., s:s + width]
    return jnp.concatenate([seg(o_cq, cqw), seg(o_ck, ckvw), seg(o_cv, ckvw), seg(o_bk, bkw), seg(o_bq, bkw), seg(o_bv, bvw),
                            seg(o_au, aw), seg(o_av, aw), seg(o_bg, bvw), seg(o_gate, 3 * d)], axis=-1)


def kernel(x, c, ctx, c_ctx, w_ada, b_ada, norm_g, w_ffn_up, w_ffn_down, w_in, a_v_gain, a_ws, a_bs,
           b_decay_w1, b_decay_w2, b_decay_b, b_norm_g, c_q_gain, c_k_gain, c_sink,
           w_br_a, w_br_b, w_br_c, w_out):
    batch, seq, d = x.shape
    ctx_len = ctx.shape[1]
    depth = w_ada.shape[0]
    assert d == 2048 and batch < MOD_ROWS and seq % 256 == 0 and ctx_len == 256
    n_lat, n_ctx = batch * seq, batch * ctx_len
    n_all = n_lat + n_ctx
    tm_ffn, tm_in, tm_merge = 512, 512, 256
    for t in (tm_ffn, tm_in, tm_merge):
        assert seq % t == 0 and n_ctx % t == 0

    def row_fn(t):
        return lambda i: jnp.minimum(i * t // seq, batch)

    c8 = jnp.zeros((MOD_ROWS, d), F32).at[:batch].set(c).at[batch].set(c_ctx)
    mod = _with_norm_gains(_mod_table(c8, w_ada, b_ada), norm_g)
    cos_t, sin_t = _rope_tables(seq, tm_in)

    w_up = w_ffn_up.astype(BF16)
    w_dn = w_ffn_down.astype(BF16)
    w_in_r = _reorder_w_in(w_in, d).astype(BF16)
    w1 = jnp.zeros((depth, d, LANES), F32).at[:, :, :B_RANK].set(b_decay_w1[:, 0]).at[
        :, :, B_RANK:2 * B_RANK].set(b_decay_w1[:, 1]).astype(BF16)
    kw = b_decay_w2.shape[-1]
    w2pad = jnp.zeros((depth, 2, LANES, kw), F32).at[:, 0, :B_RANK].set(b_decay_w2[:, 0]).at[
        :, 1, B_RANK:2 * B_RANK].set(b_decay_w2[:, 1])
    n_qh = c_sink.shape[1]
    qk_gain = jnp.concatenate([jnp.tile(c_q_gain * (HEAD_DIM ** -0.5 * LOG2E), (1, n_qh)),
                               jnp.tile(c_k_gain, (1, C_KVW // HEAD_DIM))], axis=1)
    wa, wb, wc, wo = (w.astype(BF16) for w in (w_br_a, w_br_b, w_br_c, w_out))

    srcs = [x.reshape(n_lat, d), ctx.reshape(n_ctx, d)]
    for l in range(depth):
        last = l == depth - 1
        n_mix = n_lat if last else n_all
        xs = _ffn(srcs, n_all, mod, l, 0, row_fn(tm_ffn), w_up, w_dn, tm_ffn, TF)
        z, r = _inproj(xs, mod, l, row_fn(tm_in), w_in_r, w1, qk_gain[l:l + 1], cos_t, sin_t,
                       n_lat, seq, tm_in, 2 * QK_COLS, ctx_state_only=last)
        a_out = _gmlp(z, n_mix, a_v_gain[l], a_ws[l], a_bs[l], 512)
        o_f, o_b = _gla(z, r, w2pad[l], b_decay_b[l].reshape(2, 1, kw), batch, seq, ctx_len, 256)
        c_out = _attn(z, c_sink[l], batch, seq, ctx_len, with_ctx=not last)
        xs = _merge(xs, n_mix, mod, l, row_fn(tm_merge), z, a_out, o_f, o_b, c_out, b_norm_g[l],
                    wa, wb, wc, wo, tm_merge)
        xs = _ffn([xs], n_mix, mod, l, 1, row_fn(tm_ffn), w_up, w_dn, tm_ffn, TF)
        srcs = [xs]
    return xs.reshape(batch, seq, d)
```

```python
import functools

import jax
import jax.numpy as jnp
import numpy as np
from jax import lax
from jax.experimental import pallas as pl
from jax.experimental.pallas import tpu as pltpu

F32 = jnp.float32
BF16 = jnp.bfloat16

HEAD_DIM = 128
EPS = 1e-6
N_MOD = 9
GRID_W = 64
A_CHUNK = 128
B_DK = 64
B_DV = 128
B_RANK = 16
B_TAU = 16.0
B_CHUNK = 64
C_BLOCK = 128
ROPE_BASE = 10000.0
MOD_ROWS = 8
LANES = 128
NEG_BIG = -1e30
LOG2E = 1.4426950408889634
LN2 = 0.6931471805599453
VMEM_LIMIT = 56 * 1024 * 1024

Z_CQ, Z_CK, Z_CV, Z_BK, Z_BQ, Z_BV, Z_AU, Z_AV, Z_BG, Z_GATE = (
    0, 1024, 1280, 1536, 1792, 2048, 2560, 3072, 3584, 4096)
QK_COLS = Z_CV
C_KVW = Z_CV - Z_CK


def _cparams(sem):
    return pltpu.CompilerParams(dimension_semantics=sem, vmem_limit_bytes=VMEM_LIMIT)


def _dot(a, b):
    return jnp.dot(a, b, preferred_element_type=F32)


def _dot_nt(a, b):
    return lax.dot_general(a, b, (((1,), (1,)), ((), ())), preferred_element_type=F32)


def _dot_tn(a, b):
    return lax.dot_general(a, b, (((0,), (0,)), ((), ())), preferred_element_type=F32)


def _dot_f32(a, b):
    return jnp.dot(a, b, preferred_element_type=F32, precision=lax.Precision.HIGHEST)


def _silu(x):
    return x * jax.nn.sigmoid(x)


def _rms_mod(x, gain, shift, scale):
    ms = jnp.mean(x * x, axis=-1, keepdims=True)
    y = x * lax.rsqrt(ms + EPS) * gain
    return y * (1.0 + scale) + shift


NORM_ROWS = 64


def _rms_mod_to(x_ref, m_ref, h_ref):
    shift = m_ref[0]
    gain = m_ref[3] * (1.0 + m_ref[1])

    def one_pass(c, carry):
        rows = pl.ds(pl.multiple_of(c * NORM_ROWS, NORM_ROWS), NORM_ROWS)
        x = x_ref[rows, :]
        rs = lax.rsqrt(jnp.mean(x * x, axis=-1, keepdims=True) + EPS)
        h_ref[rows, :] = (x * rs * gain + shift).astype(BF16)
        return carry

    lax.fori_loop(0, x_ref.shape[0] // NORM_ROWS, one_pass, 0, unroll=2)


def _rms_head(xh, gain):
    ms = jnp.mean(xh * xh, axis=-1, keepdims=True)
    return xh * lax.rsqrt(ms + EPS) * gain


def _mod_kernel(c_ref, w_ref, b_ref, o_ref):
    act = _silu(c_ref[...])
    o_ref[...] = _dot(act.astype(BF16), w_ref[...].astype(BF16)) + b_ref[...]


def _mod_table(c8, w_ada, b_ada):
    depth, d, nd = w_ada.shape
    tn = 1024
    nj = d // tn
    out = pl.pallas_call(
        _mod_kernel,
        out_shape=jax.ShapeDtypeStruct((depth, N_MOD, MOD_ROWS, d), F32),
        grid=(depth, N_MOD, nj),
        in_specs=[
            pl.BlockSpec((MOD_ROWS, d), lambda l, k, j: (0, 0)),
            pl.BlockSpec((None, d, tn), lambda l, k, j: (l, 0, k * nj + j)),
            pl.BlockSpec((None, 1, tn), lambda l, k, j: (l, 0, k * nj + j)),
        ],
        out_specs=pl.BlockSpec((None, None, MOD_ROWS, tn), lambda l, k, j: (l, k, 0, j)),
        compiler_params=_cparams(("parallel", "parallel", "parallel")),
        name="mod_table",
    )(c8, w_ada, b_ada.reshape(depth, 1, nd))
    return out


MOD_GROUP = 4
TF = 512


def _with_norm_gains(mod, norm_g):
    depth, _, rows, d = mod.shape
    subs = norm_g.shape[1]
    gains = jnp.broadcast_to(norm_g[:, :, None, None, :], (depth, subs, 1, rows, d))
    table = jnp.concatenate([mod.reshape(depth, subs, N_MOD // subs, rows, d), gains], axis=2)
    return table.reshape(depth, subs * MOD_GROUP, rows, 1, d)


def _mod_spec(layer, sub, row_fn, d):
    return pl.BlockSpec((None, MOD_GROUP, None, 1, d), lambda i, *_: (layer, sub, row_fn(i), 0, 0))


def _ffn_kernel(*refs, tiles_per_source):
    n_src = len(tiles_per_source)
    x_refs = refs[:n_src]
    m_ref, wg_ref, wv_ref, wd_ref, o_ref, h_ref = refs[n_src:]
    acc_ref = o_ref
    i, f = pl.program_id(0), pl.program_id(1)

    def with_own_source(fn):
        start = 0
        for x_ref, n_tiles in zip(x_refs, tiles_per_source):
            if n_src == 1:
                fn(x_ref)
            else:
                pl.when(jnp.logical_and(i >= start, i < start + n_tiles))(lambda x_ref=x_ref: fn(x_ref))
            start += n_tiles

    last = pl.num_programs(1) - 1

    def prologue(x_ref):
        _rms_mod_to(x_ref, m_ref, h_ref)

    def chunk():
        h = h_ref[...]
        act = (_silu(_dot(h, wg_ref[...])) * _dot(h, wv_ref[...])).astype(BF16)
        return _dot(act, wd_ref[...])

    @pl.when(f == 0)
    def _():
        with_own_source(prologue)
        acc_ref[...] = chunk()

    if n_src == 1:
        @pl.when(jnp.logical_and(f > 0, f < last))
        def _():
            acc_ref[...] += chunk()

        @pl.when(f == last)
        def _():
            o_ref[...] = x_refs[0][...] + 0.5 * m_ref[2] * (acc_ref[...] + chunk())
    else:
        @pl.when(f > 0)
        def _():
            acc_ref[...] += chunk()

        def epilogue(x_ref):
            o_ref[...] = x_ref[...] + 0.5 * m_ref[2] * acc_ref[...]

        @pl.when(f == last)
        def _():
            with_own_source(epilogue)


def _ffn(xs, n_rows, mod, layer, which, row_fn, w_up, w_down, tm, tf):
    d = xs[0].shape[1]
    nf = w_down.shape[2] // tf
    nt = n_rows // tm
    tiles, starts = [], []
    for x in xs:
        starts.append(sum(tiles))
        tiles.append(min(x.shape[0] // tm, nt - sum(tiles)))

    def src_spec(start, n_tiles):
        return pl.BlockSpec((tm, d), lambda i, f: (jnp.clip(i - start, 0, n_tiles - 1), 0))

    return pl.pallas_call(
        functools.partial(_ffn_kernel, tiles_per_source=tuple(tiles)),
        out_shape=jax.ShapeDtypeStruct((n_rows, d), F32),
        grid=(nt, nf),
        in_specs=[src_spec(s, t) for s, t in zip(starts, tiles)] + [
            _mod_spec(layer, 2 * which, row_fn, d),
            pl.BlockSpec((None, None, d, tf), lambda i, f: (layer, which, 0, f)),
            pl.BlockSpec((None, None, d, tf), lambda i, f: (layer, which, 0, nf + f)),
            pl.BlockSpec((None, None, tf, d), lambda i, f: (layer, which, f, 0)),
        ],
        out_specs=pl.BlockSpec((tm, d), lambda i, f: (i, 0)),
        scratch_shapes=[pltpu.VMEM((tm, d), BF16)],
        compiler_params=_cparams(("parallel", "arbitrary")),
        name="ffn",
    )(*xs, mod, w_up, w_up, w_down)


def _inproj_kernel(x_ref, m_ref, w_ref, w1_ref, qkg_ref, cos_ref, sin_ref, z_ref, r_ref, h_ref, *,
                   full_tiles):
    i, n = pl.program_id(0), pl.program_id(1)

    @pl.when(jnp.logical_and(n != 0, i < full_tiles))
    def _():
        z_ref[...] = _dot(h_ref[...], w_ref[...]).astype(z_ref.dtype)

    @pl.when(jnp.logical_and(n != 0, i >= full_tiles))
    def _():
        z_ref[...] = jnp.zeros_like(z_ref)

    @pl.when(n == 0)
    def _():
        _rms_mod_to(x_ref, m_ref, h_ref)
        h = h_ref[...]
        r_ref[...] = _dot(h, w1_ref[...])
        zt = _dot(h, w_ref[:, :QK_COLS])
        if w_ref.shape[1] > QK_COLS:
            z_ref[:, QK_COLS:] = _dot(h, w_ref[:, QK_COLS:]).astype(z_ref.dtype)
        pair = 2 * HEAD_DIM
        r = lax.broadcasted_iota(jnp.int32, (pair, pair), 0)
        c = lax.broadcasted_iota(jnp.int32, (pair, pair), 1)
        head_sum = jnp.where(jnp.bitwise_and(jnp.bitwise_xor(r, c), HEAD_DIM) == 0, 1.0, 0.0).astype(BF16)
        swap = jnp.where(c == jnp.bitwise_xor(r, HEAD_DIM // 4), 1.0, 0.0).astype(BF16)
        cos = jnp.concatenate([cos_ref[...]] * 2, axis=1)
        sin = jnp.concatenate([sin_ref[...]] * 2, axis=1)
        for p in range(QK_COLS // pair):
            cols = slice(p * pair, (p + 1) * pair)
            y = zt[:, cols]
            ms = _dot((y * y).astype(BF16), head_sum) * (1.0 / HEAD_DIM)
            y = y * lax.rsqrt(ms + EPS) * qkg_ref[:, cols]
            z_ref[:, cols] = (y * cos + _dot(y.astype(BF16), swap) * sin).astype(z_ref.dtype)


def _inproj(x, mod, layer, row_fn, w_in, w1, qk_gain, cos_t, sin_t, n_lat, seq, tm, tn, ctx_state_only):
    n_rows, d = x.shape
    ncols = w_in.shape[2]
    assert tn % QK_COLS == 0 and ncols % tn == 0 and Z_BV + 512 <= tn
    rope_blocks = seq // tm
    nt = n_rows // tm
    full_tiles = n_lat // tm if ctx_state_only else nt

    def rope_blk(i, n):
        return jnp.where(i * tm < n_lat, (i % rope_blocks), rope_blocks)

    def w_blk(i, n):
        return jnp.where(i < full_tiles, n, 0)

    return pl.pallas_call(
        functools.partial(_inproj_kernel, full_tiles=full_tiles),
        out_shape=(jax.ShapeDtypeStruct((n_rows, ncols), BF16),
                   jax.ShapeDtypeStruct((n_rows, LANES), F32)),
        grid=(n_rows // tm, ncols // tn),
        in_specs=[
            pl.BlockSpec((tm, d), lambda i, n: (i, 0)),
            _mod_spec(layer, 1, row_fn, d),
            pl.BlockSpec((None, d, tn), lambda i, n: (layer, 0, w_blk(i, n))),
            pl.BlockSpec((None, d, LANES), lambda i, n: (layer, 0, 0)),
            pl.BlockSpec((1, QK_COLS), lambda i, n: (0, 0)),
            pl.BlockSpec((tm, HEAD_DIM), lambda i, n: (rope_blk(i, n), 0)),
            pl.BlockSpec((tm, HEAD_DIM), lambda i, n: (rope_blk(i, n), 0)),
        ],
        out_specs=(pl.BlockSpec((tm, tn), lambda i, n: (i, n)),
                   pl.BlockSpec((tm, LANES), lambda i, n: (i, 0))),
        scratch_shapes=[pltpu.VMEM((tm, d), BF16)],
        compiler_params=_cparams(("parallel", "arbitrary")),
        name="inproj",
    )(x, mod, w_in, w1, qk_gain, cos_t, sin_t)


def _gmlp_kernel(u_ref, v_ref, gain_ref, ws_ref, bs_ref, o_ref):
    tg, width = u_ref.shape
    groups = width // HEAD_DIM
    u = jax.nn.gelu(u_ref[...].astype(F32))
    v = jax.nn.gelu(v_ref[...].astype(F32))
    mu = jnp.mean(v, axis=-1, keepdims=True)
    vc = v - mu
    var = jnp.mean(vc * vc, axis=-1, keepdims=True)
    vn = (vc * lax.rsqrt(var + EPS) * gain_ref[...]).astype(BF16)
    nchunk = tg // A_CHUNK
    for g in range(groups):
        cols = slice(g * HEAD_DIM, (g + 1) * HEAD_DIM)
        v_g = jnp.concatenate([vn[c * A_CHUNK:(c + 1) * A_CHUNK, cols] for c in range(nchunk)], axis=1)
        mixed = _dot(ws_ref[g], v_g)
        for c in range(nchunk):
            rows = slice(c * A_CHUNK, (c + 1) * A_CHUNK)
            o_ref[rows, cols] = (u[rows, cols] * (mixed[:, c * HEAD_DIM:(c + 1) * HEAD_DIM] + bs_ref[g])
                                 ).astype(o_ref.dtype)


def _gmlp(z, n_rows, a_v_gain, a_ws, a_bs, tg):
    groups = a_ws.shape[0]
    width = groups * HEAD_DIM
    bs_full = jnp.broadcast_to(a_bs[:, :, None], (groups, A_CHUNK, HEAD_DIM)).astype(F32)
    return pl.pallas_call(
        _gmlp_kernel,
        out_shape=jax.ShapeDtypeStruct((n_rows, width), BF16),
        grid=(n_rows // tg,),
        in_specs=[
            pl.BlockSpec((tg, width), lambda i: (i, Z_AU // width)),
            pl.BlockSpec((tg, width), lambda i: (i, Z_AV // width)),
            pl.BlockSpec((1, width), lambda i: (0, 0)),
            pl.BlockSpec((groups, A_CHUNK, A_CHUNK), lambda i: (0, 0, 0)),
            pl.BlockSpec((groups, A_CHUNK, HEAD_DIM), lambda i: (0, 0, 0)),
        ],
        out_specs=pl.BlockSpec((tg, width), lambda i: (i, 0)),
        compiler_params=_cparams(("parallel",)),
        name="gmlp",
    )(z, z, a_v_gain.reshape(1, width), a_ws.astype(BF16), bs_full)


def _gla_prep(k_ref, q_ref, r_ref, w2, bias, tri_ref, reverse):
    tb = k_ref.shape[0]
    heads = k_ref.shape[1] // B_DK
    nchunk = tb // B_CHUNK
    logit = _dot(r_ref[...].astype(BF16), w2) + bias
    soft = jnp.log2(1.0 + jnp.exp2(jnp.abs(logit) * -LOG2E))
    g = jnp.minimum(logit, 0.0) * (1.0 / B_TAU) - soft * (LN2 / B_TAU)
    g_hi = g.astype(BF16)
    g_lo = (g - g_hi.astype(F32)).astype(BF16)
    tri = tri_ref[...]
    bc = _dot(tri, g_hi) + _dot(tri, g_lo)
    last = [c * B_CHUNK + (0 if reverse else B_CHUNK - 1) for c in range(nchunk)]
    tots = [bc[r:r + 1, :] for r in last]
    tot_rows = jnp.concatenate([jnp.broadcast_to(t, (B_CHUNK, t.shape[1])) for t in tots], axis=0)
    kf = k_ref[...].astype(F32)
    q_in = (q_ref[...].astype(F32) * jnp.exp(bc) * (B_DK ** -0.5)).astype(BF16)
    k_in = kf * jnp.exp(-bc)
    k_out = kf * jnp.exp(tot_rows - bc)
    dec = [jnp.exp(t) for t in tots]
    lane = lax.broadcasted_iota(jnp.int32, (tb, LANES), 1)
    zeros = jnp.zeros((B_CHUNK, LANES), BF16)
    q_slabs, k_in_heads, k4_heads = [], [], []
    for h in range(heads):
        slab = slice((h // 2) * LANES, (h // 2 + 1) * LANES)
        mine = (lane >= B_DK) if h % 2 else (lane < B_DK)
        q_slabs.append(q_in[:, slab])
        k_in_heads.append(jnp.where(mine, k_in[:, slab], 0.0).astype(BF16))
        k_out_h = jnp.where(mine, k_out[:, slab], 0.0).astype(BF16)
        k4_heads.append(jnp.concatenate([
            jnp.concatenate([zeros] * c + [k_out_h[c * B_CHUNK:(c + 1) * B_CHUNK]] + [zeros] * (nchunk - 1 - c),
                            axis=0) for c in range(nchunk)], axis=1))
    return q_slabs, k_in_heads, k4_heads, dec


def _gla_kernel(kf_ref, vf_ref, qf_ref, rf_ref, kb_ref, vb_ref, qb_ref, rb_ref, w2_ref, b_ref,
                tri_ref, keep_ref, of_ref, ob_ref, sf_ref, sb_ref, a_ref, u_ref):
    @pl.when(pl.program_id(1) == 0)
    def _():
        sf_ref[...] = jnp.zeros_like(sf_ref)
        sb_ref[...] = jnp.zeros_like(sb_ref)

    heads = sf_ref.shape[0]
    nchunk = kf_ref.shape[0] // B_CHUNK
    v_refs, o_refs, st_refs = (vf_ref, vb_ref), (of_ref, ob_ref), (sf_ref, sb_ref)
    prep = [_gla_prep(kf_ref, qf_ref, rf_ref, w2_ref[0], b_ref[0], tri_ref.at[0], False),
            _gla_prep(kb_ref, qb_ref, rb_ref, w2_ref[1], b_ref[1], tri_ref.at[1], True)]
    jobs = [(d, h) for d in range(2) for h in range(heads)]
    for d, h in jobs:
        q_slabs, k_in_heads, k4_heads, _ = prep[d]
        v_h = v_refs[d][:, h * B_DV:(h + 1) * B_DV]
        a_ref[d * heads + h] = _dot_nt(q_slabs[h], k_in_heads[h])
        u_ref[d * heads + h] = _dot_tn(v_h, k4_heads[h])
    for d, h in jobs:
        vcols = slice(h * B_DV, (h + 1) * B_DV)
        attn = jnp.where(keep_ref[d] > 0.5, a_ref[d * heads + h], 0.0).astype(BF16)
        o_refs[d][:, vcols] = _dot(attn, v_refs[d][:, vcols])
    for d, h in jobs:
        q_s, dec = prep[d][0][h], prep[d][3]
        slab = slice((h // 2) * LANES, (h // 2 + 1) * LANES)
        vcols = slice(h * B_DV, (h + 1) * B_DV)
        st = st_refs[d][h]
        for c in (range(nchunk - 1, -1, -1) if d else range(nchunk)):
            rows = slice(c * B_CHUNK, (c + 1) * B_CHUNK)
            o_refs[d][rows, vcols] += _dot_nt(q_s[rows], st.astype(BF16))
            st = st * dec[c][:, slab] + u_ref[d * heads + h, :, c * LANES:(c + 1) * LANES]
        st_refs[d][h] = st


def _chunk_triangles(tb):
    t = np.arange(tb)[:, None]
    s = np.arange(tb)[None, :]
    same = (t // B_CHUNK) == (s // B_CHUNK)
    return np.stack([same & (s <= t), same & (s >= t)]).astype(np.float32)


def _gla(z, r, w2pad, bias, batch, seq, ctx_len, tb):
    n_rows = z.shape[0]
    heads = bias.shape[-1] // B_DK
    assert heads % 2 == 0 and 2 * B_DK == LANES
    kw, vw = heads * B_DK, heads * B_DV
    nlat, nctx = seq // tb, ctx_len // tb
    base = batch * nlat
    tri = _chunk_triangles(tb)

    def fwd(b, j):
        return jnp.where(j < nctx, base + b * nctx + j, b * nlat + (j - nctx))

    def bwd(b, j):
        return jnp.where(j < nctx, base + b * nctx + (nctx - 1 - j), b * nlat + (nlat - 1 - (j - nctx)))

    def specs(blk):
        return [pl.BlockSpec((tb, kw), lambda b, j: (blk(b, j), Z_BK // kw)),
                pl.BlockSpec((tb, vw), lambda b, j: (blk(b, j), Z_BV // vw)),
                pl.BlockSpec((tb, kw), lambda b, j: (blk(b, j), Z_BQ // kw)),
                pl.BlockSpec((tb, LANES), lambda b, j: (blk(b, j), 0))]

    return pl.pallas_call(
        _gla_kernel,
        out_shape=(jax.ShapeDtypeStruct((n_rows, vw), F32), jax.ShapeDtypeStruct((n_rows, vw), F32)),
        grid=(batch, nctx + nlat),
        in_specs=specs(fwd) + specs(bwd) + [
            pl.BlockSpec((2, LANES, kw), lambda b, j: (0, 0, 0)),
            pl.BlockSpec((2, 1, kw), lambda b, j: (0, 0, 0)),
            pl.BlockSpec((2, tb, tb), lambda b, j: (0, 0, 0)),
            pl.BlockSpec((2, tb, tb), lambda b, j: (0, 0, 0)),
        ],
        out_specs=(pl.BlockSpec((tb, vw), lambda b, j: (fwd(b, j), 0)),
                   pl.BlockSpec((tb, vw), lambda b, j: (bwd(b, j), 0))),
        scratch_shapes=[pltpu.VMEM((heads, B_DV, LANES), F32), pltpu.VMEM((heads, B_DV, LANES), F32),
                        pltpu.VMEM((2 * heads, tb, tb), F32),
                        pltpu.VMEM((2 * heads, B_DV, (tb // B_CHUNK) * LANES), F32)],
        compiler_params=_cparams(("parallel", "arbitrary")),
        name="gla",
    )(z, z, z, r, z, z, z, r, w2pad.astype(BF16), bias, jnp.asarray(tri, BF16), jnp.asarray(tri, F32))


def _attn_kernel(sink_ref, q_ref, kp_ref, kc_ref, kn_ref, kx_ref, vp_ref, vc_ref, vn_ref, vx_ref,
                 bias_ref, o_ref, s_ref):
    tq = q_ref.shape[0]
    n_heads = q_ref.shape[1] // HEAD_DIM
    kv_heads = kc_ref.shape[1] // HEAD_DIM
    group = n_heads // kv_heads
    n_loc = bias_ref.shape[1]

    def kv_cols(h):
        kh = h // group
        return slice(kh * HEAD_DIM, (kh + 1) * HEAD_DIM)

    def scores(kh):
        cols = kv_cols(kh * group)
        k_all = jnp.concatenate([kp_ref[:, cols], kc_ref[:, cols], kn_ref[:, cols], kx_ref[:, cols]], axis=0)
        q = jnp.concatenate([q_ref[:, h * HEAD_DIM:(h + 1) * HEAD_DIM]
                             for h in range(kh * group, (kh + 1) * group)], axis=0)
        s_ref[kh * group * tq:(kh + 1) * group * tq, :] = _dot_nt(q, k_all)

    def softmax_pv(h):
        cols = kv_cols(h)
        rows = slice(h * tq, (h + 1) * tq)
        v_all = jnp.concatenate([vp_ref[:, cols], vc_ref[:, cols], vn_ref[:, cols], vx_ref[:, cols]], axis=0)
        sink = sink_ref[h] * LOG2E
        sg = jnp.concatenate([s_ref[rows, :n_loc] + bias_ref[...], s_ref[rows, n_loc:]], axis=1)
        m = jnp.maximum(jnp.max(sg, axis=-1, keepdims=True), sink)
        e = jnp.exp2(sg - m)
        den = jnp.sum(e, axis=-1, keepdims=True) + jnp.exp2(sink - m)
        o_ref[:, h * HEAD_DIM:(h + 1) * HEAD_DIM] = (_dot(e.astype(BF16), v_all) / den).astype(o_ref.dtype)

    for kh in range(kv_heads):
        scores(kh)
    for h in range(n_heads):
        softmax_pv(h)


def _window_bias(tq):
    i = np.arange(tq)[:, None]
    j = np.arange(3 * tq)[None, :]
    band = (j >= i) & (j <= i + 2 * tq)
    cases = [band & (j >= tq), band, band & (j < 2 * tq), np.zeros_like(band)]
    return jnp.asarray(np.where(np.stack(cases), 0.0, NEG_BIG), F32)


def _attn(z, sink, batch, seq, ctx_len, with_ctx):
    n_heads = sink.shape[0]
    qw = n_heads * HEAD_DIM
    kvw = C_KVW
    tq = C_BLOCK
    nb = seq // tq
    cpb = ctx_len // tq if with_ctx else 0
    assert nb >= 2 and ctx_len % tq == 0
    ctx_base = batch * seq // ctx_len

    def own(b, n):
        return jnp.where(n < nb, b * nb + n, batch * nb + b * cpb + (n - nb))

    def prev(b, n):
        return b * nb + jnp.clip(n - 1, 0, nb - 1)

    def cur(b, n):
        return b * nb + jnp.minimum(n, nb - 1)

    def nxt(b, n):
        return b * nb + jnp.minimum(n + 1, nb - 1)

    def zspec(blk, col0):
        return pl.BlockSpec((tq, kvw), lambda b, n: (blk(b, n), col0 // kvw))

    def bias_case(b, n):
        return jnp.where(n >= nb, 3, jnp.where(n == 0, 0, jnp.where(n == nb - 1, 2, 1)))

    return pl.pallas_call(
        _attn_kernel,
        out_shape=jax.ShapeDtypeStruct((batch * (nb + cpb) * tq, qw), BF16),
        grid=(batch, nb + cpb),
        in_specs=[
            pl.BlockSpec(memory_space=pltpu.SMEM),
            pl.BlockSpec((tq, qw), lambda b, n: (own(b, n), Z_CQ // qw)),
            zspec(prev, Z_CK), zspec(cur, Z_CK), zspec(nxt, Z_CK),
            pl.BlockSpec((ctx_len, kvw), lambda b, n: (ctx_base + b, Z_CK // kvw)),
            zspec(prev, Z_CV), zspec(cur, Z_CV), zspec(nxt, Z_CV),
            pl.BlockSpec((ctx_len, kvw), lambda b, n: (ctx_base + b, Z_CV // kvw)),
            pl.BlockSpec((None, tq, 3 * tq), lambda b, n: (bias_case(b, n), 0, 0)),
        ],
        out_specs=pl.BlockSpec((tq, qw), lambda b, n: (own(b, n), 0)),
        scratch_shapes=[pltpu.VMEM((n_heads * tq, 3 * tq + ctx_len), F32)],
        compiler_params=_cparams(("parallel", "parallel")),
        name="window_attn",
    )(sink, z, z, z, z, z, z, z, z, z, _window_bias(tq))


MERGE_CHUNKS = 4


def _merge_kernel(x_ref, m_ref, a_ref, of_ref, ob_ref, og_ref, c_ref, ga_ref, gb_ref, gc_ref,
                  bg_ref, wa_ref, wb_ref, wc_ref, wo_ref, o_ref):
    heads = of_ref.shape[1] // B_DV
    o = of_ref[...] + ob_ref[...]
    bn = jnp.concatenate([_rms_head(o[:, h * B_DV:(h + 1) * B_DV], bg_ref[...]) for h in range(heads)], axis=1)
    b = (bn * _silu(og_ref[...].astype(F32))).astype(BF16)
    a, c = a_ref[...], c_ref[...]
    d = o_ref.shape[1]
    cw = d // MERGE_CHUNKS

    def merged_cols(k):
        cols = slice(k * cw, (k + 1) * cw)
        return (jax.nn.sigmoid(ga_ref[:, cols].astype(F32)) * _dot(a, wa_ref[:, cols])
                + jax.nn.sigmoid(gb_ref[:, cols].astype(F32)) * _dot(b, wb_ref[:, cols])
                + jax.nn.sigmoid(gc_ref[:, cols].astype(F32)) * _dot(c, wc_ref[:, cols])).astype(BF16)

    pending = merged_cols(0)
    mix = None
    for k in range(MERGE_CHUNKS):
        ready = pending
        if k + 1 < MERGE_CHUNKS:
            pending = merged_cols(k + 1)
        part = _dot(ready, wo_ref[k * cw:(k + 1) * cw, :])
        mix = part if mix is None else mix + part
    o_ref[...] = x_ref[...] + m_ref[2] * mix


def _merge(x, n_rows, mod, layer, row_fn, z, a, o_f, o_b, c, b_norm_g, wa, wb, wc, wo, tm):
    d = x.shape[1]
    aw, bw, cw = a.shape[1], o_f.shape[1], c.shape[1]

    def const(rows):
        return pl.BlockSpec((None, rows, d), lambda i: (layer, 0, 0), pipeline_mode=pl.Buffered(1))

    return pl.pallas_call(
        _merge_kernel,
        out_shape=jax.ShapeDtypeStruct((n_rows, d), F32),
        grid=(n_rows // tm,),
        in_specs=[
            pl.BlockSpec((tm, d), lambda i: (i, 0)),
            _mod_spec(layer, 1, row_fn, d),
            pl.BlockSpec((tm, aw), lambda i: (i, 0)),
            pl.BlockSpec((tm, bw), lambda i: (i, 0)),
            pl.BlockSpec((tm, bw), lambda i: (i, 0)),
            pl.BlockSpec((tm, bw), lambda i: (i, Z_BG // bw)),
            pl.BlockSpec((tm, cw), lambda i: (i, 0)),
            pl.BlockSpec((tm, d), lambda i: (i, Z_GATE // d)),
            pl.BlockSpec((tm, d), lambda i: (i, Z_GATE // d + 1)),
            pl.BlockSpec((tm, d), lambda i: (i, Z_GATE // d + 2)),
            pl.BlockSpec((1, B_DV), lambda i: (0, 0)),
            const(aw), const(bw), const(cw), const(d),
        ],
        out_specs=pl.BlockSpec((tm, d), lambda i: (i, 0)),
        compiler_params=_cparams(("parallel",)),
        name="merge",
    )(x, mod, a, o_f, o_b, z, c, z, z, z, b_norm_g.reshape(1, B_DV), wa, wb, wc, wo)


def _rope_tables(seq, pad_rows):
    half = HEAD_DIM // 4
    t = np.arange(seq)
    pos = np.stack([t // GRID_W, t % GRID_W], axis=1).astype(np.float32)
    inv_freq = jnp.asarray(ROPE_BASE, F32) ** (-jnp.arange(half, dtype=F32) / half)
    ang = jnp.asarray(pos)[:, :, None] * inv_freq[None, None, :]
    cos = jnp.cos(ang)
    sin = jnp.sin(ang)
    cos_t = jnp.concatenate([cos, cos], axis=-1).reshape(seq, HEAD_DIM)
    sin_t = jnp.concatenate([-sin, sin], axis=-1).reshape(seq, HEAD_DIM)
    cos_t = jnp.concatenate([cos_t, jnp.ones((pad_rows, HEAD_DIM), F32)], axis=0)
    sin_t = jnp.concatenate([sin_t, jnp.zeros((pad_rows, HEAD_DIM), F32)], axis=0)
    return cos_t, sin_t


def _reorder_w_in(w, d):
    bkw, bvw, ckvw, aw, cqw = 256, 512, 256, 512, 1024
    o_bk = 0
    o_bv = o_bk + bkw
    o_ck = o_bv + bvw
    o_cv = o_ck + ckvw
    o_au = o_cv + ckvw
    o_av = o_au + aw
    o_bq = o_av + aw
    o_bg = o_bq + bkw
    o_cq = o_bg + bvw
    o_gate = o_cq + cqw
    seg = lambda s, width: w[..., s:s + width]
    return jnp.concatenate([seg(o_cq, cqw), seg(o_ck, ckvw), seg(o_cv, ckvw), seg(o_bk, bkw), seg(o_bq, bkw), seg(o_bv, bvw),
                            seg(o_au, aw), seg(o_av, aw), seg(o_bg, bvw), seg(o_gate, 3 * d)], axis=-1)


def kernel(x, c, ctx, c_ctx, w_ada, b_ada, norm_g, w_ffn_up, w_ffn_down, w_in, a_v_gain, a_ws, a_bs,
           b_decay_w1, b_decay_w2, b_decay_b, b_norm_g, c_q_gain, c_k_gain, c_sink,
           w_br_a, w_br_b, w_br_c, w_out):
    batch, seq, d = x.shape
    ctx_len = ctx.shape[1]
    depth = w_ada.shape[0]
    assert d == 2048 and batch < MOD_ROWS and seq % 256 == 0 and ctx_len == 256
    n_lat, n_ctx = batch * seq, batch * ctx_len
    n_all = n_lat + n_ctx
    tm_ffn, tm_in, tm_merge = 1024, 512, 256
    for t in (tm_ffn, tm_in, tm_merge):
        assert seq % t == 0 and n_ctx % t == 0

    def row_fn(t):
        return lambda i: jnp.minimum(i * t // seq, batch)

    c8 = jnp.zeros((MOD_ROWS, d), F32).at[:batch].set(c).at[batch].set(c_ctx)
    mod = _with_norm_gains(_mod_table(c8, w_ada, b_ada), norm_g)
    cos_t, sin_t = _rope_tables(seq, tm_in)

    w_up = w_ffn_up.astype(BF16)
    w_dn = w_ffn_down.astype(BF16)
    w_in_r = _reorder_w_in(w_in, d).astype(BF16)
    w1 = jnp.zeros((depth, d, LANES), F32).at[:, :, :B_RANK].set(b_decay_w1[:, 0]).at[
        :, :, B_RANK:2 * B_RANK].set(b_decay_w1[:, 1]).astype(BF16)
    kw = b_decay_w2.shape[-1]
    w2pad = jnp.zeros((depth, 2, LANES, kw), F32).at[:, 0, :B_RANK].set(b_decay_w2[:, 0]).at[
        :, 1, B_RANK:2 * B_RANK].set(b_decay_w2[:, 1])
    n_qh = c_sink.shape[1]
    qk_gain = jnp.concatenate([jnp.tile(c_q_gain * (HEAD_DIM ** -0.5 * LOG2E), (1, n_qh)),
                               jnp.tile(c_k_gain, (1, C_KVW // HEAD_DIM))], axis=1)
    wa, wb, wc, wo = (w.astype(BF16) for w in (w_br_a, w_br_b, w_br_c, w_out))

    srcs = [x.reshape(n_lat, d), ctx.reshape(n_ctx, d)]
    for l in range(depth):
        last = l == depth - 1
        n_mix = n_lat if last else n_all
        tm_first = tm_ffn if len(srcs) == 1 else tm_ffn // 2
        xs = _ffn(srcs, n_all, mod, l, 0, row_fn(tm_first), w_up, w_dn, tm_first, TF)
        z, r = _inproj(xs, mod, l, row_fn(tm_in), w_in_r, w1, qk_gain[l:l + 1], cos_t, sin_t,
                       n_lat, seq, tm_in, 2 * QK_COLS, ctx_state_only=last)
        a_out = _gmlp(z, n_mix, a_v_gain[l], a_ws[l], a_bs[l], 512)
        o_f, o_b = _gla(z, r, w2pad[l], b_decay_b[l].reshape(2, 1, kw), batch, seq, ctx_len, 256)
        c_out = _attn(z, c_sink[l], batch, seq, ctx_len, with_ctx=not last)
        xs = _merge(xs, n_mix, mod, l, row_fn(tm_merge), z, a_out, o_f, o_b, c_out, b_norm_g[l],
                    wa, wb, wc, wo, tm_merge)
        xs = _ffn([xs], n_mix, mod, l, 1, row_fn(tm_ffn), w_up, w_dn, tm_ffn, TF)
        srcs = [xs]
    return xs.reshape(batch, seq, d)
```

```python
import functools

import jax
import jax.numpy as jnp
import numpy as np
from jax import lax
from jax.experimental import pallas as pl
from jax.experimental.pallas import tpu as pltpu

F32 = jnp.float32
BF16 = jnp.bfloat16

HEAD_DIM = 128
EPS = 1e-6
N_MOD = 9
GRID_W = 64
A_CHUNK = 128
B_DK = 64
B_DV = 128
B_RANK = 16
B_TAU = 16.0
B_CHUNK = 64
C_BLOCK = 128
ROPE_BASE = 10000.0
MOD_ROWS = 8
LANES = 128
NEG_BIG = -1e30
LOG2E = 1.4426950408889634
LN2 = 0.6931471805599453
VMEM_LIMIT = 56 * 1024 * 1024

Z_CQ, Z_CK, Z_CV, Z_BK, Z_BQ, Z_BV, Z_AU, Z_AV, Z_BG, Z_GATE = (
    0, 1024, 1280, 1536, 1792, 2048, 2560, 3072, 3584, 4096)
QK_COLS = Z_CV
C_KVW = Z_CV - Z_CK


def _cparams(sem):
    return pltpu.CompilerParams(dimension_semantics=sem, vmem_limit_bytes=VMEM_LIMIT)


def _dot(a, b):
    return jnp.dot(a, b, preferred_element_type=F32)


def _dot_nt(a, b):
    return lax.dot_general(a, b, (((1,), (1,)), ((), ())), preferred_element_type=F32)


def _dot_tn(a, b):
    return lax.dot_general(a, b, (((0,), (0,)), ((), ())), preferred_element_type=F32)


def _dot_f32(a, b):
    return jnp.dot(a, b, preferred_element_type=F32, precision=lax.Precision.HIGHEST)


def _silu(x):
    return x * jax.nn.sigmoid(x)


def _rms_mod(x, gain, shift, scale):
    ms = jnp.mean(x * x, axis=-1, keepdims=True)
    y = x * lax.rsqrt(ms + EPS) * gain
    return y * (1.0 + scale) + shift


NORM_ROWS = 64


def _rms_mod_to(x_ref, m_ref, h_ref):
    shift = m_ref[0]
    gain = m_ref[3] * (1.0 + m_ref[1])

    def one_pass(c, carry):
        rows = pl.ds(pl.multiple_of(c * NORM_ROWS, NORM_ROWS), NORM_ROWS)
        x = x_ref[rows, :]
        rs = lax.rsqrt(jnp.mean(x * x, axis=-1, keepdims=True) + EPS)
        h_ref[rows, :] = (x * rs * gain + shift).astype(BF16)
        return carry

    lax.fori_loop(0, x_ref.shape[0] // NORM_ROWS, one_pass, 0, unroll=2)


def _rms_head(xh, gain):
    ms = jnp.mean(xh * xh, axis=-1, keepdims=True)
    return xh * lax.rsqrt(ms + EPS) * gain


def _mod_kernel(c_ref, w_ref, b_ref, o_ref):
    act = _silu(c_ref[...])
    o_ref[...] = _dot(act.astype(BF16), w_ref[...].astype(BF16)) + b_ref[...]


def _mod_table(c8, w_ada, b_ada):
    depth, d, nd = w_ada.shape
    tn = 1024
    nj = d // tn
    out = pl.pallas_call(
        _mod_kernel,
        out_shape=jax.ShapeDtypeStruct((depth, N_MOD, MOD_ROWS, d), F32),
        grid=(depth, N_MOD, nj),
        in_specs=[
            pl.BlockSpec((MOD_ROWS, d), lambda l, k, j: (0, 0)),
            pl.BlockSpec((None, d, tn), lambda l, k, j: (l, 0, k * nj + j)),
            pl.BlockSpec((None, 1, tn), lambda l, k, j: (l, 0, k * nj + j)),
        ],
        out_specs=pl.BlockSpec((None, None, MOD_ROWS, tn), lambda l, k, j: (l, k, 0, j)),
        compiler_params=_cparams(("parallel", "parallel", "parallel")),
        name="mod_table",
    )(c8, w_ada, b_ada.reshape(depth, 1, nd))
    return out


MOD_GROUP = 4
TF = 512


def _with_norm_gains(mod, norm_g):
    depth, _, rows, d = mod.shape
    subs = norm_g.shape[1]
    gains = jnp.broadcast_to(norm_g[:, :, None, None, :], (depth, subs, 1, rows, d))
    table = jnp.concatenate([mod.reshape(depth, subs, N_MOD // subs, rows, d), gains], axis=2)
    return table.reshape(depth, subs * MOD_GROUP, rows, 1, d)


def _mod_spec(layer, sub, row_fn, d):
    return pl.BlockSpec((None, MOD_GROUP, None, 1, d), lambda i, *_: (layer, sub, row_fn(i), 0, 0))


def _ffn_kernel(*refs, tiles_per_source):
    n_src = len(tiles_per_source)
    x_refs = refs[:n_src]
    m_ref, wg_ref, wv_ref, wd_ref, o_ref, h_ref = refs[n_src:]
    acc_ref = o_ref
    i, f = pl.program_id(0), pl.program_id(1)

    def with_own_source(fn):
        start = 0
        for x_ref, n_tiles in zip(x_refs, tiles_per_source):
            if n_src == 1:
                fn(x_ref)
            else:
                pl.when(jnp.logical_and(i >= start, i < start + n_tiles))(lambda x_ref=x_ref: fn(x_ref))
            start += n_tiles

    last = pl.num_programs(1) - 1

    def prologue(x_ref):
        _rms_mod_to(x_ref, m_ref, h_ref)

    def chunk():
        h = h_ref[...]
        act = (_silu(_dot(h, wg_ref[...])) * _dot(h, wv_ref[...])).astype(BF16)
        return _dot(act, wd_ref[...])

    @pl.when(f == 0)
    def _():
        with_own_source(prologue)
        acc_ref[...] = chunk()

    if n_src == 1:
        @pl.when(jnp.logical_and(f > 0, f < last))
        def _():
            acc_ref[...] += chunk()

        @pl.when(f == last)
        def _():
            o_ref[...] = x_refs[0][...] + 0.5 * m_ref[2] * (acc_ref[...] + chunk())
    else:
        @pl.when(f > 0)
        def _():
            acc_ref[...] += chunk()

        def epilogue(x_ref):
            o_ref[...] = x_ref[...] + 0.5 * m_ref[2] * acc_ref[...]

        @pl.when(f == last)
        def _():
            with_own_source(epilogue)


def _ffn(xs, n_rows, mod, layer, which, row_fn, w_up, w_down, tm, tf):
    d = xs[0].shape[1]
    nf = w_down.shape[2] // tf
    nt = n_rows // tm
    tiles, starts = [], []
    for x in xs:
        starts.append(sum(tiles))
        tiles.append(min(x.shape[0] // tm, nt - sum(tiles)))

    def src_spec(start, n_tiles):
        mode = pl.Buffered(1) if (n_tiles == 1 and len(xs) > 1) else None
        return pl.BlockSpec((tm, d), lambda i, f: (jnp.clip(i - start, 0, n_tiles - 1), 0), pipeline_mode=mode)

    return pl.pallas_call(
        functools.partial(_ffn_kernel, tiles_per_source=tuple(tiles)),
        out_shape=jax.ShapeDtypeStruct((n_rows, d), F32),
        grid=(nt, nf),
        in_specs=[src_spec(s, t) for s, t in zip(starts, tiles)] + [
            _mod_spec(layer, 2 * which, row_fn, d),
            pl.BlockSpec((None, None, d, tf), lambda i, f: (layer, which, 0, f)),
            pl.BlockSpec((None, None, d, tf), lambda i, f: (layer, which, 0, nf + f)),
            pl.BlockSpec((None, None, tf, d), lambda i, f: (layer, which, f, 0)),
        ],
        out_specs=pl.BlockSpec((tm, d), lambda i, f: (i, 0)),
        scratch_shapes=[pltpu.VMEM((tm, d), BF16)],
        compiler_params=_cparams(("parallel", "arbitrary")),
        name="ffn",
    )(*xs, mod, w_up, w_up, w_down)


def _inproj_kernel(x_ref, m_ref, w_ref, w1_ref, qkg_ref, cos_ref, sin_ref, z_ref, r_ref, h_ref, *,
                   full_tiles):
    i, n = pl.program_id(0), pl.program_id(1)

    @pl.when(jnp.logical_and(n != 0, i < full_tiles))
    def _():
        z_ref[...] = _dot(h_ref[...], w_ref[...]).astype(z_ref.dtype)

    @pl.when(jnp.logical_and(n != 0, i >= full_tiles))
    def _():
        z_ref[...] = jnp.zeros_like(z_ref)

    @pl.when(n == 0)
    def _():
        _rms_mod_to(x_ref, m_ref, h_ref)
        h = h_ref[...]
        r_ref[...] = _dot(h, w1_ref[...])
        zt = _dot(h, w_ref[:, :QK_COLS])
        if w_ref.shape[1] > QK_COLS:
            z_ref[:, QK_COLS:] = _dot(h, w_ref[:, QK_COLS:]).astype(z_ref.dtype)
        pair = 2 * HEAD_DIM
        r = lax.broadcasted_iota(jnp.int32, (pair, pair), 0)
        c = lax.broadcasted_iota(jnp.int32, (pair, pair), 1)
        head_sum = jnp.where(jnp.bitwise_and(jnp.bitwise_xor(r, c), HEAD_DIM) == 0, 1.0, 0.0).astype(BF16)
        swap = jnp.where(c == jnp.bitwise_xor(r, HEAD_DIM // 4), 1.0, 0.0).astype(BF16)
        cos = jnp.concatenate([cos_ref[...]] * 2, axis=1)
        sin = jnp.concatenate([sin_ref[...]] * 2, axis=1)
        for p in range(QK_COLS // pair):
            cols = slice(p * pair, (p + 1) * pair)
            y = zt[:, cols]
            ms = _dot((y * y).astype(BF16), head_sum) * (1.0 / HEAD_DIM)
            y = y * lax.rsqrt(ms + EPS) * qkg_ref[:, cols]
            z_ref[:, cols] = (y * cos + _dot(y.astype(BF16), swap) * sin).astype(z_ref.dtype)


def _inproj(x, mod, layer, row_fn, w_in, w1, qk_gain, cos_t, sin_t, n_lat, seq, tm, tn, ctx_state_only):
    n_rows, d = x.shape
    ncols = w_in.shape[2]
    assert tn % QK_COLS == 0 and ncols % tn == 0 and Z_BV + 512 <= tn
    rope_blocks = seq // tm
    nt = n_rows // tm
    full_tiles = n_lat // tm if ctx_state_only else nt

    def rope_blk(i, n):
        return jnp.where(i * tm < n_lat, (i % rope_blocks), rope_blocks)

    def w_blk(i, n):
        return jnp.where(i < full_tiles, n, 0)

    return pl.pallas_call(
        functools.partial(_inproj_kernel, full_tiles=full_tiles),
        out_shape=(jax.ShapeDtypeStruct((n_rows, ncols), BF16),
                   jax.ShapeDtypeStruct((n_rows, LANES), F32)),
        grid=(n_rows // tm, ncols // tn),
        in_specs=[
            pl.BlockSpec((tm, d), lambda i, n: (i, 0)),
            _mod_spec(layer, 1, row_fn, d),
            pl.BlockSpec((None, d, tn), lambda i, n: (layer, 0, w_blk(i, n))),
            pl.BlockSpec((None, d, LANES), lambda i, n: (layer, 0, 0)),
            pl.BlockSpec((1, QK_COLS), lambda i, n: (0, 0)),
            pl.BlockSpec((tm, HEAD_DIM), lambda i, n: (rope_blk(i, n), 0)),
            pl.BlockSpec((tm, HEAD_DIM), lambda i, n: (rope_blk(i, n), 0)),
        ],
        out_specs=(pl.BlockSpec((tm, tn), lambda i, n: (i, n)),
                   pl.BlockSpec((tm, LANES), lambda i, n: (i, 0))),
        scratch_shapes=[pltpu.VMEM((tm, d), BF16)],
        compiler_params=_cparams(("parallel", "arbitrary")),
        name="inproj",
    )(x, mod, w_in, w1, qk_gain, cos_t, sin_t)


def _gmlp_kernel(u_ref, v_ref, gain_ref, ws_ref, bs_ref, o_ref):
    tg, width = u_ref.shape
    groups = width // HEAD_DIM
    u = jax.nn.gelu(u_ref[...].astype(F32))
    v = jax.nn.gelu(v_ref[...].astype(F32))
    mu = jnp.mean(v, axis=-1, keepdims=True)
    vc = v - mu
    var = jnp.mean(vc * vc, axis=-1, keepdims=True)
    vn = (vc * lax.rsqrt(var + EPS) * gain_ref[...]).astype(BF16)
    nchunk = tg // A_CHUNK
    for g in range(groups):
        cols = slice(g * HEAD_DIM, (g + 1) * HEAD_DIM)
        v_g = jnp.concatenate([vn[c * A_CHUNK:(c + 1) * A_CHUNK, cols] for c in range(nchunk)], axis=1)
        mixed = _dot(ws_ref[g], v_g)
        for c in range(nchunk):
            rows = slice(c * A_CHUNK, (c + 1) * A_CHUNK)
            o_ref[rows, cols] = (u[rows, cols] * (mixed[:, c * HEAD_DIM:(c + 1) * HEAD_DIM] + bs_ref[g])
                                 ).astype(o_ref.dtype)


def _gmlp(z, n_rows, a_v_gain, a_ws, a_bs, tg):
    groups = a_ws.shape[0]
    width = groups * HEAD_DIM
    bs_full = jnp.broadcast_to(a_bs[:, :, None], (groups, A_CHUNK, HEAD_DIM)).astype(F32)
    return pl.pallas_call(
        _gmlp_kernel,
        out_shape=jax.ShapeDtypeStruct((n_rows, width), BF16),
        grid=(n_rows // tg,),
        in_specs=[
            pl.BlockSpec((tg, width), lambda i: (i, Z_AU // width)),
            pl.BlockSpec((tg, width), lambda i: (i, Z_AV // width)),
            pl.BlockSpec((1, width), lambda i: (0, 0)),
            pl.BlockSpec((groups, A_CHUNK, A_CHUNK), lambda i: (0, 0, 0)),
            pl.BlockSpec((groups, A_CHUNK, HEAD_DIM), lambda i: (0, 0, 0)),
        ],
        out_specs=pl.BlockSpec((tg, width), lambda i: (i, 0)),
        compiler_params=_cparams(("parallel",)),
        name="gmlp",
    )(z, z, a_v_gain.reshape(1, width), a_ws.astype(BF16), bs_full)


def _gla_prep(k_ref, q_ref, r_ref, w2, bias, tri_ref, reverse):
    tb = k_ref.shape[0]
    heads = k_ref.shape[1] // B_DK
    nchunk = tb // B_CHUNK
    logit = _dot(r_ref[...].astype(BF16), w2) + bias
    soft = jnp.log2(1.0 + jnp.exp2(jnp.abs(logit) * -LOG2E))
    g = jnp.minimum(logit, 0.0) * (1.0 / B_TAU) - soft * (LN2 / B_TAU)
    g_hi = g.astype(BF16)
    g_lo = (g - g_hi.astype(F32)).astype(BF16)
    tri = tri_ref[...]
    bc = _dot(tri, g_hi) + _dot(tri, g_lo)
    last = [c * B_CHUNK + (0 if reverse else B_CHUNK - 1) for c in range(nchunk)]
    tots = [bc[r:r + 1, :] for r in last]
    tot_rows = jnp.concatenate([jnp.broadcast_to(t, (B_CHUNK, t.shape[1])) for t in tots], axis=0)
    kf = k_ref[...].astype(F32)
    q_in = (q_ref[...].astype(F32) * jnp.exp(bc) * (B_DK ** -0.5)).astype(BF16)
    k_in = kf * jnp.exp(-bc)
    k_out = kf * jnp.exp(tot_rows - bc)
    dec = [jnp.exp(t) for t in tots]
    lane = lax.broadcasted_iota(jnp.int32, (tb, LANES), 1)
    zeros = jnp.zeros((B_CHUNK, LANES), BF16)
    q_slabs, k_in_heads, k4_heads = [], [], []
    for h in range(heads):
        slab = slice((h // 2) * LANES, (h // 2 + 1) * LANES)
        mine = (lane >= B_DK) if h % 2 else (lane < B_DK)
        q_slabs.append(q_in[:, slab])
        k_in_heads.append(jnp.where(mine, k_in[:, slab], 0.0).astype(BF16))
        k_out_h = jnp.where(mine, k_out[:, slab], 0.0).astype(BF16)
        k4_heads.append(jnp.concatenate([
            jnp.concatenate([zeros] * c + [k_out_h[c * B_CHUNK:(c + 1) * B_CHUNK]] + [zeros] * (nchunk - 1 - c),
                            axis=0) for c in range(nchunk)], axis=1))
    return q_slabs, k_in_heads, k4_heads, dec


def _gla_kernel(kf_ref, vf_ref, qf_ref, rf_ref, kb_ref, vb_ref, qb_ref, rb_ref, w2_ref, b_ref,
                tri_ref, keep_ref, of_ref, ob_ref, sf_ref, sb_ref, a_ref, u_ref):
    @pl.when(pl.program_id(1) == 0)
    def _():
        sf_ref[...] = jnp.zeros_like(sf_ref)
        sb_ref[...] = jnp.zeros_like(sb_ref)

    heads = sf_ref.shape[0]
    nchunk = kf_ref.shape[0] // B_CHUNK
    v_refs, o_refs, st_refs = (vf_ref, vb_ref), (of_ref, ob_ref), (sf_ref, sb_ref)
    prep = [_gla_prep(kf_ref, qf_ref, rf_ref, w2_ref[0], b_ref[0], tri_ref.at[0], False),
            _gla_prep(kb_ref, qb_ref, rb_ref, w2_ref[1], b_ref[1], tri_ref.at[1], True)]
    jobs = [(d, h) for d in range(2) for h in range(heads)]
    for d, h in jobs:
        q_slabs, k_in_heads, _, _ = prep[d]
        a_ref[d * heads + h] = _dot_nt(q_slabs[h], k_in_heads[h])
    for d, h in jobs:
        v_h = v_refs[d][:, h * B_DV:(h + 1) * B_DV]
        u_ref[d * heads + h] = _dot_tn(v_h, prep[d][2][h])
    for d, h in jobs:
        vcols = slice(h * B_DV, (h + 1) * B_DV)
        attn = jnp.where(keep_ref[d] > 0.5, a_ref[d * heads + h], 0.0).astype(BF16)
        o_refs[d][:, vcols] = _dot(attn, v_refs[d][:, vcols])
    for d, h in jobs:
        q_s, dec = prep[d][0][h], prep[d][3]
        slab = slice((h // 2) * LANES, (h // 2 + 1) * LANES)
        vcols = slice(h * B_DV, (h + 1) * B_DV)
        st = st_refs[d][h]
        for c in (range(nchunk - 1, -1, -1) if d else range(nchunk)):
            rows = slice(c * B_CHUNK, (c + 1) * B_CHUNK)
            o_refs[d][rows, vcols] += _dot_nt(q_s[rows], st.astype(BF16))
            st = st * dec[c][:, slab] + u_ref[d * heads + h, :, c * LANES:(c + 1) * LANES]
        st_refs[d][h] = st


def _chunk_triangles(tb):
    t = np.arange(tb)[:, None]
    s = np.arange(tb)[None, :]
    same = (t // B_CHUNK) == (s // B_CHUNK)
    return np.stack([same & (s <= t), same & (s >= t)]).astype(np.float32)


def _gla(z, r, w2pad, bias, batch, seq, ctx_len, tb):
    n_rows = z.shape[0]
    heads = bias.shape[-1] // B_DK
    assert heads % 2 == 0 and 2 * B_DK == LANES
    kw, vw = heads * B_DK, heads * B_DV
    nlat, nctx = seq // tb, ctx_len // tb
    base = batch * nlat
    tri = _chunk_triangles(tb)

    def fwd(b, j):
        return jnp.where(j < nctx, base + b * nctx + j, b * nlat + (j - nctx))

    def bwd(b, j):
        return jnp.where(j < nctx, base + b * nctx + (nctx - 1 - j), b * nlat + (nlat - 1 - (j - nctx)))

    def specs(blk):
        return [pl.BlockSpec((tb, kw), lambda b, j: (blk(b, j), Z_BK // kw)),
                pl.BlockSpec((tb, vw), lambda b, j: (blk(b, j), Z_BV // vw)),
                pl.BlockSpec((tb, kw), lambda b, j: (blk(b, j), Z_BQ // kw)),
                pl.BlockSpec((tb, LANES), lambda b, j: (blk(b, j), 0))]

    return pl.pallas_call(
        _gla_kernel,
        out_shape=(jax.ShapeDtypeStruct((n_rows, vw), F32), jax.ShapeDtypeStruct((n_rows, vw), F32)),
        grid=(batch, nctx + nlat),
        in_specs=specs(fwd) + specs(bwd) + [
            pl.BlockSpec((2, LANES, kw), lambda b, j: (0, 0, 0)),
            pl.BlockSpec((2, 1, kw), lambda b, j: (0, 0, 0)),
            pl.BlockSpec((2, tb, tb), lambda b, j: (0, 0, 0)),
            pl.BlockSpec((2, tb, tb), lambda b, j: (0, 0, 0)),
        ],
        out_specs=(pl.BlockSpec((tb, vw), lambda b, j: (fwd(b, j), 0)),
                   pl.BlockSpec((tb, vw), lambda b, j: (bwd(b, j), 0))),
        scratch_shapes=[pltpu.VMEM((heads, B_DV, LANES), F32), pltpu.VMEM((heads, B_DV, LANES), F32),
                        pltpu.VMEM((2 * heads, tb, tb), F32),
                        pltpu.VMEM((2 * heads, B_DV, (tb // B_CHUNK) * LANES), F32)],
        compiler_params=_cparams(("parallel", "arbitrary")),
        name="gla",
    )(z, z, z, r, z, z, z, r, w2pad.astype(BF16), bias, jnp.asarray(tri, BF16), jnp.asarray(tri, F32))


def _attn_kernel(sink_ref, q_ref, kp_ref, kc_ref, kn_ref, kx_ref, vp_ref, vc_ref, vn_ref, vx_ref,
                 bias_ref, o_ref, s_ref):
    tq = q_ref.shape[0]
    n_heads = q_ref.shape[1] // HEAD_DIM
    kv_heads = kc_ref.shape[1] // HEAD_DIM
    group = n_heads // kv_heads
    n_loc = bias_ref.shape[1]

    def kv_cols(h):
        kh = h // group
        return slice(kh * HEAD_DIM, (kh + 1) * HEAD_DIM)

    def scores(kh):
        cols = kv_cols(kh * group)
        k_all = jnp.concatenate([kp_ref[:, cols], kc_ref[:, cols], kn_ref[:, cols], kx_ref[:, cols]], axis=0)
        q = jnp.concatenate([q_ref[:, h * HEAD_DIM:(h + 1) * HEAD_DIM]
                             for h in range(kh * group, (kh + 1) * group)], axis=0)
        s_ref[kh * group * tq:(kh + 1) * group * tq, :] = _dot_nt(q, k_all)

    def softmax_pv(h):
        cols = kv_cols(h)
        rows = slice(h * tq, (h + 1) * tq)
        v_all = jnp.concatenate([vp_ref[:, cols], vc_ref[:, cols], vn_ref[:, cols], vx_ref[:, cols]], axis=0)
        sink = sink_ref[h] * LOG2E
        sg = jnp.concatenate([s_ref[rows, :n_loc] + bias_ref[...], s_ref[rows, n_loc:]], axis=1)
        m = jnp.maximum(jnp.max(sg, axis=-1, keepdims=True), sink)
        e = jnp.exp2(sg - m)
        den = jnp.sum(e, axis=-1, keepdims=True) + jnp.exp2(sink - m)
        o_ref[:, h * HEAD_DIM:(h + 1) * HEAD_DIM] = (_dot(e.astype(BF16), v_all) / den).astype(o_ref.dtype)

    for kh in range(kv_heads):
        scores(kh)
    for h in range(n_heads):
        softmax_pv(h)


def _window_bias(tq):
    i = np.arange(tq)[:, None]
    j = np.arange(3 * tq)[None, :]
    band = (j >= i) & (j <= i + 2 * tq)
    cases = [band & (j >= tq), band, band & (j < 2 * tq), np.zeros_like(band)]
    return jnp.asarray(np.where(np.stack(cases), 0.0, NEG_BIG), F32)


def _attn(z, sink, batch, seq, ctx_len, with_ctx):
    n_heads = sink.shape[0]
    qw = n_heads * HEAD_DIM
    kvw = C_KVW
    tq = C_BLOCK
    nb = seq // tq
    cpb = ctx_len // tq if with_ctx else 0
    assert nb >= 2 and ctx_len % tq == 0
    ctx_base = batch * seq // ctx_len

    def own(b, n):
        return jnp.where(n < nb, b * nb + n, batch * nb + b * cpb + (n - nb))

    def prev(b, n):
        return b * nb + jnp.clip(n - 1, 0, nb - 1)

    def cur(b, n):
        return b * nb + jnp.minimum(n, nb - 1)

    def nxt(b, n):
        return b * nb + jnp.minimum(n + 1, nb - 1)

    def zspec(blk, col0):
        return pl.BlockSpec((tq, kvw), lambda b, n: (blk(b, n), col0 // kvw))

    def bias_case(b, n):
        return jnp.where(n >= nb, 3, jnp.where(n == 0, 0, jnp.where(n == nb - 1, 2, 1)))

    return pl.pallas_call(
        _attn_kernel,
        out_shape=jax.ShapeDtypeStruct((batch * (nb + cpb) * tq, qw), BF16),
        grid=(batch, nb + cpb),
        in_specs=[
            pl.BlockSpec(memory_space=pltpu.SMEM),
            pl.BlockSpec((tq, qw), lambda b, n: (own(b, n), Z_CQ // qw)),
            zspec(prev, Z_CK), zspec(cur, Z_CK), zspec(nxt, Z_CK),
            pl.BlockSpec((ctx_len, kvw), lambda b, n: (ctx_base + b, Z_CK // kvw)),
            zspec(prev, Z_CV), zspec(cur, Z_CV), zspec(nxt, Z_CV),
            pl.BlockSpec((ctx_len, kvw), lambda b, n: (ctx_base + b, Z_CV // kvw)),
            pl.BlockSpec((None, tq, 3 * tq), lambda b, n: (bias_case(b, n), 0, 0)),
        ],
        out_specs=pl.BlockSpec((tq, qw), lambda b, n: (own(b, n), 0)),
        scratch_shapes=[pltpu.VMEM((n_heads * tq, 3 * tq + ctx_len), F32)],
        compiler_params=_cparams(("parallel", "parallel")),
        name="window_attn",
    )(sink, z, z, z, z, z, z, z, z, z, _window_bias(tq))


MERGE_CHUNKS = 4


def _merge_kernel(x_ref, m_ref, a_ref, of_ref, ob_ref, og_ref, c_ref, ga_ref, gb_ref, gc_ref,
                  bg_ref, wa_ref, wb_ref, wc_ref, wo_ref, o_ref):
    heads = of_ref.shape[1] // B_DV
    o = of_ref[...] + ob_ref[...]
    bn = jnp.concatenate([_rms_head(o[:, h * B_DV:(h + 1) * B_DV], bg_ref[...]) for h in range(heads)], axis=1)
    b = (bn * _silu(og_ref[...].astype(F32))).astype(BF16)
    a, c = a_ref[...], c_ref[...]
    d = o_ref.shape[1]
    cw = d // MERGE_CHUNKS

    def merged_cols(k):
        cols = slice(k * cw, (k + 1) * cw)
        return (jax.nn.sigmoid(ga_ref[:, cols].astype(F32)) * _dot(a, wa_ref[:, cols])
                + jax.nn.sigmoid(gb_ref[:, cols].astype(F32)) * _dot(b, wb_ref[:, cols])
                + jax.nn.sigmoid(gc_ref[:, cols].astype(F32)) * _dot(c, wc_ref[:, cols])).astype(BF16)

    pending = merged_cols(0)
    mix = None
    for k in range(MERGE_CHUNKS):
        ready = pending
        if k + 1 < MERGE_CHUNKS:
            pending = merged_cols(k + 1)
        part = _dot(ready, wo_ref[k * cw:(k + 1) * cw, :])
        mix = part if mix is None else mix + part
    o_ref[...] = x_ref[...] + m_ref[2] * mix


def _merge(x, n_rows, mod, layer, row_fn, z, a, o_f, o_b, c, b_norm_g, wa, wb, wc, wo, tm):
    d = x.shape[1]
    aw, bw, cw = a.shape[1], o_f.shape[1], c.shape[1]

    def const(rows):
        return pl.BlockSpec((None, rows, d), lambda i: (layer, 0, 0), pipeline_mode=pl.Buffered(1))

    return pl.pallas_call(
        _merge_kernel,
        out_shape=jax.ShapeDtypeStruct((n_rows, d), F32),
        grid=(n_rows // tm,),
        in_specs=[
            pl.BlockSpec((tm, d), lambda i: (i, 0)),
            _mod_spec(layer, 1, row_fn, d),
            pl.BlockSpec((tm, aw), lambda i: (i, 0)),
            pl.BlockSpec((tm, bw), lambda i: (i, 0)),
            pl.BlockSpec((tm, bw), lambda i: (i, 0)),
            pl.BlockSpec((tm, bw), lambda i: (i, Z_BG // bw)),
            pl.BlockSpec((tm, cw), lambda i: (i, 0)),
            pl.BlockSpec((tm, d), lambda i: (i, Z_GATE // d)),
            pl.BlockSpec((tm, d), lambda i: (i, Z_GATE // d + 1)),
            pl.BlockSpec((tm, d), lambda i: (i, Z_GATE // d + 2)),
            pl.BlockSpec((1, B_DV), lambda i: (0, 0)),
            const(aw), const(bw), const(cw), const(d),
        ],
        out_specs=pl.BlockSpec((tm, d), lambda i: (i, 0)),
        compiler_params=_cparams(("parallel",)),
        name="merge",
    )(x, mod, a, o_f, o_b, z, c, z, z, z, b_norm_g.reshape(1, B_DV), wa, wb, wc, wo)


def _rope_tables(seq, pad_rows):
    half = HEAD_DIM // 4
    t = np.arange(seq)
    pos = np.stack([t // GRID_W, t % GRID_W], axis=1).astype(np.float32)
    inv_freq = jnp.asarray(ROPE_BASE, F32) ** (-jnp.arange(half, dtype=F32) / half)
    ang = jnp.asarray(pos)[:, :, None] * inv_freq[None, None, :]
    cos = jnp.cos(ang)
    sin = jnp.sin(ang)
    cos_t = jnp.concatenate([cos, cos], axis=-1).reshape(seq, HEAD_DIM)
    sin_t = jnp.concatenate([-sin, sin], axis=-1).reshape(seq, HEAD_DIM)
    cos_t = jnp.concatenate([cos_t, jnp.ones((pad_rows, HEAD_DIM), F32)], axis=0)
    sin_t = jnp.concatenate([sin_t, jnp.zeros((pad_rows, HEAD_DIM), F32)], axis=0)
    return cos_t, sin_t


def _reorder_w_in(w, d):
    bkw, bvw, ckvw, aw, cqw = 256, 512, 256, 512, 1024
    o_bk = 0
    o_bv = o_bk + bkw
    o_ck = o_bv + bvw
    o_cv = o_ck + ckvw
    o_au = o_cv + ckvw
    o_av = o_au + aw
    o_bq = o_av + aw
    o_bg = o_bq + bkw
    o_cq = o_bg + bvw
    o_gate = o_cq + cqw
    seg = lambda s, width: w[..., s:s + width].astype(BF16)
    return jnp.concatenate([seg(o_cq, cqw), seg(o_ck, ckvw), seg(o_cv, ckvw), seg(o_bk, bkw), seg(o_bq, bkw), seg(o_bv, bvw),
                            seg(o_au, aw), seg(o_av, aw), seg(o_bg, bvw), seg(o_gate, 3 * d)], axis=-1)


def kernel(x, c, ctx, c_ctx, w_ada, b_ada, norm_g, w_ffn_up, w_ffn_down, w_in, a_v_gain, a_ws, a_bs,
           b_decay_w1, b_decay_w2, b_decay_b, b_norm_g, c_q_gain, c_k_gain, c_sink,
           w_br_a, w_br_b, w_br_c, w_out):
    batch, seq, d = x.shape
    ctx_len = ctx.shape[1]
    depth = w_ada.shape[0]
    assert d == 2048 and batch < MOD_ROWS and seq % 256 == 0 and ctx_len == 256
    n_lat, n_ctx = batch * seq, batch * ctx_len
    n_all = n_lat + n_ctx
    tm_ffn, tm_in, tm_merge = 1024, 512, 256
    for t in (tm_ffn, tm_in, tm_merge):
        assert seq % t == 0 and n_ctx % t == 0

    def row_fn(t):
        return lambda i: jnp.minimum(i * t // seq, batch)

    c8 = jnp.zeros((MOD_ROWS, d), F32).at[:batch].set(c).at[batch].set(c_ctx)
    mod = _with_norm_gains(_mod_table(c8, w_ada, b_ada), norm_g)
    cos_t, sin_t = _rope_tables(seq, tm_in)

    w_up = w_ffn_up.astype(BF16)
    w_dn = w_ffn_down.astype(BF16)
    w_in_r = _reorder_w_in(w_in, d)
    w1 = jnp.concatenate([b_decay_w1[:, 0], b_decay_w1[:, 1],
                          jnp.zeros((depth, d, LANES - 2 * B_RANK), F32)], axis=-1).astype(BF16)
    kw = b_decay_w2.shape[-1]
    w2pad = jnp.zeros((depth, 2, LANES, kw), F32).at[:, 0, :B_RANK].set(b_decay_w2[:, 0]).at[
        :, 1, B_RANK:2 * B_RANK].set(b_decay_w2[:, 1])
    n_qh = c_sink.shape[1]
    qk_gain = jnp.concatenate([jnp.tile(c_q_gain * (HEAD_DIM ** -0.5 * LOG2E), (1, n_qh)),
                               jnp.tile(c_k_gain, (1, C_KVW // HEAD_DIM))], axis=1)
    wa, wb, wc, wo = (w.astype(BF16) for w in (w_br_a, w_br_b, w_br_c, w_out))

    srcs = [x.reshape(n_lat, d), ctx.reshape(n_ctx, d)]
    for l in range(depth):
        last = l == depth - 1
        n_mix = n_lat if last else n_all
        tm_first = tm_ffn if len(srcs) == 1 else tm_ffn // 2
        xs = _ffn(srcs, n_all, mod, l, 0, row_fn(tm_first), w_up, w_dn, tm_first, TF)
        z, r = _inproj(xs, mod, l, row_fn(tm_in), w_in_r, w1, qk_gain[l:l + 1], cos_t, sin_t,
                       n_lat, seq, tm_in, 2 * QK_COLS, ctx_state_only=last)
        a_out = _gmlp(z, n_mix, a_v_gain[l], a_ws[l], a_bs[l], 512)
        o_f, o_b = _gla(z, r, w2pad[l], b_decay_b[l].reshape(2, 1, kw), batch, seq, ctx_len, 256)
        c_out = _attn(z, c_sink[l], batch, seq, ctx_len, with_ctx=not last)
        xs = _merge(xs, n_mix, mod, l, row_fn(tm_merge), z, a_out, o_f, o_b, c_out, b_norm_g[l],
                    wa, wb, wc, wo, tm_merge)
        xs = _ffn([xs], n_mix, mod, l, 1, row_fn(tm_ffn), w_up, w_dn, tm_ffn, TF)
        srcs = [xs]
    return xs.reshape(batch, seq, d)
```

```python
import functools

import jax
import jax.numpy as jnp
import numpy as np
from jax import lax
from jax.experimental import pallas as pl
from jax.experimental.pallas import tpu as pltpu

F32 = jnp.float32
BF16 = jnp.bfloat16

HEAD_DIM = 128
EPS = 1e-6
N_MOD = 9
GRID_W = 64
A_CHUNK = 128
B_DK = 64
B_DV = 128
B_RANK = 16
B_TAU = 16.0
B_CHUNK = 64
C_BLOCK = 128
ROPE_BASE = 10000.0
MOD_ROWS = 8
LANES = 128
NEG_BIG = -1e30
LOG2E = 1.4426950408889634
LN2 = 0.6931471805599453
VMEM_LIMIT = 56 * 1024 * 1024

Z_CQ, Z_CK, Z_CV, Z_BK, Z_BQ, Z_BV, Z_AU, Z_AV, Z_BG, Z_GATE = (
    0, 1024, 1280, 1536, 1792, 2048, 2560, 3072, 3584, 4096)
QK_COLS = Z_CV
C_KVW = Z_CV - Z_CK

TM_FFN = 1024
TF = 512
TM_IN = 512
TN_IN = 2 * QK_COLS
TM_MERGE = 256
TG_GMLP = 512
TB_GLA = 256


def _cparams(sem):
    return pltpu.CompilerParams(dimension_semantics=sem, vmem_limit_bytes=VMEM_LIMIT)


def _dot(a, b):
    return jnp.dot(a, b, preferred_element_type=F32)


def _dot_nt(a, b):
    return lax.dot_general(a, b, (((1,), (1,)), ((), ())), preferred_element_type=F32)


def _dot_tn(a, b):
    return lax.dot_general(a, b, (((0,), (0,)), ((), ())), preferred_element_type=F32)


def _silu(x):
    return x * jax.nn.sigmoid(x)


NORM_ROWS = 64


def _rms_mod_to(x_ref, m_ref, h_ref):
    shift = m_ref[0]
    gain = m_ref[3] * (1.0 + m_ref[1])

    def one_pass(c, carry):
        rows = pl.ds(pl.multiple_of(c * NORM_ROWS, NORM_ROWS), NORM_ROWS)
        x = x_ref[rows, :]
        rs = lax.rsqrt(jnp.mean(x * x, axis=-1, keepdims=True) + EPS)
        h_ref[rows, :] = (x * rs * gain + shift).astype(BF16)
        return carry

    lax.fori_loop(0, x_ref.shape[0] // NORM_ROWS, one_pass, 0, unroll=2)


def _rms_head(xh, gain):
    ms = jnp.mean(xh * xh, axis=-1, keepdims=True)
    return xh * lax.rsqrt(ms + EPS) * gain


def _mod_kernel(c_ref, w_ref, b_ref, o_ref):
    act = _silu(c_ref[...])
    o_ref[...] = _dot(act.astype(BF16), w_ref[...].astype(BF16)) + b_ref[...]


def _mod_table(c8, w_ada, b_ada):
    depth, d, nd = w_ada.shape
    tn = 1024
    nj = d // tn
    out = pl.pallas_call(
        _mod_kernel,
        out_shape=jax.ShapeDtypeStruct((depth, N_MOD, MOD_ROWS, d), F32),
        grid=(depth, N_MOD, nj),
        in_specs=[
            pl.BlockSpec((MOD_ROWS, d), lambda l, k, j: (0, 0)),
            pl.BlockSpec((None, d, tn), lambda l, k, j: (l, 0, k * nj + j)),
            pl.BlockSpec((None, 1, tn), lambda l, k, j: (l, 0, k * nj + j)),
        ],
        out_specs=pl.BlockSpec((None, None, MOD_ROWS, tn), lambda l, k, j: (l, k, 0, j)),
        compiler_params=_cparams(("parallel", "parallel", "parallel")),
        name="mod_table",
    )(c8, w_ada, b_ada.reshape(depth, 1, nd))
    return out


MOD_GROUP = 4


def _with_norm_gains(mod, norm_g):
    depth, _, rows, d = mod.shape
    subs = norm_g.shape[1]
    gains = jnp.broadcast_to(norm_g[:, :, None, None, :], (depth, subs, 1, rows, d))
    table = jnp.concatenate([mod.reshape(depth, subs, N_MOD // subs, rows, d), gains], axis=2)
    return table.reshape(depth, subs * MOD_GROUP, rows, 1, d)


def _mod_spec(layer, sub, row_fn, d):
    return pl.BlockSpec((None, MOD_GROUP, None, 1, d), lambda i, *_: (layer, sub, row_fn(i), 0, 0))


def _ffn_kernel(*refs, tiles_per_source):
    n_src = len(tiles_per_source)
    x_refs = refs[:n_src]
    m_ref, wg_ref, wv_ref, wd_ref, o_ref, h_ref = refs[n_src:]
    acc_ref = o_ref
    i, f = pl.program_id(0), pl.program_id(1)

    def with_own_source(fn):
        start = 0
        for x_ref, n_tiles in zip(x_refs, tiles_per_source):
            if n_src == 1:
                fn(x_ref)
            else:
                pl.when(jnp.logical_and(i >= start, i < start + n_tiles))(lambda x_ref=x_ref: fn(x_ref))
            start += n_tiles

    last = pl.num_programs(1) - 1

    def prologue(x_ref):
        _rms_mod_to(x_ref, m_ref, h_ref)

    def chunk():
        h = h_ref[...]
        act = (_silu(_dot(h, wg_ref[...])) * _dot(h, wv_ref[...])).astype(BF16)
        return _dot(act, wd_ref[...])

    @pl.when(f == 0)
    def _():
        with_own_source(prologue)
        acc_ref[...] = chunk()

    if n_src == 1:
        @pl.when(jnp.logical_and(f > 0, f < last))
        def _():
            acc_ref[...] += chunk()

        @pl.when(f == last)
        def _():
            o_ref[...] = x_refs[0][...] + 0.5 * m_ref[2] * (acc_ref[...] + chunk())
    else:
        @pl.when(f > 0)
        def _():
            acc_ref[...] += chunk()

        def epilogue(x_ref):
            o_ref[...] = x_ref[...] + 0.5 * m_ref[2] * acc_ref[...]

        @pl.when(f == last)
        def _():
            with_own_source(epilogue)


def _ffn(xs, n_rows, mod, layer, which, row_fn, w_up, w_down, tm, tf):
    d = xs[0].shape[1]
    nf = w_down.shape[2] // tf
    nt = n_rows // tm
    tiles, starts = [], []
    for x in xs:
        starts.append(sum(tiles))
        tiles.append(min(x.shape[0] // tm, nt - sum(tiles)))

    def src_spec(start, n_tiles):
        mode = pl.Buffered(1) if (n_tiles == 1 and len(xs) > 1) else None
        return pl.BlockSpec((tm, d), lambda i, f: (jnp.clip(i - start, 0, n_tiles - 1), 0), pipeline_mode=mode)

    return pl.pallas_call(
        functools.partial(_ffn_kernel, tiles_per_source=tuple(tiles)),
        out_shape=jax.ShapeDtypeStruct((n_rows, d), F32),
        grid=(nt, nf),
        in_specs=[src_spec(s, t) for s, t in zip(starts, tiles)] + [
            _mod_spec(layer, 2 * which, row_fn, d),
            pl.BlockSpec((None, None, d, tf), lambda i, f: (layer, which, 0, f)),
            pl.BlockSpec((None, None, d, tf), lambda i, f: (layer, which, 0, nf + f)),
            pl.BlockSpec((None, None, tf, d), lambda i, f: (layer, which, f, 0)),
        ],
        out_specs=pl.BlockSpec((tm, d), lambda i, f: (i, 0)),
        scratch_shapes=[pltpu.VMEM((tm, d), BF16)],
        compiler_params=_cparams(("parallel", "arbitrary")),
        name="ffn",
    )(*xs, mod, w_up, w_up, w_down)


def _inproj_kernel(x_ref, m_ref, w_ref, w1_ref, qkg_ref, cos_ref, sin_ref, z_ref, r_ref, h_ref, *,
                   full_tiles):
    i, n = pl.program_id(0), pl.program_id(1)

    @pl.when(jnp.logical_and(n != 0, i < full_tiles))
    def _():
        z_ref[...] = _dot(h_ref[...], w_ref[...]).astype(z_ref.dtype)

    @pl.when(jnp.logical_and(n != 0, i >= full_tiles))
    def _():
        z_ref[...] = jnp.zeros_like(z_ref)

    @pl.when(n == 0)
    def _():
        _rms_mod_to(x_ref, m_ref, h_ref)
        h = h_ref[...]
        r_ref[...] = _dot(h, w1_ref[...])
        zt = _dot(h, w_ref[:, :QK_COLS])
        if w_ref.shape[1] > QK_COLS:
            z_ref[:, QK_COLS:] = _dot(h, w_ref[:, QK_COLS:]).astype(z_ref.dtype)
        pair = 2 * HEAD_DIM
        r = lax.broadcasted_iota(jnp.int32, (pair, pair), 0)
        c = lax.broadcasted_iota(jnp.int32, (pair, pair), 1)
        head_sum = jnp.where(jnp.bitwise_and(jnp.bitwise_xor(r, c), HEAD_DIM) == 0, 1.0, 0.0).astype(BF16)
        swap = jnp.where(c == jnp.bitwise_xor(r, HEAD_DIM // 4), 1.0, 0.0).astype(BF16)
        cos = jnp.concatenate([cos_ref[...]] * 2, axis=1)
        sin = jnp.concatenate([sin_ref[...]] * 2, axis=1)
        for p in range(QK_COLS // pair):
            cols = slice(p * pair, (p + 1) * pair)
            y = zt[:, cols]
            ms = _dot((y * y).astype(BF16), head_sum) * (1.0 / HEAD_DIM)
            y = y * lax.rsqrt(ms + EPS) * qkg_ref[:, cols]
            z_ref[:, cols] = (y * cos + _dot(y.astype(BF16), swap) * sin).astype(z_ref.dtype)


def _inproj(x, mod, layer, row_fn, w_in, w1, qk_gain, cos_t, sin_t, n_lat, seq, tm, tn, ctx_state_only):
    n_rows, d = x.shape
    ncols = w_in.shape[2]
    assert tn % QK_COLS == 0 and ncols % tn == 0 and Z_BV + 512 <= tn
    rope_blocks = seq // tm
    nt = n_rows // tm
    full_tiles = n_lat // tm if ctx_state_only else nt

    def rope_blk(i, n):
        return jnp.where(i * tm < n_lat, (i % rope_blocks), rope_blocks)

    def w_blk(i, n):
        return jnp.where(i < full_tiles, n, 0)

    return pl.pallas_call(
        functools.partial(_inproj_kernel, full_tiles=full_tiles),
        out_shape=(jax.ShapeDtypeStruct((n_rows, ncols), BF16),
                   jax.ShapeDtypeStruct((n_rows, LANES), F32)),
        grid=(n_rows // tm, ncols // tn),
        in_specs=[
            pl.BlockSpec((tm, d), lambda i, n: (i, 0)),
            _mod_spec(layer, 1, row_fn, d),
            pl.BlockSpec((None, d, tn), lambda i, n: (layer, 0, w_blk(i, n))),
            pl.BlockSpec((None, d, LANES), lambda i, n: (layer, 0, 0)),
            pl.BlockSpec((1, QK_COLS), lambda i, n: (0, 0)),
            pl.BlockSpec((tm, HEAD_DIM), lambda i, n: (rope_blk(i, n), 0)),
            pl.BlockSpec((tm, HEAD_DIM), lambda i, n: (rope_blk(i, n), 0)),
        ],
        out_specs=(pl.BlockSpec((tm, tn), lambda i, n: (i, n)),
                   pl.BlockSpec((tm, LANES), lambda i, n: (i, 0))),
        scratch_shapes=[pltpu.VMEM((tm, d), BF16)],
        compiler_params=_cparams(("parallel", "arbitrary")),
        name="inproj",
    )(x, mod, w_in, w1, qk_gain, cos_t, sin_t)


def _gmlp_kernel(u_ref, v_ref, gain_ref, ws_ref, bs_ref, o_ref):
    tg, width = u_ref.shape
    groups = width // HEAD_DIM
    u = jax.nn.gelu(u_ref[...].astype(F32))
    v = jax.nn.gelu(v_ref[...].astype(F32))
    mu = jnp.mean(v, axis=-1, keepdims=True)
    vc = v - mu
    var = jnp.mean(vc * vc, axis=-1, keepdims=True)
    vn = (vc * lax.rsqrt(var + EPS) * gain_ref[...]).astype(BF16)
    nchunk = tg // A_CHUNK
    for g in range(groups):
        cols = slice(g * HEAD_DIM, (g + 1) * HEAD_DIM)
        v_g = jnp.concatenate([vn[c * A_CHUNK:(c + 1) * A_CHUNK, cols] for c in range(nchunk)], axis=1)
        mixed = _dot(ws_ref[g], v_g)
        for c in range(nchunk):
            rows = slice(c * A_CHUNK, (c + 1) * A_CHUNK)
            o_ref[rows, cols] = (u[rows, cols] * (mixed[:, c * HEAD_DIM:(c + 1) * HEAD_DIM] + bs_ref[g])
                                 ).astype(o_ref.dtype)


def _gmlp(z, n_rows, a_v_gain, a_ws, a_bs, tg):
    groups = a_ws.shape[0]
    width = groups * HEAD_DIM
    bs_full = jnp.broadcast_to(a_bs[:, :, None], (groups, A_CHUNK, HEAD_DIM)).astype(F32)
    return pl.pallas_call(
        _gmlp_kernel,
        out_shape=jax.ShapeDtypeStruct((n_rows, width), BF16),
        grid=(n_rows // tg,),
        in_specs=[
            pl.BlockSpec((tg, width), lambda i: (i, Z_AU // width)),
            pl.BlockSpec((tg, width), lambda i: (i, Z_AV // width)),
            pl.BlockSpec((1, width), lambda i: (0, 0)),
            pl.BlockSpec((groups, A_CHUNK, A_CHUNK), lambda i: (0, 0, 0)),
            pl.BlockSpec((groups, A_CHUNK, HEAD_DIM), lambda i: (0, 0, 0)),
        ],
        out_specs=pl.BlockSpec((tg, width), lambda i: (i, 0)),
        compiler_params=_cparams(("parallel",)),
        name="gmlp",
    )(z, z, a_v_gain.reshape(1, width), a_ws.astype(BF16), bs_full)


def _gla_prep(k_ref, q_ref, r_ref, w2, bias, tri_ref, reverse):
    tb = k_ref.shape[0]
    heads = k_ref.shape[1] // B_DK
    nchunk = tb // B_CHUNK
    logit = _dot(r_ref[...].astype(BF16), w2) + bias
    soft = jnp.log2(1.0 + jnp.exp2(jnp.abs(logit) * -LOG2E))
    g = jnp.minimum(logit, 0.0) * (1.0 / B_TAU) - soft * (LN2 / B_TAU)
    g_hi = g.astype(BF16)
    g_lo = (g - g_hi.astype(F32)).astype(BF16)
    tri = tri_ref[...]
    bc = _dot(tri, g_hi) + _dot(tri, g_lo)
    last = [c * B_CHUNK + (0 if reverse else B_CHUNK - 1) for c in range(nchunk)]
    tots = [bc[r:r + 1, :] for r in last]
    tot_rows = jnp.concatenate([jnp.broadcast_to(t, (B_CHUNK, t.shape[1])) for t in tots], axis=0)
    kf = k_ref[...].astype(F32)
    q_in = (q_ref[...].astype(F32) * jnp.exp(bc) * (B_DK ** -0.5)).astype(BF16)
    k_in = kf * jnp.exp(-bc)
    k_out = kf * jnp.exp(tot_rows - bc)
    dec = [jnp.exp(t) for t in tots]
    lane = lax.broadcasted_iota(jnp.int32, (tb, LANES), 1)
    zeros = jnp.zeros((B_CHUNK, LANES), BF16)
    q_slabs, k_in_heads, k4_heads = [], [], []
    for h in range(heads):
        slab = slice((h // 2) * LANES, (h // 2 + 1) * LANES)
        mine = (lane >= B_DK) if h % 2 else (lane < B_DK)
        q_slabs.append(q_in[:, slab])
        k_in_heads.append(jnp.where(mine, k_in[:, slab], 0.0).astype(BF16))
        k_out_h = jnp.where(mine, k_out[:, slab], 0.0).astype(BF16)
        k4_heads.append(jnp.concatenate([
            jnp.concatenate([zeros] * c + [k_out_h[c * B_CHUNK:(c + 1) * B_CHUNK]] + [zeros] * (nchunk - 1 - c),
                            axis=0) for c in range(nchunk)], axis=1))
    return q_slabs, k_in_heads, k4_heads, dec


def _gla_kernel(kf_ref, vf_ref, qf_ref, rf_ref, kb_ref, vb_ref, qb_ref, rb_ref, w2_ref, b_ref,
                tri_ref, keep_ref, of_ref, ob_ref, sf_ref, sb_ref, a_ref, u_ref):
    @pl.when(pl.program_id(1) == 0)
    def _():
        sf_ref[...] = jnp.zeros_like(sf_ref)
        sb_ref[...] = jnp.zeros_like(sb_ref)

    heads = sf_ref.shape[0]
    nchunk = kf_ref.shape[0] // B_CHUNK
    v_refs, o_refs, st_refs = (vf_ref, vb_ref), (of_ref, ob_ref), (sf_ref, sb_ref)
    prep = [_gla_prep(kf_ref, qf_ref, rf_ref, w2_ref[0], b_ref[0], tri_ref.at[0], False),
            _gla_prep(kb_ref, qb_ref, rb_ref, w2_ref[1], b_ref[1], tri_ref.at[1], True)]
    jobs = [(d, h) for d in range(2) for h in range(heads)]
    for d, h in jobs:
        q_slabs, k_in_heads, _, _ = prep[d]
        a_ref[d * heads + h] = _dot_nt(q_slabs[h], k_in_heads[h])
    for d, h in jobs:
        v_h = v_refs[d][:, h * B_DV:(h + 1) * B_DV]
        u_ref[d * heads + h] = _dot_tn(v_h, prep[d][2][h])
    for d, h in jobs:
        vcols = slice(h * B_DV, (h + 1) * B_DV)
        attn = jnp.where(keep_ref[d] > 0.5, a_ref[d * heads + h], 0.0).astype(BF16)
        o_refs[d][:, vcols] = _dot(attn, v_refs[d][:, vcols])
    for d, h in jobs:
        q_s, dec = prep[d][0][h], prep[d][3]
        slab = slice((h // 2) * LANES, (h // 2 + 1) * LANES)
        vcols = slice(h * B_DV, (h + 1) * B_DV)
        st = st_refs[d][h]
        for c in (range(nchunk - 1, -1, -1) if d else range(nchunk)):
            rows = slice(c * B_CHUNK, (c + 1) * B_CHUNK)
            o_refs[d][rows, vcols] += _dot_nt(q_s[rows], st.astype(BF16))
            st = st * dec[c][:, slab] + u_ref[d * heads + h, :, c * LANES:(c + 1) * LANES]
        st_refs[d][h] = st


def _chunk_triangles(tb):
    t = np.arange(tb)[:, None]
    s = np.arange(tb)[None, :]
    same = (t // B_CHUNK) == (s // B_CHUNK)
    return np.stack([same & (s <= t), same & (s >= t)]).astype(np.float32)


def _gla(z, r, w2pad, bias, batch, seq, ctx_len, tb):
    n_rows = z.shape[0]
    heads = bias.shape[-1] // B_DK
    assert heads % 2 == 0 and 2 * B_DK == LANES
    kw, vw = heads * B_DK, heads * B_DV
    nlat, nctx = seq // tb, ctx_len // tb
    base = batch * nlat
    tri = _chunk_triangles(tb)

    def fwd(b, j):
        return jnp.where(j < nctx, base + b * nctx + j, b * nlat + (j - nctx))

    def bwd(b, j):
        return jnp.where(j < nctx, base + b * nctx + (nctx - 1 - j), b * nlat + (nlat - 1 - (j - nctx)))

    def specs(blk):
        return [pl.BlockSpec((tb, kw), lambda b, j: (blk(b, j), Z_BK // kw)),
                pl.BlockSpec((tb, vw), lambda b, j: (blk(b, j), Z_BV // vw)),
                pl.BlockSpec((tb, kw), lambda b, j: (blk(b, j), Z_BQ // kw)),
                pl.BlockSpec((tb, LANES), lambda b, j: (blk(b, j), 0))]

    return pl.pallas_call(
        _gla_kernel,
        out_shape=(jax.ShapeDtypeStruct((n_rows, vw), F32), jax.ShapeDtypeStruct((n_rows, vw), F32)),
        grid=(batch, nctx + nlat),
        in_specs=specs(fwd) + specs(bwd) + [
            pl.BlockSpec((2, LANES, kw), lambda b, j: (0, 0, 0)),
            pl.BlockSpec((2, 1, kw), lambda b, j: (0, 0, 0)),
            pl.BlockSpec((2, tb, tb), lambda b, j: (0, 0, 0)),
            pl.BlockSpec((2, tb, tb), lambda b, j: (0, 0, 0)),
        ],
        out_specs=(pl.BlockSpec((tb, vw), lambda b, j: (fwd(b, j), 0)),
                   pl.BlockSpec((tb, vw), lambda b, j: (bwd(b, j), 0))),
        scratch_shapes=[pltpu.VMEM((heads, B_DV, LANES), F32), pltpu.VMEM((heads, B_DV, LANES), F32),
                        pltpu.VMEM((2 * heads, tb, tb), F32),
                        pltpu.VMEM((2 * heads, B_DV, (tb // B_CHUNK) * LANES), F32)],
        compiler_params=_cparams(("parallel", "arbitrary")),
        name="gla",
    )(z, z, z, r, z, z, z, r, w2pad.astype(BF16), bias, jnp.asarray(tri, BF16), jnp.asarray(tri, F32))


def _attn_kernel(sink_ref, q_ref, kp_ref, kc_ref, kn_ref, kx_ref, vp_ref, vc_ref, vn_ref, vx_ref,
                 bias_ref, o_ref, s_ref):
    tq = q_ref.shape[0]
    n_heads = q_ref.shape[1] // HEAD_DIM
    kv_heads = kc_ref.shape[1] // HEAD_DIM
    group = n_heads // kv_heads
    n_loc = bias_ref.shape[1]

    def kv_cols(h):
        kh = h // group
        return slice(kh * HEAD_DIM, (kh + 1) * HEAD_DIM)

    def scores(kh):
        cols = kv_cols(kh * group)
        k_all = jnp.concatenate([kp_ref[:, cols], kc_ref[:, cols], kn_ref[:, cols], kx_ref[:, cols]], axis=0)
        q = jnp.concatenate([q_ref[:, h * HEAD_DIM:(h + 1) * HEAD_DIM]
                             for h in range(kh * group, (kh + 1) * group)], axis=0)
        s_ref[kh * group * tq:(kh + 1) * group * tq, :] = _dot_nt(q, k_all)

    def softmax_pv(h):
        cols = kv_cols(h)
        rows = slice(h * tq, (h + 1) * tq)
        v_all = jnp.concatenate([vp_ref[:, cols], vc_ref[:, cols], vn_ref[:, cols], vx_ref[:, cols]], axis=0)
        sink = sink_ref[h] * LOG2E
        sg = jnp.concatenate([s_ref[rows, :n_loc] + bias_ref[...], s_ref[rows, n_loc:]], axis=1)
        m = jnp.maximum(jnp.max(sg, axis=-1, keepdims=True), sink)
        e = jnp.exp2(sg - m)
        den = jnp.sum(e, axis=-1, keepdims=True) + jnp.exp2(sink - m)
        o_ref[:, h * HEAD_DIM:(h + 1) * HEAD_DIM] = (_dot(e.astype(BF16), v_all) / den).astype(o_ref.dtype)

    for kh in range(kv_heads):
        scores(kh)
    for h in range(n_heads):
        softmax_pv(h)


def _window_bias(tq):
    i = np.arange(tq)[:, None]
    j = np.arange(3 * tq)[None, :]
    band = (j >= i) & (j <= i + 2 * tq)
    cases = [band & (j >= tq), band, band & (j < 2 * tq), np.zeros_like(band)]
    return jnp.asarray(np.where(np.stack(cases), 0.0, NEG_BIG), F32)


def _attn(z, sink, batch, seq, ctx_len, with_ctx):
    n_heads = sink.shape[0]
    qw = n_heads * HEAD_DIM
    kvw = C_KVW
    tq = C_BLOCK
    nb = seq // tq
    cpb = ctx_len // tq if with_ctx else 0
    assert nb >= 2 and ctx_len % tq == 0
    ctx_base = batch * seq // ctx_len

    def own(b, n):
        return jnp.where(n < nb, b * nb + n, batch * nb + b * cpb + (n - nb))

    def prev(b, n):
        return b * nb + jnp.clip(n - 1, 0, nb - 1)

    def cur(b, n):
        return b * nb + jnp.minimum(n, nb - 1)

    def nxt(b, n):
        return b * nb + jnp.minimum(n + 1, nb - 1)

    def zspec(blk, col0):
        return pl.BlockSpec((tq, kvw), lambda b, n: (blk(b, n), col0 // kvw))

    def bias_case(b, n):
        return jnp.where(n >= nb, 3, jnp.where(n == 0, 0, jnp.where(n == nb - 1, 2, 1)))

    return pl.pallas_call(
        _attn_kernel,
        out_shape=jax.ShapeDtypeStruct((batch * (nb + cpb) * tq, qw), BF16),
        grid=(batch, nb + cpb),
        in_specs=[
            pl.BlockSpec(memory_space=pltpu.SMEM),
            pl.BlockSpec((tq, qw), lambda b, n: (own(b, n), Z_CQ // qw)),
            zspec(prev, Z_CK), zspec(cur, Z_CK), zspec(nxt, Z_CK),
            pl.BlockSpec((ctx_len, kvw), lambda b, n: (ctx_base + b, Z_CK // kvw)),
            zspec(prev, Z_CV), zspec(cur, Z_CV), zspec(nxt, Z_CV),
            pl.BlockSpec((ctx_len, kvw), lambda b, n: (ctx_base + b, Z_CV // kvw)),
            pl.BlockSpec((None, tq, 3 * tq), lambda b, n: (bias_case(b, n), 0, 0)),
        ],
        out_specs=pl.BlockSpec((tq, qw), lambda b, n: (own(b, n), 0)),
        scratch_shapes=[pltpu.VMEM((n_heads * tq, 3 * tq + ctx_len), F32)],
        compiler_params=_cparams(("parallel", "parallel")),
        name="window_attn",
    )(sink, z, z, z, z, z, z, z, z, z, _window_bias(tq))


MERGE_CHUNKS = 4


def _merge_kernel(x_ref, m_ref, a_ref, of_ref, ob_ref, og_ref, c_ref, ga_ref, gb_ref, gc_ref,
                  bg_ref, wa_ref, wb_ref, wc_ref, wo_ref, o_ref):
    heads = of_ref.shape[1] // B_DV
    o = of_ref[...] + ob_ref[...]
    bn = jnp.concatenate([_rms_head(o[:, h * B_DV:(h + 1) * B_DV], bg_ref[...]) for h in range(heads)], axis=1)
    b = (bn * _silu(og_ref[...].astype(F32))).astype(BF16)
    a, c = a_ref[...], c_ref[...]
    d = o_ref.shape[1]
    cw = d // MERGE_CHUNKS

    def merged_cols(k):
        cols = slice(k * cw, (k + 1) * cw)
        return (jax.nn.sigmoid(ga_ref[:, cols].astype(F32)) * _dot(a, wa_ref[:, cols])
                + jax.nn.sigmoid(gb_ref[:, cols].astype(F32)) * _dot(b, wb_ref[:, cols])
                + jax.nn.sigmoid(gc_ref[:, cols].astype(F32)) * _dot(c, wc_ref[:, cols])).astype(BF16)

    pending = merged_cols(0)
    mix = None
    for k in range(MERGE_CHUNKS):
        ready = pending
        if k + 1 < MERGE_CHUNKS:
            pending = merged_cols(k + 1)
        part = _dot(ready, wo_ref[k * cw:(k + 1) * cw, :])
        mix = part if mix is None else mix + part
    o_ref[...] = x_ref[...] + m_ref[2] * mix


def _merge(x, n_rows, mod, layer, row_fn, z, a, o_f, o_b, c, b_norm_g, wa, wb, wc, wo, tm):
    d = x.shape[1]
    aw, bw, cw = a.shape[1], o_f.shape[1], c.shape[1]

    def const(rows):
        return pl.BlockSpec((None, rows, d), lambda i: (layer, 0, 0), pipeline_mode=pl.Buffered(1))

    return pl.pallas_call(
        _merge_kernel,
        out_shape=jax.ShapeDtypeStruct((n_rows, d), F32),
        grid=(n_rows // tm,),
        in_specs=[
            pl.BlockSpec((tm, d), lambda i: (i, 0)),
            _mod_spec(layer, 1, row_fn, d),
            pl.BlockSpec((tm, aw), lambda i: (i, 0)),
            pl.BlockSpec((tm, bw), lambda i: (i, 0)),
            pl.BlockSpec((tm, bw), lambda i: (i, 0)),
            pl.BlockSpec((tm, bw), lambda i: (i, Z_BG // bw)),
            pl.BlockSpec((tm, cw), lambda i: (i, 0)),
            pl.BlockSpec((tm, d), lambda i: (i, Z_GATE // d)),
            pl.BlockSpec((tm, d), lambda i: (i, Z_GATE // d + 1)),
            pl.BlockSpec((tm, d), lambda i: (i, Z_GATE // d + 2)),
            pl.BlockSpec((1, B_DV), lambda i: (0, 0)),
            const(aw), const(bw), const(cw), const(d),
        ],
        out_specs=pl.BlockSpec((tm, d), lambda i: (i, 0)),
        compiler_params=_cparams(("parallel",)),
        name="merge",
    )(x, mod, a, o_f, o_b, z, c, z, z, z, b_norm_g.reshape(1, B_DV), wa, wb, wc, wo)


def _rope_tables(seq, pad_rows):
    half = HEAD_DIM // 4
    t = np.arange(seq)
    pos = np.stack([t // GRID_W, t % GRID_W], axis=1).astype(np.float32)
    inv_freq = jnp.asarray(ROPE_BASE, F32) ** (-jnp.arange(half, dtype=F32) / half)
    ang = jnp.asarray(pos)[:, :, None] * inv_freq[None, None, :]
    cos = jnp.cos(ang)
    sin = jnp.sin(ang)
    cos_t = jnp.concatenate([cos, cos], axis=-1).reshape(seq, HEAD_DIM)
    sin_t = jnp.concatenate([-sin, sin], axis=-1).reshape(seq, HEAD_DIM)
    cos_t = jnp.concatenate([cos_t, jnp.ones((pad_rows, HEAD_DIM), F32)], axis=0)
    sin_t = jnp.concatenate([sin_t, jnp.zeros((pad_rows, HEAD_DIM), F32)], axis=0)
    return cos_t, sin_t


def _reorder_w_in(w, d):
    bkw, bvw, ckvw, aw, cqw = 256, 512, 256, 512, 1024
    o_bk = 0
    o_bv = o_bk + bkw
    o_ck = o_bv + bvw
    o_cv = o_ck + ckvw
    o_au = o_cv + ckvw
    o_av = o_au + aw
    o_bq = o_av + aw
    o_bg = o_bq + bkw
    o_cq = o_bg + bvw
    o_gate = o_cq + cqw
    seg = lambda s, width: w[..., s:s + width].astype(BF16)
    return jnp.concatenate([seg(o_cq, cqw), seg(o_ck, ckvw), seg(o_cv, ckvw), seg(o_bk, bkw), seg(o_bq, bkw), seg(o_bv, bvw),
                            seg(o_au, aw), seg(o_av, aw), seg(o_bg, bvw), seg(o_gate, 3 * d)], axis=-1)


def kernel(x, c, ctx, c_ctx, w_ada, b_ada, norm_g, w_ffn_up, w_ffn_down, w_in, a_v_gain, a_ws, a_bs,
           b_decay_w1, b_decay_w2, b_decay_b, b_norm_g, c_q_gain, c_k_gain, c_sink,
           w_br_a, w_br_b, w_br_c, w_out):
    batch, seq, d = x.shape
    ctx_len = ctx.shape[1]
    depth = w_ada.shape[0]
    assert d == 2048 and batch < MOD_ROWS and seq % 256 == 0 and ctx_len == 256
    n_lat, n_ctx = batch * seq, batch * ctx_len
    n_all = n_lat + n_ctx
    for t in (TM_FFN, TM_IN, TM_MERGE, TG_GMLP, TB_GLA):
        assert seq % t == 0 and n_ctx % t == 0

    def row_fn(t):
        return lambda i: jnp.minimum(i * t // seq, batch)

    c8 = jnp.zeros((MOD_ROWS, d), F32).at[:batch].set(c).at[batch].set(c_ctx)
    mod = _with_norm_gains(_mod_table(c8, w_ada, b_ada), norm_g)
    cos_t, sin_t = _rope_tables(seq, TM_IN)

    w_up = w_ffn_up.astype(BF16)
    w_dn = w_ffn_down.astype(BF16)
    w_in_r = _reorder_w_in(w_in, d)
    w1 = jnp.concatenate([b_decay_w1[:, 0], b_decay_w1[:, 1],
                          jnp.zeros((depth, d, LANES - 2 * B_RANK), F32)], axis=-1).astype(BF16)
    kw = b_decay_w2.shape[-1]
    w2pad = jnp.zeros((depth, 2, LANES, kw), F32).at[:, 0, :B_RANK].set(b_decay_w2[:, 0]).at[
        :, 1, B_RANK:2 * B_RANK].set(b_decay_w2[:, 1])
    n_qh = c_sink.shape[1]
    qk_gain = jnp.concatenate([jnp.tile(c_q_gain * (HEAD_DIM ** -0.5 * LOG2E), (1, n_qh)),
                               jnp.tile(c_k_gain, (1, C_KVW // HEAD_DIM))], axis=1)
    wa, wb, wc, wo = (w.astype(BF16) for w in (w_br_a, w_br_b, w_br_c, w_out))

    srcs = [x.reshape(n_lat, d), ctx.reshape(n_ctx, d)]
    for l in range(depth):
        last = l == depth - 1
        n_mix = n_lat if last else n_all
        tm_first = TM_FFN if len(srcs) == 1 else TM_FFN // 2
        xs = _ffn(srcs, n_all, mod, l, 0, row_fn(tm_first), w_up, w_dn, tm_first, TF)
        z, r = _inproj(xs, mod, l, row_fn(TM_IN), w_in_r, w1, qk_gain[l:l + 1], cos_t, sin_t,
                       n_lat, seq, TM_IN, TN_IN, ctx_state_only=last)
        a_out = _gmlp(z, n_mix, a_v_gain[l], a_ws[l], a_bs[l], TG_GMLP)
        o_f, o_b = _gla(z, r, w2pad[l], b_decay_b[l].reshape(2, 1, kw), batch, seq, ctx_len, TB_GLA)
        c_out = _attn(z, c_sink[l], batch, seq, ctx_len, with_ctx=not last)
        xs = _merge(xs, n_mix, mod, l, row_fn(TM_MERGE), z, a_out, o_f, o_b, c_out, b_norm_g[l],
                    wa, wb, wc, wo, TM_MERGE)
        xs = _ffn([xs], n_mix, mod, l, 1, row_fn(TM_FFN), w_up, w_dn, TM_FFN, TF)
        srcs = [xs]
    return xs.reshape(batch, seq, d)
```

```python
import functools

import jax
import jax.numpy as jnp
import numpy as np
from jax import lax
from jax.experimental import pallas as pl
from jax.experimental.pallas import tpu as pltpu

F32 = jnp.float32
BF16 = jnp.bfloat16

HEAD_DIM = 128
EPS = 1e-6
N_MOD = 9
GRID_W = 64
A_CHUNK = 128
B_DK = 64
B_DV = 128
B_RANK = 16
B_TAU = 16.0
B_CHUNK = 64
C_BLOCK = 128
ROPE_BASE = 10000.0
MOD_ROWS = 8
LANES = 128
NEG_BIG = -1e30
LOG2E = 1.4426950408889634
LN2 = 0.6931471805599453
VMEM_LIMIT = 56 * 1024 * 1024

Z_CQ, Z_CK, Z_CV, Z_BK, Z_BQ, Z_BV, Z_AU, Z_AV, Z_BG, Z_GATE = (
    0, 1024, 1280, 1536, 1792, 2048, 2560, 3072, 3584, 4096)
QK_COLS = Z_CV
C_KVW = Z_CV - Z_CK

TM_FFN = 1024
TF = 512
TM_IN = 512
TN_IN = 2 * QK_COLS
TM_MERGE = 256
TG_GMLP = 512
TB_GLA = 256


def _cparams(sem):
    return pltpu.CompilerParams(dimension_semantics=sem, vmem_limit_bytes=VMEM_LIMIT)


def _dot(a, b):
    return jnp.dot(a, b, preferred_element_type=F32)


def _dot_nt(a, b):
    return lax.dot_general(a, b, (((1,), (1,)), ((), ())), preferred_element_type=F32)


def _dot_tn(a, b):
    return lax.dot_general(a, b, (((0,), (0,)), ((), ())), preferred_element_type=F32)


def _silu(x):
    return x * jax.nn.sigmoid(x)


NORM_ROWS = 64


def _rms_mod_to(x_ref, m_ref, h_ref):
    shift = m_ref[0]
    gain = m_ref[3] * (1.0 + m_ref[1])

    def one_pass(c, carry):
        rows = pl.ds(pl.multiple_of(c * NORM_ROWS, NORM_ROWS), NORM_ROWS)
        x = x_ref[rows, :]
        rs = lax.rsqrt(jnp.mean(x * x, axis=-1, keepdims=True) + EPS)
        h_ref[rows, :] = (x * rs * gain + shift).astype(BF16)
        return carry

    lax.fori_loop(0, x_ref.shape[0] // NORM_ROWS, one_pass, 0, unroll=2)


def _rms_head(xh, gain):
    ms = jnp.mean(xh * xh, axis=-1, keepdims=True)
    return xh * lax.rsqrt(ms + EPS) * gain


def _mod_kernel(c_ref, w_ref, b_ref, o_ref):
    act = _silu(c_ref[...])
    o_ref[...] = _dot(act.astype(BF16), w_ref[...].astype(BF16)) + b_ref[...]


def _mod_table(c8, w_ada, b_ada):
    depth, d, nd = w_ada.shape
    tn = 1024
    nj = d // tn
    out = pl.pallas_call(
        _mod_kernel,
        out_shape=jax.ShapeDtypeStruct((depth, N_MOD, MOD_ROWS, d), F32),
        grid=(depth, N_MOD, nj),
        in_specs=[
            pl.BlockSpec((MOD_ROWS, d), lambda l, k, j: (0, 0)),
            pl.BlockSpec((None, d, tn), lambda l, k, j: (l, 0, k * nj + j)),
            pl.BlockSpec((None, 1, tn), lambda l, k, j: (l, 0, k * nj + j)),
        ],
        out_specs=pl.BlockSpec((None, None, MOD_ROWS, tn), lambda l, k, j: (l, k, 0, j)),
        compiler_params=_cparams(("parallel", "parallel", "parallel")),
        name="mod_table",
    )(c8, w_ada, b_ada.reshape(depth, 1, nd))
    return out


MOD_GROUP = 4


def _with_norm_gains(mod, norm_g):
    depth, _, rows, d = mod.shape
    subs = norm_g.shape[1]
    gains = jnp.broadcast_to(norm_g[:, :, None, None, :], (depth, subs, 1, rows, d))
    table = jnp.concatenate([mod.reshape(depth, subs, N_MOD // subs, rows, d), gains], axis=2)
    return table.reshape(depth, subs * MOD_GROUP, rows, 1, d)


def _mod_spec(layer, sub, row_fn, d):
    return pl.BlockSpec((None, MOD_GROUP, None, 1, d), lambda i, *_: (layer, sub, row_fn(i), 0, 0))


def _ffn_kernel(*refs, tiles_per_source, n_cast):
    n_src = len(tiles_per_source)
    x_refs = refs[:n_src]
    m_ref, wg_ref, wv_ref, wd_ref = refs[n_src:n_src + 4]
    cast_in_refs = refs[n_src + 4:n_src + 4 + n_cast]
    o_ref = refs[n_src + 4 + n_cast]
    cast_out_refs = refs[n_src + 5 + n_cast:n_src + 5 + 2 * n_cast]
    h_ref = refs[n_src + 5 + 2 * n_cast]
    acc_ref = o_ref
    i, f = pl.program_id(0), pl.program_id(1)

    for cast_in_ref, cast_out_ref in zip(cast_in_refs, cast_out_refs):
        cast_out_ref[...] = cast_in_ref[...].astype(BF16)

    def with_own_source(fn):
        start = 0
        for x_ref, n_tiles in zip(x_refs, tiles_per_source):
            if n_src == 1:
                fn(x_ref)
            else:
                pl.when(jnp.logical_and(i >= start, i < start + n_tiles))(lambda x_ref=x_ref: fn(x_ref))
            start += n_tiles

    last = pl.num_programs(1) - 1

    def prologue(x_ref):
        _rms_mod_to(x_ref, m_ref, h_ref)

    def chunk():
        h = h_ref[...]
        act = (_silu(_dot(h, wg_ref[...])) * _dot(h, wv_ref[...])).astype(BF16)
        return _dot(act, wd_ref[...])

    @pl.when(f == 0)
    def _():
        with_own_source(prologue)
        acc_ref[...] = chunk()

    if n_src == 1:
        @pl.when(jnp.logical_and(f > 0, f < last))
        def _():
            acc_ref[...] += chunk()

        @pl.when(f == last)
        def _():
            o_ref[...] = x_refs[0][...] + 0.5 * m_ref[2] * (acc_ref[...] + chunk())
    else:
        @pl.when(f > 0)
        def _():
            acc_ref[...] += chunk()

        def epilogue(x_ref):
            o_ref[...] = x_ref[...] + 0.5 * m_ref[2] * acc_ref[...]

        @pl.when(f == last)
        def _():
            with_own_source(epilogue)


def _ffn(xs, n_rows, mod, layer, which, row_fn, w_up, w_down, tm, tf, cast=()):
    d = xs[0].shape[1]
    nf = w_up.shape[1] // (2 * tf)
    nt = n_rows // tm
    tiles, starts = [], []
    for x in xs:
        starts.append(sum(tiles))
        tiles.append(min(x.shape[0] // tm, nt - sum(tiles)))

    def src_spec(start, n_tiles):
        mode = pl.Buffered(1) if (n_tiles == 1 and len(xs) > 1) else None
        return pl.BlockSpec((tm, d), lambda i, f: (jnp.clip(i - start, 0, n_tiles - 1), 0), pipeline_mode=mode)

    in_specs = [src_spec(s, t) for s, t in zip(starts, tiles)] + [
        _mod_spec(layer, 2 * which, row_fn, d),
        pl.BlockSpec((d, tf), lambda i, f: (which, f)),
        pl.BlockSpec((d, tf), lambda i, f: (which, nf + f)),
        pl.BlockSpec((tf, d), lambda i, f: (which * nf + f, 0)),
    ]
    out_shape = [jax.ShapeDtypeStruct((n_rows, d), F32)]
    out_specs = [pl.BlockSpec((tm, d), lambda i, f: (i, 0))]
    operands = [*xs, mod, w_up, w_up, w_down]
    row_blocks = 1 << (nt.bit_length() - 1)
    cast_in_specs = []
    for src, split, part, parts in cast:
        rows, cols = src.shape[0] // parts, src.shape[1]
        if split == "cols":
            br, bc = rows // row_blocks, cols // nf
            in_blk = lambda i, f, part=part: (part * row_blocks + jnp.minimum(i, row_blocks - 1), f)
            out_blk = lambda i, f: (i, f)
        else:
            br, bc = rows // (row_blocks * nf), cols
            in_blk = lambda i, f, part=part: ((part * row_blocks + jnp.minimum(i, row_blocks - 1)) * nf + f, 0)
            out_blk = lambda i, f: (i * nf + f, 0)
        assert br % 16 == 0 and bc % LANES == 0 and rows % br == 0 and cols % bc == 0
        cast_in_specs.append(pl.BlockSpec((br, bc), in_blk))
        out_shape.append(jax.ShapeDtypeStruct((rows // row_blocks * nt, cols), BF16))
        out_specs.append(pl.BlockSpec((br, bc), out_blk))
        operands.append(src)

    outs = pl.pallas_call(
        functools.partial(_ffn_kernel, tiles_per_source=tuple(tiles), n_cast=len(cast)),
        out_shape=out_shape,
        grid=(nt, nf),
        in_specs=in_specs + cast_in_specs,
        out_specs=out_specs,
        scratch_shapes=[pltpu.VMEM((tm, d), BF16)],
        compiler_params=_cparams(("parallel", "arbitrary")),
        name="ffn",
    )(*operands)
    return outs[0] if not cast else tuple(outs)


def _inproj_kernel(x_ref, m_ref, w_ref, w1_ref, qkg_ref, cos_ref, sin_ref, z_ref, r_ref, h_ref, *,
                   full_tiles):
    i, n = pl.program_id(0), pl.program_id(1)

    @pl.when(jnp.logical_and(n != 0, i < full_tiles))
    def _():
        z_ref[...] = _dot(h_ref[...], w_ref[...]).astype(z_ref.dtype)

    @pl.when(jnp.logical_and(n != 0, i >= full_tiles))
    def _():
        z_ref[...] = jnp.zeros_like(z_ref)

    @pl.when(n == 0)
    def _():
        _rms_mod_to(x_ref, m_ref, h_ref)
        h = h_ref[...]
        r_ref[...] = _dot(h, w1_ref[...])
        zt = _dot(h, w_ref[:, :QK_COLS])
        if w_ref.shape[1] > QK_COLS:
            z_ref[:, QK_COLS:] = _dot(h, w_ref[:, QK_COLS:]).astype(z_ref.dtype)
        pair = 2 * HEAD_DIM
        r = lax.broadcasted_iota(jnp.int32, (pair, pair), 0)
        c = lax.broadcasted_iota(jnp.int32, (pair, pair), 1)
        head_sum = jnp.where(jnp.bitwise_and(jnp.bitwise_xor(r, c), HEAD_DIM) == 0, 1.0, 0.0).astype(BF16)
        swap = jnp.where(c == jnp.bitwise_xor(r, HEAD_DIM // 4), 1.0, 0.0).astype(BF16)
        cos = jnp.concatenate([cos_ref[...]] * 2, axis=1)
        sin = jnp.concatenate([sin_ref[...]] * 2, axis=1)
        for p in range(QK_COLS // pair):
            cols = slice(p * pair, (p + 1) * pair)
            y = zt[:, cols]
            ms = _dot((y * y).astype(BF16), head_sum) * (1.0 / HEAD_DIM)
            y = y * lax.rsqrt(ms + EPS) * qkg_ref[:, cols]
            z_ref[:, cols] = (y * cos + _dot(y.astype(BF16), swap) * sin).astype(z_ref.dtype)


def _inproj(x, mod, layer, row_fn, w_in, w1, qk_gain, cos_t, sin_t, n_lat, seq, tm, tn, ctx_state_only):
    n_rows, d = x.shape
    ncols = w_in.shape[2]
    assert tn % QK_COLS == 0 and ncols % tn == 0 and Z_BV + 512 <= tn
    rope_blocks = seq // tm
    nt = n_rows // tm
    full_tiles = n_lat // tm if ctx_state_only else nt

    def rope_blk(i, n):
        return jnp.where(i * tm < n_lat, (i % rope_blocks), rope_blocks)

    def w_blk(i, n):
        return jnp.where(i < full_tiles, n, 0)

    return pl.pallas_call(
        functools.partial(_inproj_kernel, full_tiles=full_tiles),
        out_shape=(jax.ShapeDtypeStruct((n_rows, ncols), BF16),
                   jax.ShapeDtypeStruct((n_rows, LANES), F32)),
        grid=(n_rows // tm, ncols // tn),
        in_specs=[
            pl.BlockSpec((tm, d), lambda i, n: (i, 0)),
            _mod_spec(layer, 1, row_fn, d),
            pl.BlockSpec((None, d, tn), lambda i, n: (layer, 0, w_blk(i, n))),
            pl.BlockSpec((None, d, LANES), lambda i, n: (layer, 0, 0)),
            pl.BlockSpec((1, QK_COLS), lambda i, n: (0, 0)),
            pl.BlockSpec((tm, HEAD_DIM), lambda i, n: (rope_blk(i, n), 0)),
            pl.BlockSpec((tm, HEAD_DIM), lambda i, n: (rope_blk(i, n), 0)),
        ],
        out_specs=(pl.BlockSpec((tm, tn), lambda i, n: (i, n)),
                   pl.BlockSpec((tm, LANES), lambda i, n: (i, 0))),
        scratch_shapes=[pltpu.VMEM((tm, d), BF16)],
        compiler_params=_cparams(("parallel", "arbitrary")),
        name="inproj",
    )(x, mod, w_in, w1, qk_gain, cos_t, sin_t)


def _gmlp_kernel(u_ref, v_ref, gain_ref, ws_ref, bs_ref, o_ref):
    tg, width = u_ref.shape
    groups = width // HEAD_DIM
    u = jax.nn.gelu(u_ref[...].astype(F32))
    v = jax.nn.gelu(v_ref[...].astype(F32))
    mu = jnp.mean(v, axis=-1, keepdims=True)
    vc = v - mu
    var = jnp.mean(vc * vc, axis=-1, keepdims=True)
    vn = (vc * lax.rsqrt(var + EPS) * gain_ref[...]).astype(BF16)
    nchunk = tg // A_CHUNK
    for g in range(groups):
        cols = slice(g * HEAD_DIM, (g + 1) * HEAD_DIM)
        v_g = jnp.concatenate([vn[c * A_CHUNK:(c + 1) * A_CHUNK, cols] for c in range(nchunk)], axis=1)
        mixed = _dot(ws_ref[g], v_g)
        for c in range(nchunk):
            rows = slice(c * A_CHUNK, (c + 1) * A_CHUNK)
            o_ref[rows, cols] = (u[rows, cols] * (mixed[:, c * HEAD_DIM:(c + 1) * HEAD_DIM] + bs_ref[g])
                                 ).astype(o_ref.dtype)


def _gmlp(z, n_rows, a_v_gain, a_ws, a_bs, tg):
    groups = a_ws.shape[0]
    width = groups * HEAD_DIM
    bs_full = jnp.broadcast_to(a_bs[:, :, None], (groups, A_CHUNK, HEAD_DIM)).astype(F32)
    return pl.pallas_call(
        _gmlp_kernel,
        out_shape=jax.ShapeDtypeStruct((n_rows, width), BF16),
        grid=(n_rows // tg,),
        in_specs=[
            pl.BlockSpec((tg, width), lambda i: (i, Z_AU // width)),
            pl.BlockSpec((tg, width), lambda i: (i, Z_AV // width)),
            pl.BlockSpec((1, width), lambda i: (0, 0)),
            pl.BlockSpec((groups, A_CHUNK, A_CHUNK), lambda i: (0, 0, 0)),
            pl.BlockSpec((groups, A_CHUNK, HEAD_DIM), lambda i: (0, 0, 0)),
        ],
        out_specs=pl.BlockSpec((tg, width), lambda i: (i, 0)),
        compiler_params=_cparams(("parallel",)),
        name="gmlp",
    )(z, z, a_v_gain.reshape(1, width), a_ws.astype(BF16), bs_full)


def _gla_prep(k_ref, q_ref, r_ref, w2, bias, tri_ref, reverse):
    tb = k_ref.shape[0]
    heads = k_ref.shape[1] // B_DK
    nchunk = tb // B_CHUNK
    logit = _dot(r_ref[...].astype(BF16), w2) + bias
    soft = jnp.log2(1.0 + jnp.exp2(jnp.abs(logit) * -LOG2E))
    g = jnp.minimum(logit, 0.0) * (1.0 / B_TAU) - soft * (LN2 / B_TAU)
    g_hi = g.astype(BF16)
    g_lo = (g - g_hi.astype(F32)).astype(BF16)
    tri = tri_ref[...]
    bc = _dot(tri, g_hi) + _dot(tri, g_lo)
    last = [c * B_CHUNK + (0 if reverse else B_CHUNK - 1) for c in range(nchunk)]
    tots = [bc[r:r + 1, :] for r in last]
    tot_rows = jnp.concatenate([jnp.broadcast_to(t, (B_CHUNK, t.shape[1])) for t in tots], axis=0)
    kf = k_ref[...].astype(F32)
    q_in = (q_ref[...].astype(F32) * jnp.exp(bc) * (B_DK ** -0.5)).astype(BF16)
    k_in = kf * jnp.exp(-bc)
    k_out = kf * jnp.exp(tot_rows - bc)
    dec = [jnp.exp(t) for t in tots]
    lane = lax.broadcasted_iota(jnp.int32, (tb, LANES), 1)
    zeros = jnp.zeros((B_CHUNK, LANES), BF16)
    q_slabs, k_in_heads, k4_heads = [], [], []
    for h in range(heads):
        slab = slice((h // 2) * LANES, (h // 2 + 1) * LANES)
        mine = (lane >= B_DK) if h % 2 else (lane < B_DK)
        q_slabs.append(q_in[:, slab])
        k_in_heads.append(jnp.where(mine, k_in[:, slab], 0.0).astype(BF16))
        k_out_h = jnp.where(mine, k_out[:, slab], 0.0).astype(BF16)
        k4_heads.append(jnp.concatenate([
            jnp.concatenate([zeros] * c + [k_out_h[c * B_CHUNK:(c + 1) * B_CHUNK]] + [zeros] * (nchunk - 1 - c),
                            axis=0) for c in range(nchunk)], axis=1))
    return q_slabs, k_in_heads, k4_heads, dec


def _gla_kernel(kf_ref, vf_ref, qf_ref, rf_ref, kb_ref, vb_ref, qb_ref, rb_ref, w2_ref, b_ref,
                tri_ref, keep_ref, of_ref, ob_ref, sf_ref, sb_ref, a_ref, u_ref):
    @pl.when(pl.program_id(1) == 0)
    def _():
        sf_ref[...] = jnp.zeros_like(sf_ref)
        sb_ref[...] = jnp.zeros_like(sb_ref)

    heads = sf_ref.shape[0]
    nchunk = kf_ref.shape[0] // B_CHUNK
    v_refs, o_refs, st_refs = (vf_ref, vb_ref), (of_ref, ob_ref), (sf_ref, sb_ref)
    prep = [_gla_prep(kf_ref, qf_ref, rf_ref, w2_ref[0], b_ref[0], tri_ref.at[0], False),
            _gla_prep(kb_ref, qb_ref, rb_ref, w2_ref[1], b_ref[1], tri_ref.at[1], True)]
    jobs = [(d, h) for d in range(2) for h in range(heads)]
    for d, h in jobs:
        q_slabs, k_in_heads, _, _ = prep[d]
        a_ref[d * heads + h] = _dot_nt(q_slabs[h], k_in_heads[h])
    for d, h in jobs:
        v_h = v_refs[d][:, h * B_DV:(h + 1) * B_DV]
        u_ref[d * heads + h] = _dot_tn(v_h, prep[d][2][h])
    for d, h in jobs:
        vcols = slice(h * B_DV, (h + 1) * B_DV)
        attn = jnp.where(keep_ref[d] > 0.5, a_ref[d * heads + h], 0.0).astype(BF16)
        o_refs[d][:, vcols] = _dot(attn, v_refs[d][:, vcols])
    for d, h in jobs:
        q_s, dec = prep[d][0][h], prep[d][3]
        slab = slice((h // 2) * LANES, (h // 2 + 1) * LANES)
        vcols = slice(h * B_DV, (h + 1) * B_DV)
        st = st_refs[d][h]
        for c in (range(nchunk - 1, -1, -1) if d else range(nchunk)):
            rows = slice(c * B_CHUNK, (c + 1) * B_CHUNK)
            o_refs[d][rows, vcols] += _dot_nt(q_s[rows], st.astype(BF16))
            st = st * dec[c][:, slab] + u_ref[d * heads + h, :, c * LANES:(c + 1) * LANES]
        st_refs[d][h] = st


def _chunk_triangles(tb):
    t = np.arange(tb)[:, None]
    s = np.arange(tb)[None, :]
    same = (t // B_CHUNK) == (s // B_CHUNK)
    return np.stack([same & (s <= t), same & (s >= t)]).astype(np.float32)


def _gla(z, r, w2pad, bias, batch, seq, ctx_len, tb):
    n_rows = z.shape[0]
    heads = bias.shape[-1] // B_DK
    assert heads % 2 == 0 and 2 * B_DK == LANES
    kw, vw = heads * B_DK, heads * B_DV
    nlat, nctx = seq // tb, ctx_len // tb
    base = batch * nlat
    tri = _chunk_triangles(tb)

    def fwd(b, j):
        return jnp.where(j < nctx, base + b * nctx + j, b * nlat + (j - nctx))

    def bwd(b, j):
        return jnp.where(j < nctx, base + b * nctx + (nctx - 1 - j), b * nlat + (nlat - 1 - (j - nctx)))

    def specs(blk):
        return [pl.BlockSpec((tb, kw), lambda b, j: (blk(b, j), Z_BK // kw)),
                pl.BlockSpec((tb, vw), lambda b, j: (blk(b, j), Z_BV // vw)),
                pl.BlockSpec((tb, kw), lambda b, j: (blk(b, j), Z_BQ // kw)),
                pl.BlockSpec((tb, LANES), lambda b, j: (blk(b, j), 0))]

    return pl.pallas_call(
        _gla_kernel,
        out_shape=(jax.ShapeDtypeStruct((n_rows, vw), F32), jax.ShapeDtypeStruct((n_rows, vw), F32)),
        grid=(batch, nctx + nlat),
        in_specs=specs(fwd) + specs(bwd) + [
            pl.BlockSpec((2, LANES, kw), lambda b, j: (0, 0, 0)),
            pl.BlockSpec((2, 1, kw), lambda b, j: (0, 0, 0)),
            pl.BlockSpec((2, tb, tb), lambda b, j: (0, 0, 0)),
            pl.BlockSpec((2, tb, tb), lambda b, j: (0, 0, 0)),
        ],
        out_specs=(pl.BlockSpec((tb, vw), lambda b, j: (fwd(b, j), 0)),
                   pl.BlockSpec((tb, vw), lambda b, j: (bwd(b, j), 0))),
        scratch_shapes=[pltpu.VMEM((heads, B_DV, LANES), F32), pltpu.VMEM((heads, B_DV, LANES), F32),
                        pltpu.VMEM((2 * heads, tb, tb), F32),
                        pltpu.VMEM((2 * heads, B_DV, (tb // B_CHUNK) * LANES), F32)],
        compiler_params=_cparams(("parallel", "arbitrary")),
        name="gla",
    )(z, z, z, r, z, z, z, r, w2pad.astype(BF16), bias, jnp.asarray(tri, BF16), jnp.asarray(tri, F32))


def _attn_kernel(sink_ref, q_ref, kp_ref, kc_ref, kn_ref, kx_ref, vp_ref, vc_ref, vn_ref, vx_ref,
                 bias_ref, o_ref, s_ref):
    tq = q_ref.shape[0]
    n_heads = q_ref.shape[1] // HEAD_DIM
    kv_heads = kc_ref.shape[1] // HEAD_DIM
    group = n_heads // kv_heads
    n_loc = bias_ref.shape[1]

    def kv_cols(h):
        kh = h // group
        return slice(kh * HEAD_DIM, (kh + 1) * HEAD_DIM)

    def scores(kh):
        cols = kv_cols(kh * group)
        k_all = jnp.concatenate([kp_ref[:, cols], kc_ref[:, cols], kn_ref[:, cols], kx_ref[:, cols]], axis=0)
        q = jnp.concatenate([q_ref[:, h * HEAD_DIM:(h + 1) * HEAD_DIM]
                             for h in range(kh * group, (kh + 1) * group)], axis=0)
        s_ref[kh * group * tq:(kh + 1) * group * tq, :] = _dot_nt(q, k_all)

    def softmax_pv(h):
        cols = kv_cols(h)
        rows = slice(h * tq, (h + 1) * tq)
        v_all = jnp.concatenate([vp_ref[:, cols], vc_ref[:, cols], vn_ref[:, cols], vx_ref[:, cols]], axis=0)
        sink = sink_ref[h] * LOG2E
        sg = jnp.concatenate([s_ref[rows, :n_loc] + bias_ref[...], s_ref[rows, n_loc:]], axis=1)
        m = jnp.maximum(jnp.max(sg, axis=-1, keepdims=True), sink)
        e = jnp.exp2(sg - m)
        den = jnp.sum(e, axis=-1, keepdims=True) + jnp.exp2(sink - m)
        o_ref[:, h * HEAD_DIM:(h + 1) * HEAD_DIM] = (_dot(e.astype(BF16), v_all) / den).astype(o_ref.dtype)

    for kh in range(kv_heads):
        scores(kh)
    for h in range(n_heads):
        softmax_pv(h)


def _window_bias(tq):
    i = np.arange(tq)[:, None]
    j = np.arange(3 * tq)[None, :]
    band = (j >= i) & (j <= i + 2 * tq)
    cases = [band & (j >= tq), band, band & (j < 2 * tq), np.zeros_like(band)]
    return jnp.asarray(np.where(np.stack(cases), 0.0, NEG_BIG), F32)


def _attn(z, sink, batch, seq, ctx_len, with_ctx):
    n_heads = sink.shape[0]
    qw = n_heads * HEAD_DIM
    kvw = C_KVW
    tq = C_BLOCK
    nb = seq // tq
    cpb = ctx_len // tq if with_ctx else 0
    assert nb >= 2 and ctx_len % tq == 0
    ctx_base = batch * seq // ctx_len

    def own(b, n):
        return jnp.where(n < nb, b * nb + n, batch * nb + b * cpb + (n - nb))

    def prev(b, n):
        return b * nb + jnp.clip(n - 1, 0, nb - 1)

    def cur(b, n):
        return b * nb + jnp.minimum(n, nb - 1)

    def nxt(b, n):
        return b * nb + jnp.minimum(n + 1, nb - 1)

    def zspec(blk, col0):
        return pl.BlockSpec((tq, kvw), lambda b, n: (blk(b, n), col0 // kvw))

    def bias_case(b, n):
        return jnp.where(n >= nb, 3, jnp.where(n == 0, 0, jnp.where(n == nb - 1, 2, 1)))

    return pl.pallas_call(
        _attn_kernel,
        out_shape=jax.ShapeDtypeStruct((batch * (nb + cpb) * tq, qw), BF16),
        grid=(batch, nb + cpb),
        in_specs=[
            pl.BlockSpec(memory_space=pltpu.SMEM),
            pl.BlockSpec((tq, qw), lambda b, n: (own(b, n), Z_CQ // qw)),
            zspec(prev, Z_CK), zspec(cur, Z_CK), zspec(nxt, Z_CK),
            pl.BlockSpec((ctx_len, kvw), lambda b, n: (ctx_base + b, Z_CK // kvw)),
            zspec(prev, Z_CV), zspec(cur, Z_CV), zspec(nxt, Z_CV),
            pl.BlockSpec((ctx_len, kvw), lambda b, n: (ctx_base + b, Z_CV // kvw)),
            pl.BlockSpec((None, tq, 3 * tq), lambda b, n: (bias_case(b, n), 0, 0)),
        ],
        out_specs=pl.BlockSpec((tq, qw), lambda b, n: (own(b, n), 0)),
        scratch_shapes=[pltpu.VMEM((n_heads * tq, 3 * tq + ctx_len), F32)],
        compiler_params=_cparams(("parallel", "parallel")),
        name="window_attn",
    )(sink, z, z, z, z, z, z, z, z, z, _window_bias(tq))


MERGE_CHUNKS = 4


def _merge_kernel(x_ref, m_ref, a_ref, of_ref, ob_ref, og_ref, c_ref, ga_ref, gb_ref, gc_ref,
                  bg_ref, wa_ref, wb_ref, wc_ref, wo_ref, o_ref):
    heads = of_ref.shape[1] // B_DV
    o = of_ref[...] + ob_ref[...]
    bn = jnp.concatenate([_rms_head(o[:, h * B_DV:(h + 1) * B_DV], bg_ref[...]) for h in range(heads)], axis=1)
    b = (bn * _silu(og_ref[...].astype(F32))).astype(BF16)
    a, c = a_ref[...], c_ref[...]
    d = o_ref.shape[1]
    cw = d // MERGE_CHUNKS

    def merged_cols(k):
        cols = slice(k * cw, (k + 1) * cw)
        return (jax.nn.sigmoid(ga_ref[:, cols].astype(F32)) * _dot(a, wa_ref[:, cols])
                + jax.nn.sigmoid(gb_ref[:, cols].astype(F32)) * _dot(b, wb_ref[:, cols])
                + jax.nn.sigmoid(gc_ref[:, cols].astype(F32)) * _dot(c, wc_ref[:, cols])).astype(BF16)

    pending = merged_cols(0)
    mix = None
    for k in range(MERGE_CHUNKS):
        ready = pending
        if k + 1 < MERGE_CHUNKS:
            pending = merged_cols(k + 1)
        part = _dot(ready, wo_ref[k * cw:(k + 1) * cw, :])
        mix = part if mix is None else mix + part
    o_ref[...] = x_ref[...] + m_ref[2] * mix


def _merge(x, n_rows, mod, layer, row_fn, z, a, o_f, o_b, c, b_norm_g, wa, wb, wc, wo, tm):
    d = x.shape[1]
    aw, bw, cw = a.shape[1], o_f.shape[1], c.shape[1]

    def const(rows):
        return pl.BlockSpec((None, rows, d), lambda i: (layer, 0, 0), pipeline_mode=pl.Buffered(1))

    return pl.pallas_call(
        _merge_kernel,
        out_shape=jax.ShapeDtypeStruct((n_rows, d), F32),
        grid=(n_rows // tm,),
        in_specs=[
            pl.BlockSpec((tm, d), lambda i: (i, 0)),
            _mod_spec(layer, 1, row_fn, d),
            pl.BlockSpec((tm, aw), lambda i: (i, 0)),
            pl.BlockSpec((tm, bw), lambda i: (i, 0)),
            pl.BlockSpec((tm, bw), lambda i: (i, 0)),
            pl.BlockSpec((tm, bw), lambda i: (i, Z_BG // bw)),
            pl.BlockSpec((tm, cw), lambda i: (i, 0)),
            pl.BlockSpec((tm, d), lambda i: (i, Z_GATE // d)),
            pl.BlockSpec((tm, d), lambda i: (i, Z_GATE // d + 1)),
            pl.BlockSpec((tm, d), lambda i: (i, Z_GATE // d + 2)),
            pl.BlockSpec((1, B_DV), lambda i: (0, 0)),
            const(aw), const(bw), const(cw), const(d),
        ],
        out_specs=pl.BlockSpec((tm, d), lambda i: (i, 0)),
        compiler_params=_cparams(("parallel",)),
        name="merge",
    )(x, mod, a, o_f, o_b, z, c, z, z, z, b_norm_g.reshape(1, B_DV), wa, wb, wc, wo)


def _rope_tables(seq, pad_rows):
    half = HEAD_DIM // 4
    t = np.arange(seq)
    pos = np.stack([t // GRID_W, t % GRID_W], axis=1).astype(np.float32)
    inv_freq = jnp.asarray(ROPE_BASE, F32) ** (-jnp.arange(half, dtype=F32) / half)
    ang = jnp.asarray(pos)[:, :, None] * inv_freq[None, None, :]
    cos = jnp.cos(ang)
    sin = jnp.sin(ang)
    cos_t = jnp.concatenate([cos, cos], axis=-1).reshape(seq, HEAD_DIM)
    sin_t = jnp.concatenate([-sin, sin], axis=-1).reshape(seq, HEAD_DIM)
    cos_t = jnp.concatenate([cos_t, jnp.ones((pad_rows, HEAD_DIM), F32)], axis=0)
    sin_t = jnp.concatenate([sin_t, jnp.zeros((pad_rows, HEAD_DIM), F32)], axis=0)
    return cos_t, sin_t


def _reorder_w_in(w, d):
    bkw, bvw, ckvw, aw, cqw = 256, 512, 256, 512, 1024
    o_bk = 0
    o_bv = o_bk + bkw
    o_ck = o_bv + bvw
    o_cv = o_ck + ckvw
    o_au = o_cv + ckvw
    o_av = o_au + aw
    o_bq = o_av + aw
    o_bg = o_bq + bkw
    o_cq = o_bg + bvw
    o_gate = o_cq + cqw
    seg = lambda s, width: w[..., s:s + width].astype(BF16)
    return jnp.concatenate([seg(o_cq, cqw), seg(o_ck, ckvw), seg(o_cv, ckvw), seg(o_bk, bkw), seg(o_bq, bkw), seg(o_bv, bvw),
                            seg(o_au, aw), seg(o_av, aw), seg(o_bg, bvw), seg(o_gate, 3 * d)], axis=-1)


def kernel(x, c, ctx, c_ctx, w_ada, b_ada, norm_g, w_ffn_up, w_ffn_down, w_in, a_v_gain, a_ws, a_bs,
           b_decay_w1, b_decay_w2, b_decay_b, b_norm_g, c_q_gain, c_k_gain, c_sink,
           w_br_a, w_br_b, w_br_c, w_out):
    batch, seq, d = x.shape
    ctx_len = ctx.shape[1]
    depth = w_ada.shape[0]
    assert d == 2048 and batch < MOD_ROWS and seq % 256 == 0 and ctx_len == 256
    n_lat, n_ctx = batch * seq, batch * ctx_len
    n_all = n_lat + n_ctx
    for t in (TM_FFN, TM_IN, TM_MERGE, TG_GMLP, TB_GLA):
        assert seq % t == 0 and n_ctx % t == 0

    def row_fn(t):
        return lambda i: jnp.minimum(i * t // seq, batch)

    c8 = jnp.zeros((MOD_ROWS, d), F32).at[:batch].set(c).at[batch].set(c_ctx)
    mod = _with_norm_gains(_mod_table(c8, w_ada, b_ada), norm_g)
    cos_t, sin_t = _rope_tables(seq, TM_IN)

    ffn = w_ffn_down.shape[2]
    w_up = w_ffn_up[0].reshape(2 * d, 2 * ffn).astype(BF16)
    w_dn = w_ffn_down[0].reshape(2 * ffn, d).astype(BF16)
    w_up_f32 = w_ffn_up.reshape(depth * 2 * d, 2 * ffn)
    w_dn_f32 = w_ffn_down.reshape(depth * 2 * ffn, d)
    w_in_r = _reorder_w_in(w_in, d)
    w1 = jnp.concatenate([b_decay_w1[:, 0], b_decay_w1[:, 1],
                          jnp.zeros((depth, d, LANES - 2 * B_RANK), F32)], axis=-1).astype(BF16)
    kw = b_decay_w2.shape[-1]
    w2pad = jnp.zeros((depth, 2, LANES, kw), F32).at[:, 0, :B_RANK].set(b_decay_w2[:, 0]).at[
        :, 1, B_RANK:2 * B_RANK].set(b_decay_w2[:, 1])
    n_qh = c_sink.shape[1]
    qk_gain = jnp.concatenate([jnp.tile(c_q_gain * (HEAD_DIM ** -0.5 * LOG2E), (1, n_qh)),
                               jnp.tile(c_k_gain, (1, C_KVW // HEAD_DIM))], axis=1)
    wa, wb, wc, wo = (w.astype(BF16) for w in (w_br_a, w_br_b, w_br_c, w_out))

    srcs = [x.reshape(n_lat, d), ctx.reshape(n_ctx, d)]
    for l in range(depth):
        last = l == depth - 1
        n_mix = n_lat if last else n_all
        tm_first = TM_FFN if len(srcs) == 1 else TM_FFN // 2
        casts = () if last else ((w_up_f32, "cols", l + 1, depth), (w_dn_f32, "rows", l + 1, depth))
        xs = _ffn(srcs, n_all, mod, l, 0, row_fn(tm_first), w_up, w_dn, tm_first, TF, cast=casts)
        if not last:
            xs, w_up_next, w_dn_next = xs
        z, r = _inproj(xs, mod, l, row_fn(TM_IN), w_in_r, w1, qk_gain[l:l + 1], cos_t, sin_t,
                       n_lat, seq, TM_IN, TN_IN, ctx_state_only=last)
        a_out = _gmlp(z, n_mix, a_v_gain[l], a_ws[l], a_bs[l], TG_GMLP)
        o_f, o_b = _gla(z, r, w2pad[l], b_decay_b[l].reshape(2, 1, kw), batch, seq, ctx_len, TB_GLA)
        c_out = _attn(z, c_sink[l], batch, seq, ctx_len, with_ctx=not last)
        xs = _merge(xs, n_mix, mod, l, row_fn(TM_MERGE), z, a_out, o_f, o_b, c_out, b_norm_g[l],
                    wa, wb, wc, wo, TM_MERGE)
        xs = _ffn([xs], n_mix, mod, l, 1, row_fn(TM_FFN), w_up, w_dn, TM_FFN, TF)
        if not last:
            w_up, w_dn = w_up_next, w_dn_next
        srcs = [xs]
    return xs.reshape(batch, seq, d)
```

```python
import functools

import jax
import jax.numpy as jnp
import numpy as np
from jax import lax
from jax.experimental import pallas as pl
from jax.experimental.pallas import tpu as pltpu

F32 = jnp.float32
BF16 = jnp.bfloat16

HEAD_DIM = 128
EPS = 1e-6
N_MOD = 9
GRID_W = 64
A_CHUNK = 128
B_DK = 64
B_DV = 128
B_RANK = 16
B_TAU = 16.0
B_CHUNK = 64
C_BLOCK = 128
ROPE_BASE = 10000.0
MOD_ROWS = 8
LANES = 128
NEG_BIG = -1e30
LOG2E = 1.4426950408889634
LN2 = 0.6931471805599453
VMEM_LIMIT = 56 * 1024 * 1024

Z_CQ, Z_CK, Z_CV, Z_BK, Z_BQ, Z_BV, Z_AU, Z_AV, Z_BG, Z_GATE = (
    0, 1024, 1280, 1536, 1792, 2048, 2560, 3072, 3584, 4096)
QK_COLS = Z_CV
C_KVW = Z_CV - Z_CK

TM_FFN = 1024
TF = 512
TM_IN = 512
TN_IN = 2 * QK_COLS
TM_MERGE = 256
TG_GMLP = 512
TB_GLA = 256


def _cparams(sem):
    return pltpu.CompilerParams(dimension_semantics=sem, vmem_limit_bytes=VMEM_LIMIT)


def _dot(a, b):
    return jnp.dot(a, b, preferred_element_type=F32)


def _dot_nt(a, b):
    return lax.dot_general(a, b, (((1,), (1,)), ((), ())), preferred_element_type=F32)


def _dot_tn(a, b):
    return lax.dot_general(a, b, (((0,), (0,)), ((), ())), preferred_element_type=F32)


def _silu(x):
    return x * jax.nn.sigmoid(x)


NORM_ROWS = 64


def _rms_mod_to(x_ref, m_ref, h_ref):
    shift = m_ref[0]
    gain = m_ref[3] * (1.0 + m_ref[1])

    def one_pass(c, carry):
        rows = pl.ds(pl.multiple_of(c * NORM_ROWS, NORM_ROWS), NORM_ROWS)
        x = x_ref[rows, :]
        rs = lax.rsqrt(jnp.mean(x * x, axis=-1, keepdims=True) + EPS)
        h_ref[rows, :] = (x * rs * gain + shift).astype(BF16)
        return carry

    lax.fori_loop(0, x_ref.shape[0] // NORM_ROWS, one_pass, 0, unroll=2)


def _rms_head(xh, gain):
    ms = jnp.mean(xh * xh, axis=-1, keepdims=True)
    return xh * lax.rsqrt(ms + EPS) * gain


def _mod_kernel(c_ref, w_ref, b_ref, o_ref):
    act = _silu(c_ref[...])
    o_ref[...] = _dot(act.astype(BF16), w_ref[...].astype(BF16)) + b_ref[...]


def _mod_table(c8, w_ada, b_ada):
    depth, d, nd = w_ada.shape
    tn = 1024
    nj = d // tn
    out = pl.pallas_call(
        _mod_kernel,
        out_shape=jax.ShapeDtypeStruct((depth, N_MOD, MOD_ROWS, d), F32),
        grid=(depth, N_MOD, nj),
        in_specs=[
            pl.BlockSpec((MOD_ROWS, d), lambda l, k, j: (0, 0)),
            pl.BlockSpec((None, d, tn), lambda l, k, j: (l, 0, k * nj + j)),
            pl.BlockSpec((None, 1, tn), lambda l, k, j: (l, 0, k * nj + j)),
        ],
        out_specs=pl.BlockSpec((None, None, MOD_ROWS, tn), lambda l, k, j: (l, k, 0, j)),
        compiler_params=_cparams(("parallel", "parallel", "parallel")),
        name="mod_table",
    )(c8, w_ada, b_ada.reshape(depth, 1, nd))
    return out


MOD_GROUP = 4


def _with_norm_gains(mod, norm_g):
    depth, _, rows, d = mod.shape
    subs = norm_g.shape[1]
    gains = jnp.broadcast_to(norm_g[:, :, None, None, :], (depth, subs, 1, rows, d))
    table = jnp.concatenate([mod.reshape(depth, subs, N_MOD // subs, rows, d), gains], axis=2)
    return table.reshape(depth, subs * MOD_GROUP, rows, 1, d)


def _mod_spec(layer, sub, row_fn, d):
    return pl.BlockSpec((None, MOD_GROUP, None, 1, d), lambda i, *_: (layer, sub, row_fn(i), 0, 0))


def _ffn_kernel(*refs, tiles_per_source, n_cast):
    n_src = len(tiles_per_source)
    x_refs = refs[:n_src]
    m_ref, wg_ref, wv_ref, wd_ref = refs[n_src:n_src + 4]
    cast_in_refs = refs[n_src + 4:n_src + 4 + n_cast]
    o_ref = refs[n_src + 4 + n_cast]
    cast_out_refs = refs[n_src + 5 + n_cast:n_src + 5 + 2 * n_cast]
    h_ref = refs[n_src + 5 + 2 * n_cast]
    acc_ref = o_ref
    i, f = pl.program_id(0), pl.program_id(1)

    for cast_in_ref, cast_out_ref in zip(cast_in_refs, cast_out_refs):
        cast_out_ref[...] = cast_in_ref[...].astype(BF16)

    def with_own_source(fn):
        start = 0
        for x_ref, n_tiles in zip(x_refs, tiles_per_source):
            if n_src == 1:
                fn(x_ref)
            else:
                pl.when(jnp.logical_and(i >= start, i < start + n_tiles))(lambda x_ref=x_ref: fn(x_ref))
            start += n_tiles

    last = pl.num_programs(1) - 1

    def prologue(x_ref):
        _rms_mod_to(x_ref, m_ref, h_ref)

    def chunk():
        h = h_ref[...]
        act = (_silu(_dot(h, wg_ref[...])) * _dot(h, wv_ref[...])).astype(BF16)
        return _dot(act, wd_ref[...])

    @pl.when(f == 0)
    def _():
        with_own_source(prologue)
        acc_ref[...] = chunk()

    if n_src == 1:
        @pl.when(jnp.logical_and(f > 0, f < last))
        def _():
            acc_ref[...] += chunk()

        @pl.when(f == last)
        def _():
            o_ref[...] = x_refs[0][...] + 0.5 * m_ref[2] * (acc_ref[...] + chunk())
    else:
        @pl.when(f > 0)
        def _():
            acc_ref[...] += chunk()

        def epilogue(x_ref):
            o_ref[...] = x_ref[...] + 0.5 * m_ref[2] * acc_ref[...]

        @pl.when(f == last)
        def _():
            with_own_source(epilogue)


def _ffn(xs, n_rows, mod, layer, which, row_fn, w_up, w_down, wsel, tm, tf, cast=()):
    d = xs[0].shape[1]
    nf = w_up.shape[1] // (2 * tf)
    nt = n_rows // tm
    tiles, starts = [], []
    for x in xs:
        starts.append(sum(tiles))
        tiles.append(min(x.shape[0] // tm, nt - sum(tiles)))

    def src_spec(start, n_tiles):
        mode = pl.Buffered(1) if (n_tiles == 1 and len(xs) > 1) else None
        return pl.BlockSpec((tm, d), lambda i, f: (jnp.clip(i - start, 0, n_tiles - 1), 0), pipeline_mode=mode)

    in_specs = [src_spec(s, t) for s, t in zip(starts, tiles)] + [
        _mod_spec(layer, 2 * which, row_fn, d),
        pl.BlockSpec((d, tf), lambda i, f: (wsel, f)),
        pl.BlockSpec((d, tf), lambda i, f: (wsel, nf + f)),
        pl.BlockSpec((tf, d), lambda i, f: (wsel * nf + f, 0)),
    ]
    out_shape = [jax.ShapeDtypeStruct((n_rows, d), F32)]
    out_specs = [pl.BlockSpec((tm, d), lambda i, f: (i, 0))]
    operands = [*xs, mod, w_up, w_up, w_down]
    row_blocks = 1 << (nt.bit_length() - 1)
    cast_in_specs = []
    for src, split, row0, rows in cast:
        cols = src.shape[1]
        if split == "cols":
            br, bc = rows // row_blocks, cols // nf
            first = row0 // br
            in_blk = lambda i, f, first=first: (first + jnp.minimum(i, row_blocks - 1), f)
            out_blk = lambda i, f: (i, f)
        else:
            br, bc = rows // (row_blocks * nf), cols
            first = row0 // br
            in_blk = lambda i, f, first=first: (first + jnp.minimum(i, row_blocks - 1) * nf + f, 0)
            out_blk = lambda i, f: (i * nf + f, 0)
        assert br % 16 == 0 and bc % LANES == 0 and rows % br == 0 and cols % bc == 0 and row0 % br == 0
        cast_in_specs.append(pl.BlockSpec((br, bc), in_blk))
        out_shape.append(jax.ShapeDtypeStruct((rows // row_blocks * nt, cols), BF16))
        out_specs.append(pl.BlockSpec((br, bc), out_blk))
        operands.append(src)

    outs = pl.pallas_call(
        functools.partial(_ffn_kernel, tiles_per_source=tuple(tiles), n_cast=len(cast)),
        out_shape=out_shape,
        grid=(nt, nf),
        in_specs=in_specs + cast_in_specs,
        out_specs=out_specs,
        scratch_shapes=[pltpu.VMEM((tm, d), BF16)],
        compiler_params=_cparams(("parallel", "arbitrary")),
        name="ffn",
    )(*operands)
    return outs[0] if not cast else tuple(outs)


def _inproj_kernel(x_ref, m_ref, w_ref, w1_ref, qkg_ref, cos_ref, sin_ref, z_ref, r_ref, h_ref, *,
                   full_tiles):
    i, n = pl.program_id(0), pl.program_id(1)

    @pl.when(jnp.logical_and(n != 0, i < full_tiles))
    def _():
        z_ref[...] = _dot(h_ref[...], w_ref[...]).astype(z_ref.dtype)

    @pl.when(jnp.logical_and(n != 0, i >= full_tiles))
    def _():
        z_ref[...] = jnp.zeros_like(z_ref)

    @pl.when(n == 0)
    def _():
        _rms_mod_to(x_ref, m_ref, h_ref)
        h = h_ref[...]
        r_ref[...] = _dot(h, w1_ref[...])
        zt = _dot(h, w_ref[:, :QK_COLS])
        if w_ref.shape[1] > QK_COLS:
            z_ref[:, QK_COLS:] = _dot(h, w_ref[:, QK_COLS:]).astype(z_ref.dtype)
        pair = 2 * HEAD_DIM
        r = lax.broadcasted_iota(jnp.int32, (pair, pair), 0)
        c = lax.broadcasted_iota(jnp.int32, (pair, pair), 1)
        head_sum = jnp.where(jnp.bitwise_and(jnp.bitwise_xor(r, c), HEAD_DIM) == 0, 1.0, 0.0).astype(BF16)
        swap = jnp.where(c == jnp.bitwise_xor(r, HEAD_DIM // 4), 1.0, 0.0).astype(BF16)
        cos = jnp.concatenate([cos_ref[...]] * 2, axis=1)
        sin = jnp.concatenate([sin_ref[...]] * 2, axis=1)
        for p in range(QK_COLS // pair):
            cols = slice(p * pair, (p + 1) * pair)
            y = zt[:, cols]
            ms = _dot((y * y).astype(BF16), head_sum) * (1.0 / HEAD_DIM)
            y = y * lax.rsqrt(ms + EPS) * qkg_ref[:, cols]
            z_ref[:, cols] = (y * cos + _dot(y.astype(BF16), swap) * sin).astype(z_ref.dtype)


def _inproj(x, mod, layer, row_fn, w_in, w1, qk_gain, cos_t, sin_t, n_lat, seq, tm, tn, ctx_state_only):
    n_rows, d = x.shape
    ncols = w_in.shape[2]
    assert tn % QK_COLS == 0 and ncols % tn == 0 and Z_BV + 512 <= tn
    rope_blocks = seq // tm
    nt = n_rows // tm
    full_tiles = n_lat // tm if ctx_state_only else nt

    def rope_blk(i, n):
        return jnp.where(i * tm < n_lat, (i % rope_blocks), rope_blocks)

    def w_blk(i, n):
        return jnp.where(i < full_tiles, n, 0)

    return pl.pallas_call(
        functools.partial(_inproj_kernel, full_tiles=full_tiles),
        out_shape=(jax.ShapeDtypeStruct((n_rows, ncols), BF16),
                   jax.ShapeDtypeStruct((n_rows, LANES), F32)),
        grid=(n_rows // tm, ncols // tn),
        in_specs=[
            pl.BlockSpec((tm, d), lambda i, n: (i, 0)),
            _mod_spec(layer, 1, row_fn, d),
            pl.BlockSpec((None, d, tn), lambda i, n: (layer, 0, w_blk(i, n))),
            pl.BlockSpec((None, d, LANES), lambda i, n: (layer, 0, 0)),
            pl.BlockSpec((1, QK_COLS), lambda i, n: (0, 0)),
            pl.BlockSpec((tm, HEAD_DIM), lambda i, n: (rope_blk(i, n), 0)),
            pl.BlockSpec((tm, HEAD_DIM), lambda i, n: (rope_blk(i, n), 0)),
        ],
        out_specs=(pl.BlockSpec((tm, tn), lambda i, n: (i, n)),
                   pl.BlockSpec((tm, LANES), lambda i, n: (i, 0))),
        scratch_shapes=[pltpu.VMEM((tm, d), BF16)],
        compiler_params=_cparams(("parallel", "arbitrary")),
        name="inproj",
    )(x, mod, w_in, w1, qk_gain, cos_t, sin_t)


def _gmlp_kernel(u_ref, v_ref, gain_ref, ws_ref, bs_ref, o_ref):
    tg, width = u_ref.shape
    groups = width // HEAD_DIM
    u = jax.nn.gelu(u_ref[...].astype(F32))
    v = jax.nn.gelu(v_ref[...].astype(F32))
    mu = jnp.mean(v, axis=-1, keepdims=True)
    vc = v - mu
    var = jnp.mean(vc * vc, axis=-1, keepdims=True)
    vn = (vc * lax.rsqrt(var + EPS) * gain_ref[...]).astype(BF16)
    nchunk = tg // A_CHUNK
    for g in range(groups):
        cols = slice(g * HEAD_DIM, (g + 1) * HEAD_DIM)
        v_g = jnp.concatenate([vn[c * A_CHUNK:(c + 1) * A_CHUNK, cols] for c in range(nchunk)], axis=1)
        mixed = _dot(ws_ref[g], v_g)
        for c in range(nchunk):
            rows = slice(c * A_CHUNK, (c + 1) * A_CHUNK)
            o_ref[rows, cols] = (u[rows, cols] * (mixed[:, c * HEAD_DIM:(c + 1) * HEAD_DIM] + bs_ref[g])
                                 ).astype(o_ref.dtype)


def _gmlp(z, n_rows, a_v_gain, a_ws, a_bs, tg):
    groups = a_ws.shape[0]
    width = groups * HEAD_DIM
    bs_full = jnp.broadcast_to(a_bs[:, :, None], (groups, A_CHUNK, HEAD_DIM)).astype(F32)
    return pl.pallas_call(
        _gmlp_kernel,
        out_shape=jax.ShapeDtypeStruct((n_rows, width), BF16),
        grid=(n_rows // tg,),
        in_specs=[
            pl.BlockSpec((tg, width), lambda i: (i, Z_AU // width)),
            pl.BlockSpec((tg, width), lambda i: (i, Z_AV // width)),
            pl.BlockSpec((1, width), lambda i: (0, 0)),
            pl.BlockSpec((groups, A_CHUNK, A_CHUNK), lambda i: (0, 0, 0)),
            pl.BlockSpec((groups, A_CHUNK, HEAD_DIM), lambda i: (0, 0, 0)),
        ],
        out_specs=pl.BlockSpec((tg, width), lambda i: (i, 0)),
        compiler_params=_cparams(("parallel",)),
        name="gmlp",
    )(z, z, a_v_gain.reshape(1, width), a_ws.astype(BF16), bs_full)


def _gla_prep(k_ref, q_ref, r_ref, w2, bias, tri_ref, reverse):
    tb = k_ref.shape[0]
    heads = k_ref.shape[1] // B_DK
    nchunk = tb // B_CHUNK
    logit = _dot(r_ref[...].astype(BF16), w2) + bias
    soft = jnp.log2(1.0 + jnp.exp2(jnp.abs(logit) * -LOG2E))
    g = jnp.minimum(logit, 0.0) * (1.0 / B_TAU) - soft * (LN2 / B_TAU)
    g_hi = g.astype(BF16)
    g_lo = (g - g_hi.astype(F32)).astype(BF16)
    tri = tri_ref[...]
    bc = _dot(tri, g_hi) + _dot(tri, g_lo)
    last = [c * B_CHUNK + (0 if reverse else B_CHUNK - 1) for c in range(nchunk)]
    tots = [bc[r:r + 1, :] for r in last]
    tot_rows = jnp.concatenate([jnp.broadcast_to(t, (B_CHUNK, t.shape[1])) for t in tots], axis=0)
    kf = k_ref[...].astype(F32)
    q_in = (q_ref[...].astype(F32) * jnp.exp(bc) * (B_DK ** -0.5)).astype(BF16)
    k_in = kf * jnp.exp(-bc)
    k_out = kf * jnp.exp(tot_rows - bc)
    dec = [jnp.exp(t) for t in tots]
    lane = lax.broadcasted_iota(jnp.int32, (tb, LANES), 1)
    zeros = jnp.zeros((B_CHUNK, LANES), BF16)
    q_slabs, k_in_heads, k4_heads = [], [], []
    for h in range(heads):
        slab = slice((h // 2) * LANES, (h // 2 + 1) * LANES)
        mine = (lane >= B_DK) if h % 2 else (lane < B_DK)
        q_slabs.append(q_in[:, slab])
        k_in_heads.append(jnp.where(mine, k_in[:, slab], 0.0).astype(BF16))
        k_out_h = jnp.where(mine, k_out[:, slab], 0.0).astype(BF16)
        k4_heads.append(jnp.concatenate([
            jnp.concatenate([zeros] * c + [k_out_h[c * B_CHUNK:(c + 1) * B_CHUNK]] + [zeros] * (nchunk - 1 - c),
                            axis=0) for c in range(nchunk)], axis=1))
    return q_slabs, k_in_heads, k4_heads, dec


def _gla_kernel(kf_ref, vf_ref, qf_ref, rf_ref, kb_ref, vb_ref, qb_ref, rb_ref, w2_ref, b_ref,
                tri_ref, keep_ref, of_ref, ob_ref, sf_ref, sb_ref, a_ref, u_ref):
    @pl.when(pl.program_id(1) == 0)
    def _():
        sf_ref[...] = jnp.zeros_like(sf_ref)
        sb_ref[...] = jnp.zeros_like(sb_ref)

    heads = sf_ref.shape[0]
    nchunk = kf_ref.shape[0] // B_CHUNK
    v_refs, o_refs, st_refs = (vf_ref, vb_ref), (of_ref, ob_ref), (sf_ref, sb_ref)
    prep = [_gla_prep(kf_ref, qf_ref, rf_ref, w2_ref[0], b_ref[0], tri_ref.at[0], False),
            _gla_prep(kb_ref, qb_ref, rb_ref, w2_ref[1], b_ref[1], tri_ref.at[1], True)]
    jobs = [(d, h) for d in range(2) for h in range(heads)]
    for d, h in jobs:
        q_slabs, k_in_heads, _, _ = prep[d]
        a_ref[d * heads + h] = _dot_nt(q_slabs[h], k_in_heads[h])
    for d, h in jobs:
        v_h = v_refs[d][:, h * B_DV:(h + 1) * B_DV]
        u_ref[d * heads + h] = _dot_tn(v_h, prep[d][2][h])
    for d, h in jobs:
        vcols = slice(h * B_DV, (h + 1) * B_DV)
        attn = jnp.where(keep_ref[d] > 0.5, a_ref[d * heads + h], 0.0).astype(BF16)
        o_refs[d][:, vcols] = _dot(attn, v_refs[d][:, vcols])
    for d, h in jobs:
        q_s, dec = prep[d][0][h], prep[d][3]
        slab = slice((h // 2) * LANES, (h // 2 + 1) * LANES)
        vcols = slice(h * B_DV, (h + 1) * B_DV)
        st = st_refs[d][h]
        for c in (range(nchunk - 1, -1, -1) if d else range(nchunk)):
            rows = slice(c * B_CHUNK, (c + 1) * B_CHUNK)
            o_refs[d][rows, vcols] += _dot_nt(q_s[rows], st.astype(BF16))
            st = st * dec[c][:, slab] + u_ref[d * heads + h, :, c * LANES:(c + 1) * LANES]
        st_refs[d][h] = st


def _chunk_triangles(tb):
    t = np.arange(tb)[:, None]
    s = np.arange(tb)[None, :]
    same = (t // B_CHUNK) == (s // B_CHUNK)
    return np.stack([same & (s <= t), same & (s >= t)]).astype(np.float32)


def _gla(z, r, w2pad, bias, batch, seq, ctx_len, tb):
    n_rows = z.shape[0]
    heads = bias.shape[-1] // B_DK
    assert heads % 2 == 0 and 2 * B_DK == LANES
    kw, vw = heads * B_DK, heads * B_DV
    nlat, nctx = seq // tb, ctx_len // tb
    base = batch * nlat
    tri = _chunk_triangles(tb)

    def fwd(b, j):
        return jnp.where(j < nctx, base + b * nctx + j, b * nlat + (j - nctx))

    def bwd(b, j):
        return jnp.where(j < nctx, base + b * nctx + (nctx - 1 - j), b * nlat + (nlat - 1 - (j - nctx)))

    def specs(blk):
        return [pl.BlockSpec((tb, kw), lambda b, j: (blk(b, j), Z_BK // kw)),
                pl.BlockSpec((tb, vw), lambda b, j: (blk(b, j), Z_BV // vw)),
                pl.BlockSpec((tb, kw), lambda b, j: (blk(b, j), Z_BQ // kw)),
                pl.BlockSpec((tb, LANES), lambda b, j: (blk(b, j), 0))]

    return pl.pallas_call(
        _gla_kernel,
        out_shape=(jax.ShapeDtypeStruct((n_rows, vw), F32), jax.ShapeDtypeStruct((n_rows, vw), F32)),
        grid=(batch, nctx + nlat),
        in_specs=specs(fwd) + specs(bwd) + [
            pl.BlockSpec((2, LANES, kw), lambda b, j: (0, 0, 0)),
            pl.BlockSpec((2, 1, kw), lambda b, j: (0, 0, 0)),
            pl.BlockSpec((2, tb, tb), lambda b, j: (0, 0, 0)),
            pl.BlockSpec((2, tb, tb), lambda b, j: (0, 0, 0)),
        ],
        out_specs=(pl.BlockSpec((tb, vw), lambda b, j: (fwd(b, j), 0)),
                   pl.BlockSpec((tb, vw), lambda b, j: (bwd(b, j), 0))),
        scratch_shapes=[pltpu.VMEM((heads, B_DV, LANES), F32), pltpu.VMEM((heads, B_DV, LANES), F32),
                        pltpu.VMEM((2 * heads, tb, tb), F32),
                        pltpu.VMEM((2 * heads, B_DV, (tb // B_CHUNK) * LANES), F32)],
        compiler_params=_cparams(("parallel", "arbitrary")),
        name="gla",
    )(z, z, z, r, z, z, z, r, w2pad.astype(BF16), bias, jnp.asarray(tri, BF16), jnp.asarray(tri, F32))


def _attn_kernel(sink_ref, q_ref, kp_ref, kc_ref, kn_ref, kx_ref, vp_ref, vc_ref, vn_ref, vx_ref,
                 bias_ref, o_ref, s_ref):
    tq = q_ref.shape[0]
    n_heads = q_ref.shape[1] // HEAD_DIM
    kv_heads = kc_ref.shape[1] // HEAD_DIM
    group = n_heads // kv_heads
    n_loc = bias_ref.shape[1]

    def kv_cols(h):
        kh = h // group
        return slice(kh * HEAD_DIM, (kh + 1) * HEAD_DIM)

    def scores(kh):
        cols = kv_cols(kh * group)
        k_all = jnp.concatenate([kp_ref[:, cols], kc_ref[:, cols], kn_ref[:, cols], kx_ref[:, cols]], axis=0)
        q = jnp.concatenate([q_ref[:, h * HEAD_DIM:(h + 1) * HEAD_DIM]
                             for h in range(kh * group, (kh + 1) * group)], axis=0)
        s_ref[kh * group * tq:(kh + 1) * group * tq, :] = _dot_nt(q, k_all)

    def softmax_pv(h):
        cols = kv_cols(h)
        rows = slice(h * tq, (h + 1) * tq)
        v_all = jnp.concatenate([vp_ref[:, cols], vc_ref[:, cols], vn_ref[:, cols], vx_ref[:, cols]], axis=0)
        sink = sink_ref[h] * LOG2E
        sg = jnp.concatenate([s_ref[rows, :n_loc] + bias_ref[...], s_ref[rows, n_loc:]], axis=1)
        m = jnp.maximum(jnp.max(sg, axis=-1, keepdims=True), sink)
        e = jnp.exp2(sg - m)
        den = jnp.sum(e, axis=-1, keepdims=True) + jnp.exp2(sink - m)
        o_ref[:, h * HEAD_DIM:(h + 1) * HEAD_DIM] = (_dot(e.astype(BF16), v_all) / den).astype(o_ref.dtype)

    for kh in range(kv_heads):
        scores(kh)
    for h in range(n_heads):
        softmax_pv(h)


def _window_bias(tq):
    i = np.arange(tq)[:, None]
    j = np.arange(3 * tq)[None, :]
    band = (j >= i) & (j <= i + 2 * tq)
    cases = [band & (j >= tq), band, band & (j < 2 * tq), np.zeros_like(band)]
    return jnp.asarray(np.where(np.stack(cases), 0.0, NEG_BIG), F32)


def _attn(z, sink, batch, seq, ctx_len, with_ctx):
    n_heads = sink.shape[0]
    qw = n_heads * HEAD_DIM
    kvw = C_KVW
    tq = C_BLOCK
    nb = seq // tq
    cpb = ctx_len // tq if with_ctx else 0
    assert nb >= 2 and ctx_len % tq == 0
    ctx_base = batch * seq // ctx_len

    def own(b, n):
        return jnp.where(n < nb, b * nb + n, batch * nb + b * cpb + (n - nb))

    def prev(b, n):
        return b * nb + jnp.clip(n - 1, 0, nb - 1)

    def cur(b, n):
        return b * nb + jnp.minimum(n, nb - 1)

    def nxt(b, n):
        return b * nb + jnp.minimum(n + 1, nb - 1)

    def zspec(blk, col0):
        return pl.BlockSpec((tq, kvw), lambda b, n: (blk(b, n), col0 // kvw))

    def bias_case(b, n):
        return jnp.where(n >= nb, 3, jnp.where(n == 0, 0, jnp.where(n == nb - 1, 2, 1)))

    return pl.pallas_call(
        _attn_kernel,
        out_shape=jax.ShapeDtypeStruct((batch * (nb + cpb) * tq, qw), BF16),
        grid=(batch, nb + cpb),
        in_specs=[
            pl.BlockSpec(memory_space=pltpu.SMEM),
            pl.BlockSpec((tq, qw), lambda b, n: (own(b, n), Z_CQ // qw)),
            zspec(prev, Z_CK), zspec(cur, Z_CK), zspec(nxt, Z_CK),
            pl.BlockSpec((ctx_len, kvw), lambda b, n: (ctx_base + b, Z_CK // kvw)),
            zspec(prev, Z_CV), zspec(cur, Z_CV), zspec(nxt, Z_CV),
            pl.BlockSpec((ctx_len, kvw), lambda b, n: (ctx_base + b, Z_CV // kvw)),
            pl.BlockSpec((None, tq, 3 * tq), lambda b, n: (bias_case(b, n), 0, 0)),
        ],
        out_specs=pl.BlockSpec((tq, qw), lambda b, n: (own(b, n), 0)),
        scratch_shapes=[pltpu.VMEM((n_heads * tq, 3 * tq + ctx_len), F32)],
        compiler_params=_cparams(("parallel", "parallel")),
        name="window_attn",
    )(sink, z, z, z, z, z, z, z, z, z, _window_bias(tq))


MERGE_CHUNKS = 4


def _merge_kernel(x_ref, m_ref, a_ref, of_ref, ob_ref, og_ref, c_ref, ga_ref, gb_ref, gc_ref,
                  bg_ref, wa_ref, wb_ref, wc_ref, wo_ref, o_ref):
    heads = of_ref.shape[1] // B_DV
    o = of_ref[...] + ob_ref[...]
    bn = jnp.concatenate([_rms_head(o[:, h * B_DV:(h + 1) * B_DV], bg_ref[...]) for h in range(heads)], axis=1)
    b = (bn * _silu(og_ref[...].astype(F32))).astype(BF16)
    a, c = a_ref[...], c_ref[...]
    d = o_ref.shape[1]
    cw = d // MERGE_CHUNKS

    def merged_cols(k):
        cols = slice(k * cw, (k + 1) * cw)
        return (jax.nn.sigmoid(ga_ref[:, cols].astype(F32)) * _dot(a, wa_ref[:, cols])
                + jax.nn.sigmoid(gb_ref[:, cols].astype(F32)) * _dot(b, wb_ref[:, cols])
                + jax.nn.sigmoid(gc_ref[:, cols].astype(F32)) * _dot(c, wc_ref[:, cols])).astype(BF16)

    pending = merged_cols(0)
    mix = None
    for k in range(MERGE_CHUNKS):
        ready = pending
        if k + 1 < MERGE_CHUNKS:
            pending = merged_cols(k + 1)
        part = _dot(ready, wo_ref[k * cw:(k + 1) * cw, :])
        mix = part if mix is None else mix + part
    o_ref[...] = x_ref[...] + m_ref[2] * mix


def _merge(x, n_rows, mod, layer, row_fn, z, a, o_f, o_b, c, b_norm_g, wa, wb, wc, wo, tm):
    d = x.shape[1]
    aw, bw, cw = a.shape[1], o_f.shape[1], c.shape[1]

    def const(rows):
        return pl.BlockSpec((None, rows, d), lambda i: (layer, 0, 0), pipeline_mode=pl.Buffered(1))

    return pl.pallas_call(
        _merge_kernel,
        out_shape=jax.ShapeDtypeStruct((n_rows, d), F32),
        grid=(n_rows // tm,),
        in_specs=[
            pl.BlockSpec((tm, d), lambda i: (i, 0)),
            _mod_spec(layer, 1, row_fn, d),
            pl.BlockSpec((tm, aw), lambda i: (i, 0)),
            pl.BlockSpec((tm, bw), lambda i: (i, 0)),
            pl.BlockSpec((tm, bw), lambda i: (i, 0)),
            pl.BlockSpec((tm, bw), lambda i: (i, Z_BG // bw)),
            pl.BlockSpec((tm, cw), lambda i: (i, 0)),
            pl.BlockSpec((tm, d), lambda i: (i, Z_GATE // d)),
            pl.BlockSpec((tm, d), lambda i: (i, Z_GATE // d + 1)),
            pl.BlockSpec((tm, d), lambda i: (i, Z_GATE // d + 2)),
            pl.BlockSpec((1, B_DV), lambda i: (0, 0)),
            const(aw), const(bw), const(cw), const(d),
        ],
        out_specs=pl.BlockSpec((tm, d), lambda i: (i, 0)),
        compiler_params=_cparams(("parallel",)),
        name="merge",
    )(x, mod, a, o_f, o_b, z, c, z, z, z, b_norm_g.reshape(1, B_DV), wa, wb, wc, wo)


def _rope_tables(seq, pad_rows):
    half = HEAD_DIM // 4
    t = np.arange(seq)
    pos = np.stack([t // GRID_W, t % GRID_W], axis=1).astype(np.float32)
    inv_freq = jnp.asarray(ROPE_BASE, F32) ** (-jnp.arange(half, dtype=F32) / half)
    ang = jnp.asarray(pos)[:, :, None] * inv_freq[None, None, :]
    cos = jnp.cos(ang)
    sin = jnp.sin(ang)
    cos_t = jnp.concatenate([cos, cos], axis=-1).reshape(seq, HEAD_DIM)
    sin_t = jnp.concatenate([-sin, sin], axis=-1).reshape(seq, HEAD_DIM)
    cos_t = jnp.concatenate([cos_t, jnp.ones((pad_rows, HEAD_DIM), F32)], axis=0)
    sin_t = jnp.concatenate([sin_t, jnp.zeros((pad_rows, HEAD_DIM), F32)], axis=0)
    return cos_t, sin_t


def _reorder_w_in(w, d):
    bkw, bvw, ckvw, aw, cqw = 256, 512, 256, 512, 1024
    o_bk = 0
    o_bv = o_bk + bkw
    o_ck = o_bv + bvw
    o_cv = o_ck + ckvw
    o_au = o_cv + ckvw
    o_av = o_au + aw
    o_bq = o_av + aw
    o_bg = o_bq + bkw
    o_cq = o_bg + bvw
    o_gate = o_cq + cqw
    seg = lambda s, width: w[..., s:s + width].astype(BF16)
    return jnp.concatenate([seg(o_cq, cqw), seg(o_ck, ckvw), seg(o_cv, ckvw), seg(o_bk, bkw), seg(o_bq, bkw), seg(o_bv, bvw),
                            seg(o_au, aw), seg(o_av, aw), seg(o_bg, bvw), seg(o_gate, 3 * d)], axis=-1)


def kernel(x, c, ctx, c_ctx, w_ada, b_ada, norm_g, w_ffn_up, w_ffn_down, w_in, a_v_gain, a_ws, a_bs,
           b_decay_w1, b_decay_w2, b_decay_b, b_norm_g, c_q_gain, c_k_gain, c_sink,
           w_br_a, w_br_b, w_br_c, w_out):
    batch, seq, d = x.shape
    ctx_len = ctx.shape[1]
    depth = w_ada.shape[0]
    assert d == 2048 and batch < MOD_ROWS and seq % 256 == 0 and ctx_len == 256
    n_lat, n_ctx = batch * seq, batch * ctx_len
    n_all = n_lat + n_ctx
    for t in (TM_FFN, TM_IN, TM_MERGE, TG_GMLP, TB_GLA):
        assert seq % t == 0 and n_ctx % t == 0

    def row_fn(t):
        return lambda i: jnp.minimum(i * t // seq, batch)

    c8 = jnp.zeros((MOD_ROWS, d), F32).at[:batch].set(c).at[batch].set(c_ctx)
    mod = _with_norm_gains(_mod_table(c8, w_ada, b_ada), norm_g)
    cos_t, sin_t = _rope_tables(seq, TM_IN)

    ffn = w_ffn_down.shape[2]
    w_up_f32 = w_ffn_up.reshape(depth * 2 * d, 2 * ffn)
    w_dn_f32 = w_ffn_down.reshape(depth * 2 * ffn, d)
    first_casts = [(w_up_f32, "cols", d, d), (w_dn_f32, "rows", ffn, ffn)]
    if depth > 1:
        first_casts += [(w_up_f32, "cols", 2 * d, (depth - 1) * 2 * d),
                        (w_dn_f32, "rows", 2 * ffn, (depth - 1) * 2 * ffn)]
    ffn_w = {(0, 0): (w_ffn_up[0, 0].astype(BF16), w_ffn_down[0, 0].astype(BF16), 0)}
    w_in_r = _reorder_w_in(w_in, d)
    w1 = jnp.concatenate([b_decay_w1[:, 0], b_decay_w1[:, 1],
                          jnp.zeros((depth, d, LANES - 2 * B_RANK), F32)], axis=-1).astype(BF16)
    kw = b_decay_w2.shape[-1]
    w2pad = jnp.zeros((depth, 2, LANES, kw), F32).at[:, 0, :B_RANK].set(b_decay_w2[:, 0]).at[
        :, 1, B_RANK:2 * B_RANK].set(b_decay_w2[:, 1])
    n_qh = c_sink.shape[1]
    qk_gain = jnp.concatenate([jnp.tile(c_q_gain * (HEAD_DIM ** -0.5 * LOG2E), (1, n_qh)),
                               jnp.tile(c_k_gain, (1, C_KVW // HEAD_DIM))], axis=1)
    wa, wb, wc, wo = (w.astype(BF16) for w in (w_br_a, w_br_b, w_br_c, w_out))

    srcs = [x.reshape(n_lat, d), ctx.reshape(n_ctx, d)]
    for l in range(depth):
        last = l == depth - 1
        n_mix = n_lat if last else n_all
        tm_first = TM_FFN if len(srcs) == 1 else TM_FFN // 2
        w_up, w_dn, wsel = ffn_w[(l, 0)]
        xs = _ffn(srcs, n_all, mod, l, 0, row_fn(tm_first), w_up, w_dn, wsel, tm_first, TF,
                  cast=first_casts if l == 0 else ())
        if l == 0:
            xs, *cast_out = xs
            ffn_w[(0, 1)] = (cast_out[0], cast_out[1], 0)
            for later in range(1, depth):
                for which in range(2):
                    ffn_w[(later, which)] = (cast_out[2], cast_out[3], 2 * (later - 1) + which)
        z, r = _inproj(xs, mod, l, row_fn(TM_IN), w_in_r, w1, qk_gain[l:l + 1], cos_t, sin_t,
                       n_lat, seq, TM_IN, TN_IN, ctx_state_only=last)
        a_out = _gmlp(z, n_mix, a_v_gain[l], a_ws[l], a_bs[l], TG_GMLP)
        o_f, o_b = _gla(z, r, w2pad[l], b_decay_b[l].reshape(2, 1, kw), batch, seq, ctx_len, TB_GLA)
        c_out = _attn(z, c_sink[l], batch, seq, ctx_len, with_ctx=not last)
        xs = _merge(xs, n_mix, mod, l, row_fn(TM_MERGE), z, a_out, o_f, o_b, c_out, b_norm_g[l],
                    wa, wb, wc, wo, TM_MERGE)
        w_up, w_dn, wsel = ffn_w[(l, 1)]
        xs = _ffn([xs], n_mix, mod, l, 1, row_fn(TM_FFN), w_up, w_dn, wsel, TM_FFN, TF)
        srcs = [xs]
    return xs.reshape(batch, seq, d)
```

```python
import functools

import jax
import jax.numpy as jnp
import numpy as np
from jax import lax
from jax.experimental import pallas as pl
from jax.experimental.pallas import tpu as pltpu

F32 = jnp.float32
BF16 = jnp.bfloat16

HEAD_DIM = 128
EPS = 1e-6
N_MOD = 9
GRID_W = 64
A_CHUNK = 128
B_DK = 64
B_DV = 128
B_RANK = 16
B_TAU = 16.0
B_CHUNK = 64
C_BLOCK = 128
ROPE_BASE = 10000.0
MOD_ROWS = 8
LANES = 128
NEG_BIG = -1e30
LOG2E = 1.4426950408889634
LN2 = 0.6931471805599453
VMEM_LIMIT = 56 * 1024 * 1024

Z_CQ, Z_CK, Z_CV, Z_BK, Z_BQ, Z_BV, Z_AU, Z_AV, Z_BG, Z_GATE = (
    0, 1024, 1280, 1536, 1792, 2048, 2560, 3072, 3584, 4096)
QK_COLS = Z_CV
C_KVW = Z_CV - Z_CK

TM_FFN = 1024
TF = 512
TM_IN = 512
TN_IN = 2 * QK_COLS
TM_MERGE = 256
TG_GMLP = 512
TB_GLA = 256


def _cparams(sem):
    return pltpu.CompilerParams(dimension_semantics=sem, vmem_limit_bytes=VMEM_LIMIT)


def _dot(a, b):
    return jnp.dot(a, b, preferred_element_type=F32)


def _dot_nt(a, b):
    return lax.dot_general(a, b, (((1,), (1,)), ((), ())), preferred_element_type=F32)


def _dot_tn(a, b):
    return lax.dot_general(a, b, (((0,), (0,)), ((), ())), preferred_element_type=F32)


def _silu(x):
    return x * jax.nn.sigmoid(x)


NORM_ROWS = 64


def _rms_mod_to(x_ref, m_ref, h_ref):
    shift = m_ref[0]
    gain = m_ref[3] * (1.0 + m_ref[1])

    def one_pass(c, carry):
        rows = pl.ds(pl.multiple_of(c * NORM_ROWS, NORM_ROWS), NORM_ROWS)
        x = x_ref[rows, :]
        rs = lax.rsqrt(jnp.mean(x * x, axis=-1, keepdims=True) + EPS)
        h_ref[rows, :] = (x * rs * gain + shift).astype(BF16)
        return carry

    lax.fori_loop(0, x_ref.shape[0] // NORM_ROWS, one_pass, 0, unroll=2)


def _rms_head(xh, gain):
    ms = jnp.mean(xh * xh, axis=-1, keepdims=True)
    return xh * lax.rsqrt(ms + EPS) * gain


def _mod_kernel(c_ref, w_ref, b_ref, o_ref):
    act = _silu(c_ref[...])
    o_ref[...] = _dot(act.astype(BF16), w_ref[...].astype(BF16)) + b_ref[...]


def _mod_table(c8, w_ada, b_ada):
    depth, d, nd = w_ada.shape
    tn = 1024
    nj = d // tn
    out = pl.pallas_call(
        _mod_kernel,
        out_shape=jax.ShapeDtypeStruct((depth, N_MOD, MOD_ROWS, d), F32),
        grid=(depth, N_MOD, nj),
        in_specs=[
            pl.BlockSpec((MOD_ROWS, d), lambda l, k, j: (0, 0)),
            pl.BlockSpec((None, d, tn), lambda l, k, j: (l, 0, k * nj + j)),
            pl.BlockSpec((None, 1, tn), lambda l, k, j: (l, 0, k * nj + j)),
        ],
        out_specs=pl.BlockSpec((None, None, MOD_ROWS, tn), lambda l, k, j: (l, k, 0, j)),
        compiler_params=_cparams(("parallel", "parallel", "parallel")),
        name="mod_table",
    )(c8, w_ada, b_ada.reshape(depth, 1, nd))
    return out


MOD_GROUP = 4


def _with_norm_gains(mod, norm_g):
    depth, _, rows, d = mod.shape
    subs = norm_g.shape[1]
    gains = jnp.broadcast_to(norm_g[:, :, None, None, :], (depth, subs, 1, rows, d))
    table = jnp.concatenate([mod.reshape(depth, subs, N_MOD // subs, rows, d), gains], axis=2)
    return table.reshape(depth, subs * MOD_GROUP, rows, 1, d)


def _mod_spec(layer, sub, row_fn, d):
    return pl.BlockSpec((None, MOD_GROUP, None, 1, d), lambda i, *_: (layer, sub, row_fn(i), 0, 0))


def _ffn_kernel(*refs, tiles_per_source, n_cast):
    n_src = len(tiles_per_source)
    x_refs = refs[:n_src]
    m_ref, wg_ref, wv_ref, wd_ref = refs[n_src:n_src + 4]
    cast_in_refs = refs[n_src + 4:n_src + 4 + n_cast]
    o_ref = refs[n_src + 4 + n_cast]
    cast_out_refs = refs[n_src + 5 + n_cast:n_src + 5 + 2 * n_cast]
    h_ref = refs[n_src + 5 + 2 * n_cast]
    acc_ref = o_ref
    i, f = pl.program_id(0), pl.program_id(1)

    for cast_in_ref, cast_out_ref in zip(cast_in_refs, cast_out_refs):
        cast_out_ref[...] = cast_in_ref[...].astype(BF16)

    def with_own_source(fn):
        start = 0
        for x_ref, n_tiles in zip(x_refs, tiles_per_source):
            if n_src == 1:
                fn(x_ref)
            else:
                pl.when(jnp.logical_and(i >= start, i < start + n_tiles))(lambda x_ref=x_ref: fn(x_ref))
            start += n_tiles

    last = pl.num_programs(1) - 1

    def prologue(x_ref):
        _rms_mod_to(x_ref, m_ref, h_ref)

    def chunk():
        h = h_ref[...]
        act = (_silu(_dot(h, wg_ref[...])) * _dot(h, wv_ref[...])).astype(BF16)
        return _dot(act, wd_ref[...])

    @pl.when(f == 0)
    def _():
        with_own_source(prologue)
        acc_ref[...] = chunk()

    if n_src == 1:
        @pl.when(jnp.logical_and(f > 0, f < last))
        def _():
            acc_ref[...] += chunk()

        @pl.when(f == last)
        def _():
            o_ref[...] = x_refs[0][...] + 0.5 * m_ref[2] * (acc_ref[...] + chunk())
    else:
        @pl.when(f > 0)
        def _():
            acc_ref[...] += chunk()

        def epilogue(x_ref):
            o_ref[...] = x_ref[...] + 0.5 * m_ref[2] * acc_ref[...]

        @pl.when(f == last)
        def _():
            with_own_source(epilogue)


def _ffn(xs, n_rows, mod, layer, which, row_fn, w_up, w_down, wsel, tm, tf, cast=()):
    d = xs[0].shape[1]
    nf = w_up.shape[1] // (2 * tf)
    nt = n_rows // tm
    tiles, starts = [], []
    for x in xs:
        starts.append(sum(tiles))
        tiles.append(min(x.shape[0] // tm, nt - sum(tiles)))

    def src_spec(start, n_tiles):
        mode = pl.Buffered(1) if (n_tiles == 1 and len(xs) > 1) else None
        return pl.BlockSpec((tm, d), lambda i, f: (jnp.clip(i - start, 0, n_tiles - 1), 0), pipeline_mode=mode)

    in_specs = [src_spec(s, t) for s, t in zip(starts, tiles)] + [
        _mod_spec(layer, 2 * which, row_fn, d),
        pl.BlockSpec((d, tf), lambda i, f: (wsel, f)),
        pl.BlockSpec((d, tf), lambda i, f: (wsel, nf + f)),
        pl.BlockSpec((tf, d), lambda i, f: (wsel * nf + f, 0)),
    ]
    out_shape = [jax.ShapeDtypeStruct((n_rows, d), F32)]
    out_specs = [pl.BlockSpec((tm, d), lambda i, f: (i, 0))]
    operands = [*xs, mod, w_up, w_up, w_down]
    row_blocks = 1 << (nt.bit_length() - 1)
    cast_in_specs = []
    for src, split, row0, rows in cast:
        cols = src.shape[1]
        if split == "cols":
            br, bc = rows // row_blocks, cols // nf
            first = row0 // br
            in_blk = lambda i, f, first=first: (first + jnp.minimum(i, row_blocks - 1), f)
            out_blk = lambda i, f: (i, f)
        else:
            br, bc = rows // (row_blocks * nf), cols
            first = row0 // br
            in_blk = lambda i, f, first=first: (first + jnp.minimum(i, row_blocks - 1) * nf + f, 0)
            out_blk = lambda i, f: (i * nf + f, 0)
        assert br % 16 == 0 and bc % LANES == 0 and rows % br == 0 and cols % bc == 0 and row0 % br == 0
        cast_in_specs.append(pl.BlockSpec((br, bc), in_blk))
        out_shape.append(jax.ShapeDtypeStruct((rows // row_blocks * nt, cols), BF16))
        out_specs.append(pl.BlockSpec((br, bc), out_blk))
        operands.append(src)

    outs = pl.pallas_call(
        functools.partial(_ffn_kernel, tiles_per_source=tuple(tiles), n_cast=len(cast)),
        out_shape=out_shape,
        grid=(nt, nf),
        in_specs=in_specs + cast_in_specs,
        out_specs=out_specs,
        scratch_shapes=[pltpu.VMEM((tm, d), BF16)],
        compiler_params=_cparams(("parallel", "arbitrary")),
        name="ffn",
    )(*operands)
    return outs[0] if not cast else tuple(outs)


def _inproj_kernel(x_ref, m_ref, w_ref, w1_ref, qkg_ref, cos_ref, sin_ref, *rest, full_tiles, has_cast):
    if has_cast:
        cast_in_ref, z_ref, r_ref, cast_out_ref, h_ref = rest
        cast_out_ref[...] = cast_in_ref[...].astype(BF16)
    else:
        z_ref, r_ref, h_ref = rest
    i, n = pl.program_id(0), pl.program_id(1)

    @pl.when(jnp.logical_and(n != 0, i < full_tiles))
    def _():
        z_ref[...] = _dot(h_ref[...], w_ref[...]).astype(z_ref.dtype)

    @pl.when(jnp.logical_and(n != 0, i >= full_tiles))
    def _():
        z_ref[...] = jnp.zeros_like(z_ref)

    @pl.when(n == 0)
    def _():
        _rms_mod_to(x_ref, m_ref, h_ref)
        h = h_ref[...]
        r_ref[...] = _dot(h, w1_ref[...])
        zt = _dot(h, w_ref[:, :QK_COLS])
        if w_ref.shape[1] > QK_COLS:
            z_ref[:, QK_COLS:] = _dot(h, w_ref[:, QK_COLS:]).astype(z_ref.dtype)
        pair = 2 * HEAD_DIM
        r = lax.broadcasted_iota(jnp.int32, (pair, pair), 0)
        c = lax.broadcasted_iota(jnp.int32, (pair, pair), 1)
        head_sum = jnp.where(jnp.bitwise_and(jnp.bitwise_xor(r, c), HEAD_DIM) == 0, 1.0, 0.0).astype(BF16)
        swap = jnp.where(c == jnp.bitwise_xor(r, HEAD_DIM // 4), 1.0, 0.0).astype(BF16)
        cos = jnp.concatenate([cos_ref[...]] * 2, axis=1)
        sin = jnp.concatenate([sin_ref[...]] * 2, axis=1)
        for p in range(QK_COLS // pair):
            cols = slice(p * pair, (p + 1) * pair)
            y = zt[:, cols]
            ms = _dot((y * y).astype(BF16), head_sum) * (1.0 / HEAD_DIM)
            y = y * lax.rsqrt(ms + EPS) * qkg_ref[:, cols]
            z_ref[:, cols] = (y * cos + _dot(y.astype(BF16), swap) * sin).astype(z_ref.dtype)


def _inproj(x, mod, layer, row_fn, w_in, w1, qk_gain, cos_t, sin_t, n_lat, seq, tm, tn, ctx_state_only,
            cast=None):
    n_rows, d = x.shape
    ncols = w_in.shape[1]
    assert tn % QK_COLS == 0 and ncols % tn == 0 and Z_BV + 512 <= tn
    rope_blocks = seq // tm
    nt = n_rows // tm
    ncol = ncols // tn
    full_tiles = n_lat // tm if ctx_state_only else nt

    def rope_blk(i, n):
        return jnp.where(i * tm < n_lat, (i % rope_blocks), rope_blocks)

    def w_blk(i, n):
        return jnp.where(i < full_tiles, n, 0)

    in_specs = [
        pl.BlockSpec((tm, d), lambda i, n: (i, 0)),
        _mod_spec(layer, 1, row_fn, d),
        pl.BlockSpec((d, tn), lambda i, n: (0, w_blk(i, n))),
        pl.BlockSpec((None, d, LANES), lambda i, n: (layer, 0, 0)),
        pl.BlockSpec((1, QK_COLS), lambda i, n: (0, 0)),
        pl.BlockSpec((tm, HEAD_DIM), lambda i, n: (rope_blk(i, n), 0)),
        pl.BlockSpec((tm, HEAD_DIM), lambda i, n: (rope_blk(i, n), 0)),
    ]
    out_shape = [jax.ShapeDtypeStruct((n_rows, ncols), BF16), jax.ShapeDtypeStruct((n_rows, LANES), F32)]
    out_specs = [pl.BlockSpec((tm, tn), lambda i, n: (i, n)), pl.BlockSpec((tm, LANES), lambda i, n: (i, 0))]
    operands = [x, mod, w_in, w1, qk_gain, cos_t, sin_t]
    if cast is not None:
        src, row0, rows = cast
        row_blocks = 1 << (nt.bit_length() - 1)
        br = rows // row_blocks
        assert br % 16 == 0 and rows % br == 0 and row0 % br == 0 and src.shape[1] == ncols
        first = row0 // br
        in_specs.append(pl.BlockSpec((br, tn), lambda i, n: (first + jnp.minimum(i, row_blocks - 1), n)))
        out_shape.append(jax.ShapeDtypeStruct((nt * br, ncols), BF16))
        out_specs.append(pl.BlockSpec((br, tn), lambda i, n: (i, n)))
        operands.append(src)

    return pl.pallas_call(
        functools.partial(_inproj_kernel, full_tiles=full_tiles, has_cast=cast is not None),
        out_shape=out_shape,
        grid=(nt, ncol),
        in_specs=in_specs,
        out_specs=out_specs,
        scratch_shapes=[pltpu.VMEM((tm, d), BF16)],
        compiler_params=_cparams(("parallel", "arbitrary")),
        name="inproj",
    )(*operands)


def _gmlp_kernel(u_ref, v_ref, gain_ref, ws_ref, bs_ref, o_ref):
    tg, width = u_ref.shape
    groups = width // HEAD_DIM
    u = jax.nn.gelu(u_ref[...].astype(F32))
    v = jax.nn.gelu(v_ref[...].astype(F32))
    mu = jnp.mean(v, axis=-1, keepdims=True)
    vc = v - mu
    var = jnp.mean(vc * vc, axis=-1, keepdims=True)
    vn = (vc * lax.rsqrt(var + EPS) * gain_ref[...]).astype(BF16)
    nchunk = tg // A_CHUNK
    for g in range(groups):
        cols = slice(g * HEAD_DIM, (g + 1) * HEAD_DIM)
        v_g = jnp.concatenate([vn[c * A_CHUNK:(c + 1) * A_CHUNK, cols] for c in range(nchunk)], axis=1)
        mixed = _dot(ws_ref[g], v_g)
        for c in range(nchunk):
            rows = slice(c * A_CHUNK, (c + 1) * A_CHUNK)
            o_ref[rows, cols] = (u[rows, cols] * (mixed[:, c * HEAD_DIM:(c + 1) * HEAD_DIM] + bs_ref[g])
                                 ).astype(o_ref.dtype)


def _gmlp(z, n_rows, a_v_gain, a_ws, a_bs, tg):
    groups = a_ws.shape[0]
    width = groups * HEAD_DIM
    bs_full = jnp.broadcast_to(a_bs[:, :, None], (groups, A_CHUNK, HEAD_DIM)).astype(F32)
    return pl.pallas_call(
        _gmlp_kernel,
        out_shape=jax.ShapeDtypeStruct((n_rows, width), BF16),
        grid=(n_rows // tg,),
        in_specs=[
            pl.BlockSpec((tg, width), lambda i: (i, Z_AU // width)),
            pl.BlockSpec((tg, width), lambda i: (i, Z_AV // width)),
            pl.BlockSpec((1, width), lambda i: (0, 0)),
            pl.BlockSpec((groups, A_CHUNK, A_CHUNK), lambda i: (0, 0, 0)),
            pl.BlockSpec((groups, A_CHUNK, HEAD_DIM), lambda i: (0, 0, 0)),
        ],
        out_specs=pl.BlockSpec((tg, width), lambda i: (i, 0)),
        compiler_params=_cparams(("parallel",)),
        name="gmlp",
    )(z, z, a_v_gain.reshape(1, width), a_ws.astype(BF16), bs_full)


def _gla_prep(k_ref, q_ref, r_ref, w2, bias, tri_ref, reverse):
    tb = k_ref.shape[0]
    heads = k_ref.shape[1] // B_DK
    nchunk = tb // B_CHUNK
    logit = _dot(r_ref[...].astype(BF16), w2) + bias
    soft = jnp.log2(1.0 + jnp.exp2(jnp.abs(logit) * -LOG2E))
    g = jnp.minimum(logit, 0.0) * (1.0 / B_TAU) - soft * (LN2 / B_TAU)
    g_hi = g.astype(BF16)
    g_lo = (g - g_hi.astype(F32)).astype(BF16)
    tri = tri_ref[...]
    bc = _dot(tri, g_hi) + _dot(tri, g_lo)
    last = [c * B_CHUNK + (0 if reverse else B_CHUNK - 1) for c in range(nchunk)]
    tots = [bc[r:r + 1, :] for r in last]
    tot_rows = jnp.concatenate([jnp.broadcast_to(t, (B_CHUNK, t.shape[1])) for t in tots], axis=0)
    kf = k_ref[...].astype(F32)
    q_in = (q_ref[...].astype(F32) * jnp.exp(bc) * (B_DK ** -0.5)).astype(BF16)
    k_in = kf * jnp.exp(-bc)
    k_out = kf * jnp.exp(tot_rows - bc)
    dec = [jnp.exp(t) for t in tots]
    lane = lax.broadcasted_iota(jnp.int32, (tb, LANES), 1)
    zeros = jnp.zeros((B_CHUNK, LANES), BF16)
    q_slabs, k_in_heads, k4_heads = [], [], []
    for h in range(heads):
        slab = slice((h // 2) * LANES, (h // 2 + 1) * LANES)
        mine = (lane >= B_DK) if h % 2 else (lane < B_DK)
        q_slabs.append(q_in[:, slab])
        k_in_heads.append(jnp.where(mine, k_in[:, slab], 0.0).astype(BF16))
        k_out_h = jnp.where(mine, k_out[:, slab], 0.0).astype(BF16)
        k4_heads.append(jnp.concatenate([
            jnp.concatenate([zeros] * c + [k_out_h[c * B_CHUNK:(c + 1) * B_CHUNK]] + [zeros] * (nchunk - 1 - c),
                            axis=0) for c in range(nchunk)], axis=1))
    return q_slabs, k_in_heads, k4_heads, dec


def _gla_kernel(kf_ref, vf_ref, qf_ref, rf_ref, kb_ref, vb_ref, qb_ref, rb_ref, w2_ref, b_ref,
                tri_ref, keep_ref, of_ref, ob_ref, sf_ref, sb_ref, a_ref, u_ref):
    @pl.when(pl.program_id(1) == 0)
    def _():
        sf_ref[...] = jnp.zeros_like(sf_ref)
        sb_ref[...] = jnp.zeros_like(sb_ref)

    heads = sf_ref.shape[0]
    nchunk = kf_ref.shape[0] // B_CHUNK
    v_refs, o_refs, st_refs = (vf_ref, vb_ref), (of_ref, ob_ref), (sf_ref, sb_ref)
    prep = [_gla_prep(kf_ref, qf_ref, rf_ref, w2_ref[0], b_ref[0], tri_ref.at[0], False),
            _gla_prep(kb_ref, qb_ref, rb_ref, w2_ref[1], b_ref[1], tri_ref.at[1], True)]
    jobs = [(d, h) for d in range(2) for h in range(heads)]
    for d, h in jobs:
        q_slabs, k_in_heads, _, _ = prep[d]
        a_ref[d * heads + h] = _dot_nt(q_slabs[h], k_in_heads[h])
    for d, h in jobs:
        v_h = v_refs[d][:, h * B_DV:(h + 1) * B_DV]
        u_ref[d * heads + h] = _dot_tn(v_h, prep[d][2][h])
    for d, h in jobs:
        vcols = slice(h * B_DV, (h + 1) * B_DV)
        attn = jnp.where(keep_ref[d] > 0.5, a_ref[d * heads + h], 0.0).astype(BF16)
        o_refs[d][:, vcols] = _dot(attn, v_refs[d][:, vcols])
    for d, h in jobs:
        q_s, dec = prep[d][0][h], prep[d][3]
        slab = slice((h // 2) * LANES, (h // 2 + 1) * LANES)
        vcols = slice(h * B_DV, (h + 1) * B_DV)
        st = st_refs[d][h]
        for c in (range(nchunk - 1, -1, -1) if d else range(nchunk)):
            rows = slice(c * B_CHUNK, (c + 1) * B_CHUNK)
            o_refs[d][rows, vcols] += _dot_nt(q_s[rows], st.astype(BF16))
            st = st * dec[c][:, slab] + u_ref[d * heads + h, :, c * LANES:(c + 1) * LANES]
        st_refs[d][h] = st


def _chunk_triangles(tb):
    t = np.arange(tb)[:, None]
    s = np.arange(tb)[None, :]
    same = (t // B_CHUNK) == (s // B_CHUNK)
    return np.stack([same & (s <= t), same & (s >= t)]).astype(np.float32)


def _gla(z, r, w2pad, bias, batch, seq, ctx_len, tb):
    n_rows = z.shape[0]
    heads = bias.shape[-1] // B_DK
    assert heads % 2 == 0 and 2 * B_DK == LANES
    kw, vw = heads * B_DK, heads * B_DV
    nlat, nctx = seq // tb, ctx_len // tb
    base = batch * nlat
    tri = _chunk_triangles(tb)

    def fwd(b, j):
        return jnp.where(j < nctx, base + b * nctx + j, b * nlat + (j - nctx))

    def bwd(b, j):
        return jnp.where(j < nctx, base + b * nctx + (nctx - 1 - j), b * nlat + (nlat - 1 - (j - nctx)))

    def specs(blk):
        return [pl.BlockSpec((tb, kw), lambda b, j: (blk(b, j), Z_BK // kw)),
                pl.BlockSpec((tb, vw), lambda b, j: (blk(b, j), Z_BV // vw)),
                pl.BlockSpec((tb, kw), lambda b, j: (blk(b, j), Z_BQ // kw)),
                pl.BlockSpec((tb, LANES), lambda b, j: (blk(b, j), 0))]

    return pl.pallas_call(
        _gla_kernel,
        out_shape=(jax.ShapeDtypeStruct((n_rows, vw), F32), jax.ShapeDtypeStruct((n_rows, vw), F32)),
        grid=(batch, nctx + nlat),
        in_specs=specs(fwd) + specs(bwd) + [
            pl.BlockSpec((2, LANES, kw), lambda b, j: (0, 0, 0)),
            pl.BlockSpec((2, 1, kw), lambda b, j: (0, 0, 0)),
            pl.BlockSpec((2, tb, tb), lambda b, j: (0, 0, 0)),
            pl.BlockSpec((2, tb, tb), lambda b, j: (0, 0, 0)),
        ],
        out_specs=(pl.BlockSpec((tb, vw), lambda b, j: (fwd(b, j), 0)),
                   pl.BlockSpec((tb, vw), lambda b, j: (bwd(b, j), 0))),
        scratch_shapes=[pltpu.VMEM((heads, B_DV, LANES), F32), pltpu.VMEM((heads, B_DV, LANES), F32),
                        pltpu.VMEM((2 * heads, tb, tb), F32),
                        pltpu.VMEM((2 * heads, B_DV, (tb // B_CHUNK) * LANES), F32)],
        compiler_params=_cparams(("parallel", "arbitrary")),
        name="gla",
    )(z, z, z, r, z, z, z, r, w2pad.astype(BF16), bias, jnp.asarray(tri, BF16), jnp.asarray(tri, F32))


def _attn_kernel(sink_ref, q_ref, kp_ref, kc_ref, kn_ref, kx_ref, vp_ref, vc_ref, vn_ref, vx_ref,
                 bias_ref, o_ref, s_ref):
    tq = q_ref.shape[0]
    n_heads = q_ref.shape[1] // HEAD_DIM
    kv_heads = kc_ref.shape[1] // HEAD_DIM
    group = n_heads // kv_heads
    n_loc = bias_ref.shape[1]

    def kv_cols(h):
        kh = h // group
        return slice(kh * HEAD_DIM, (kh + 1) * HEAD_DIM)

    def scores(kh):
        cols = kv_cols(kh * group)
        k_all = jnp.concatenate([kp_ref[:, cols], kc_ref[:, cols], kn_ref[:, cols], kx_ref[:, cols]], axis=0)
        q = jnp.concatenate([q_ref[:, h * HEAD_DIM:(h + 1) * HEAD_DIM]
                             for h in range(kh * group, (kh + 1) * group)], axis=0)
        s_ref[kh * group * tq:(kh + 1) * group * tq, :] = _dot_nt(q, k_all)

    def softmax_pv(h):
        cols = kv_cols(h)
        rows = slice(h * tq, (h + 1) * tq)
        v_all = jnp.concatenate([vp_ref[:, cols], vc_ref[:, cols], vn_ref[:, cols], vx_ref[:, cols]], axis=0)
        sink = sink_ref[h] * LOG2E
        sg = jnp.concatenate([s_ref[rows, :n_loc] + bias_ref[...], s_ref[rows, n_loc:]], axis=1)
        m = jnp.maximum(jnp.max(sg, axis=-1, keepdims=True), sink)
        e = jnp.exp2(sg - m)
        den = jnp.sum(e, axis=-1, keepdims=True) + jnp.exp2(sink - m)
        o_ref[:, h * HEAD_DIM:(h + 1) * HEAD_DIM] = (_dot(e.astype(BF16), v_all) / den).astype(o_ref.dtype)

    for kh in range(kv_heads):
        scores(kh)
    for h in range(n_heads):
        softmax_pv(h)


def _window_bias(tq):
    i = np.arange(tq)[:, None]
    j = np.arange(3 * tq)[None, :]
    band = (j >= i) & (j <= i + 2 * tq)
    cases = [band & (j >= tq), band, band & (j < 2 * tq), np.zeros_like(band)]
    return jnp.asarray(np.where(np.stack(cases), 0.0, NEG_BIG), F32)


def _attn(z, sink, batch, seq, ctx_len, with_ctx):
    n_heads = sink.shape[0]
    qw = n_heads * HEAD_DIM
    kvw = C_KVW
    tq = C_BLOCK
    nb = seq // tq
    cpb = ctx_len // tq if with_ctx else 0
    assert nb >= 2 and ctx_len % tq == 0
    ctx_base = batch * seq // ctx_len

    def own(b, n):
        return jnp.where(n < nb, b * nb + n, batch * nb + b * cpb + (n - nb))

    def prev(b, n):
        return b * nb + jnp.clip(n - 1, 0, nb - 1)

    def cur(b, n):
        return b * nb + jnp.minimum(n, nb - 1)

    def nxt(b, n):
        return b * nb + jnp.minimum(n + 1, nb - 1)

    def zspec(blk, col0):
        return pl.BlockSpec((tq, kvw), lambda b, n: (blk(b, n), col0 // kvw))

    def bias_case(b, n):
        return jnp.where(n >= nb, 3, jnp.where(n == 0, 0, jnp.where(n == nb - 1, 2, 1)))

    return pl.pallas_call(
        _attn_kernel,
        out_shape=jax.ShapeDtypeStruct((batch * (nb + cpb) * tq, qw), BF16),
        grid=(batch, nb + cpb),
        in_specs=[
            pl.BlockSpec(memory_space=pltpu.SMEM),
            pl.BlockSpec((tq, qw), lambda b, n: (own(b, n), Z_CQ // qw)),
            zspec(prev, Z_CK), zspec(cur, Z_CK), zspec(nxt, Z_CK),
            pl.BlockSpec((ctx_len, kvw), lambda b, n: (ctx_base + b, Z_CK // kvw)),
            zspec(prev, Z_CV), zspec(cur, Z_CV), zspec(nxt, Z_CV),
            pl.BlockSpec((ctx_len, kvw), lambda b, n: (ctx_base + b, Z_CV // kvw)),
            pl.BlockSpec((None, tq, 3 * tq), lambda b, n: (bias_case(b, n), 0, 0)),
        ],
        out_specs=pl.BlockSpec((tq, qw), lambda b, n: (own(b, n), 0)),
        scratch_shapes=[pltpu.VMEM((n_heads * tq, 3 * tq + ctx_len), F32)],
        compiler_params=_cparams(("parallel", "parallel")),
        name="window_attn",
    )(sink, z, z, z, z, z, z, z, z, z, _window_bias(tq))


MERGE_CHUNKS = 4


def _merge_kernel(x_ref, m_ref, a_ref, of_ref, ob_ref, og_ref, c_ref, ga_ref, gb_ref, gc_ref,
                  bg_ref, wa_ref, wb_ref, wc_ref, wo_ref, o_ref):
    heads = of_ref.shape[1] // B_DV
    o = of_ref[...] + ob_ref[...]
    bn = jnp.concatenate([_rms_head(o[:, h * B_DV:(h + 1) * B_DV], bg_ref[...]) for h in range(heads)], axis=1)
    b = (bn * _silu(og_ref[...].astype(F32))).astype(BF16)
    a, c = a_ref[...], c_ref[...]
    d = o_ref.shape[1]
    cw = d // MERGE_CHUNKS

    def merged_cols(k):
        cols = slice(k * cw, (k + 1) * cw)
        return (jax.nn.sigmoid(ga_ref[:, cols].astype(F32)) * _dot(a, wa_ref[:, cols])
                + jax.nn.sigmoid(gb_ref[:, cols].astype(F32)) * _dot(b, wb_ref[:, cols])
                + jax.nn.sigmoid(gc_ref[:, cols].astype(F32)) * _dot(c, wc_ref[:, cols])).astype(BF16)

    pending = merged_cols(0)
    mix = None
    for k in range(MERGE_CHUNKS):
        ready = pending
        if k + 1 < MERGE_CHUNKS:
            pending = merged_cols(k + 1)
        part = _dot(ready, wo_ref[k * cw:(k + 1) * cw, :])
        mix = part if mix is None else mix + part
    o_ref[...] = x_ref[...] + m_ref[2] * mix


def _merge(x, n_rows, mod, layer, row_fn, z, a, o_f, o_b, c, b_norm_g, wa, wb, wc, wo, tm):
    d = x.shape[1]
    aw, bw, cw = a.shape[1], o_f.shape[1], c.shape[1]

    def const(rows):
        return pl.BlockSpec((None, rows, d), lambda i: (layer, 0, 0), pipeline_mode=pl.Buffered(1))

    return pl.pallas_call(
        _merge_kernel,
        out_shape=jax.ShapeDtypeStruct((n_rows, d), F32),
        grid=(n_rows // tm,),
        in_specs=[
            pl.BlockSpec((tm, d), lambda i: (i, 0)),
            _mod_spec(layer, 1, row_fn, d),
            pl.BlockSpec((tm, aw), lambda i: (i, 0)),
            pl.BlockSpec((tm, bw), lambda i: (i, 0)),
            pl.BlockSpec((tm, bw), lambda i: (i, 0)),
            pl.BlockSpec((tm, bw), lambda i: (i, Z_BG // bw)),
            pl.BlockSpec((tm, cw), lambda i: (i, 0)),
            pl.BlockSpec((tm, d), lambda i: (i, Z_GATE // d)),
            pl.BlockSpec((tm, d), lambda i: (i, Z_GATE // d + 1)),
            pl.BlockSpec((tm, d), lambda i: (i, Z_GATE // d + 2)),
            pl.BlockSpec((1, B_DV), lambda i: (0, 0)),
            const(aw), const(bw), const(cw), const(d),
        ],
        out_specs=pl.BlockSpec((tm, d), lambda i: (i, 0)),
        compiler_params=_cparams(("parallel",)),
        name="merge",
    )(x, mod, a, o_f, o_b, z, c, z, z, z, b_norm_g.reshape(1, B_DV), wa, wb, wc, wo)


def _rope_tables(seq, pad_rows):
    half = HEAD_DIM // 4
    t = np.arange(seq)
    pos = np.stack([t // GRID_W, t % GRID_W], axis=1).astype(np.float32)
    inv_freq = jnp.asarray(ROPE_BASE, F32) ** (-jnp.arange(half, dtype=F32) / half)
    ang = jnp.asarray(pos)[:, :, None] * inv_freq[None, None, :]
    cos = jnp.cos(ang)
    sin = jnp.sin(ang)
    cos_t = jnp.concatenate([cos, cos], axis=-1).reshape(seq, HEAD_DIM)
    sin_t = jnp.concatenate([-sin, sin], axis=-1).reshape(seq, HEAD_DIM)
    cos_t = jnp.concatenate([cos_t, jnp.ones((pad_rows, HEAD_DIM), F32)], axis=0)
    sin_t = jnp.concatenate([sin_t, jnp.zeros((pad_rows, HEAD_DIM), F32)], axis=0)
    return cos_t, sin_t


def _reorder_w_in(w, d):
    bkw, bvw, ckvw, aw, cqw = 256, 512, 256, 512, 1024
    o_bk = 0
    o_bv = o_bk + bkw
    o_ck = o_bv + bvw
    o_cv = o_ck + ckvw
    o_au = o_cv + ckvw
    o_av = o_au + aw
    o_bq = o_av + aw
    o_bg = o_bq + bkw
    o_cq = o_bg + bvw
    o_gate = o_cq + cqw
    seg = lambda s, width: w[..., s:s + width].astype(BF16)
    return jnp.concatenate([seg(o_cq, cqw), seg(o_ck, ckvw), seg(o_cv, ckvw), seg(o_bk, bkw), seg(o_bq, bkw), seg(o_bv, bvw),
                            seg(o_au, aw), seg(o_av, aw), seg(o_bg, bvw), seg(o_gate, 3 * d)], axis=-1)


def kernel(x, c, ctx, c_ctx, w_ada, b_ada, norm_g, w_ffn_up, w_ffn_down, w_in, a_v_gain, a_ws, a_bs,
           b_decay_w1, b_decay_w2, b_decay_b, b_norm_g, c_q_gain, c_k_gain, c_sink,
           w_br_a, w_br_b, w_br_c, w_out):
    batch, seq, d = x.shape
    ctx_len = ctx.shape[1]
    depth = w_ada.shape[0]
    assert d == 2048 and batch < MOD_ROWS and seq % 256 == 0 and ctx_len == 256
    n_lat, n_ctx = batch * seq, batch * ctx_len
    n_all = n_lat + n_ctx
    for t in (TM_FFN, TM_IN, TM_MERGE, TG_GMLP, TB_GLA):
        assert seq % t == 0 and n_ctx % t == 0

    def row_fn(t):
        return lambda i: jnp.minimum(i * t // seq, batch)

    c8 = jnp.zeros((MOD_ROWS, d), F32).at[:batch].set(c).at[batch].set(c_ctx)
    mod = _with_norm_gains(_mod_table(c8, w_ada, b_ada), norm_g)
    cos_t, sin_t = _rope_tables(seq, TM_IN)

    ffn = w_ffn_down.shape[2]
    w_up_f32 = w_ffn_up.reshape(depth * 2 * d, 2 * ffn)
    w_dn_f32 = w_ffn_down.reshape(depth * 2 * ffn, d)
    first_casts = [(w_up_f32, "cols", d, d), (w_dn_f32, "rows", ffn, ffn)]
    if depth > 1:
        first_casts += [(w_up_f32, "cols", 2 * d, (depth - 1) * 2 * d),
                        (w_dn_f32, "rows", 2 * ffn, (depth - 1) * 2 * ffn)]
    ffn_w = {(0, 0): (w_ffn_up[0, 0].astype(BF16), w_ffn_down[0, 0].astype(BF16), 0)}
    w_in_r = _reorder_w_in(w_in[0], d)
    w_in_f32 = w_in.reshape(depth * d, w_in.shape[2])
    w1 = jnp.concatenate([b_decay_w1[:, 0], b_decay_w1[:, 1],
                          jnp.zeros((depth, d, LANES - 2 * B_RANK), F32)], axis=-1).astype(BF16)
    kw = b_decay_w2.shape[-1]
    w2pad = jnp.zeros((depth, 2, LANES, kw), F32).at[:, 0, :B_RANK].set(b_decay_w2[:, 0]).at[
        :, 1, B_RANK:2 * B_RANK].set(b_decay_w2[:, 1])
    n_qh = c_sink.shape[1]
    qk_gain = jnp.concatenate([jnp.tile(c_q_gain * (HEAD_DIM ** -0.5 * LOG2E), (1, n_qh)),
                               jnp.tile(c_k_gain, (1, C_KVW // HEAD_DIM))], axis=1)
    wa, wb, wc, wo = (w.astype(BF16) for w in (w_br_a, w_br_b, w_br_c, w_out))

    srcs = [x.reshape(n_lat, d), ctx.reshape(n_ctx, d)]
    for l in range(depth):
        last = l == depth - 1
        n_mix = n_lat if last else n_all
        tm_first = TM_FFN if len(srcs) == 1 else TM_FFN // 2
        w_up, w_dn, wsel = ffn_w[(l, 0)]
        xs = _ffn(srcs, n_all, mod, l, 0, row_fn(tm_first), w_up, w_dn, wsel, tm_first, TF,
                  cast=first_casts if l == 0 else ())
        if l == 0:
            xs, *cast_out = xs
            ffn_w[(0, 1)] = (cast_out[0], cast_out[1], 0)
            for later in range(1, depth):
                for which in range(2):
                    ffn_w[(later, which)] = (cast_out[2], cast_out[3], 2 * (later - 1) + which)
        outs = _inproj(xs, mod, l, row_fn(TM_IN), w_in_r, w1, qk_gain[l:l + 1], cos_t, sin_t,
                       n_lat, seq, TM_IN, TN_IN, ctx_state_only=last,
                       cast=None if last else (w_in_f32, (l + 1) * d, d))
        z, r = outs[:2]
        if not last:
            w_in_r = _reorder_w_in(outs[2][:d], d)
        a_out = _gmlp(z, n_mix, a_v_gain[l], a_ws[l], a_bs[l], TG_GMLP)
        o_f, o_b = _gla(z, r, w2pad[l], b_decay_b[l].reshape(2, 1, kw), batch, seq, ctx_len, TB_GLA)
        c_out = _attn(z, c_sink[l], batch, seq, ctx_len, with_ctx=not last)
        xs = _merge(xs, n_mix, mod, l, row_fn(TM_MERGE), z, a_out, o_f, o_b, c_out, b_norm_g[l],
                    wa, wb, wc, wo, TM_MERGE)
        w_up, w_dn, wsel = ffn_w[(l, 1)]
        xs = _ffn([xs], n_mix, mod, l, 1, row_fn(TM_FFN), w_up, w_dn, wsel, TM_FFN, TF)
        srcs = [xs]
    return xs.reshape(batch, seq, d)
```

```python
import functools

import jax
import jax.numpy as jnp
import numpy as np
from jax import lax
from jax.experimental import pallas as pl
from jax.experimental.pallas import tpu as pltpu

F32 = jnp.float32
BF16 = jnp.bfloat16

HEAD_DIM = 128
EPS = 1e-6
N_MOD = 9
GRID_W = 64
A_CHUNK = 128
B_DK = 64
B_DV = 128
B_RANK = 16
B_TAU = 16.0
B_CHUNK = 64
C_BLOCK = 128
ROPE_BASE = 10000.0
MOD_ROWS = 8
LANES = 128
NEG_BIG = -1e30
LOG2E = 1.4426950408889634
LN2 = 0.6931471805599453
VMEM_LIMIT = 56 * 1024 * 1024

Z_CQ, Z_CK, Z_CV, Z_BK, Z_BQ, Z_BV, Z_AU, Z_AV, Z_BG, Z_GATE = (
    0, 1024, 1280, 1536, 1792, 2048, 2560, 3072, 3584, 4096)
QK_COLS = Z_CV
C_KVW = Z_CV - Z_CK

TM_FFN = 1024
TF = 512
TM_IN = 512
TN_IN = 2 * QK_COLS
TM_MERGE = 256
TG_GMLP = 512
TB_GLA = 256
GLA_BATCH_GROUP = 2


def _cparams(sem):
    return pltpu.CompilerParams(dimension_semantics=sem, vmem_limit_bytes=VMEM_LIMIT)


def _dot(a, b):
    return jnp.dot(a, b, preferred_element_type=F32)


def _dot_nt(a, b):
    return lax.dot_general(a, b, (((1,), (1,)), ((), ())), preferred_element_type=F32)


def _dot_tn(a, b):
    return lax.dot_general(a, b, (((0,), (0,)), ((), ())), preferred_element_type=F32)


def _silu(x):
    return x * jax.nn.sigmoid(x)


NORM_ROWS = 64


def _rms_mod_to(x_ref, m_ref, h_ref):
    shift = m_ref[0]
    gain = m_ref[3] * (1.0 + m_ref[1])

    def one_pass(c, carry):
        rows = pl.ds(pl.multiple_of(c * NORM_ROWS, NORM_ROWS), NORM_ROWS)
        x = x_ref[rows, :]
        rs = lax.rsqrt(jnp.mean(x * x, axis=-1, keepdims=True) + EPS)
        h_ref[rows, :] = (x * rs * gain + shift).astype(BF16)
        return carry

    lax.fori_loop(0, x_ref.shape[0] // NORM_ROWS, one_pass, 0, unroll=2)


def _rms_head(xh, gain):
    ms = jnp.mean(xh * xh, axis=-1, keepdims=True)
    return xh * lax.rsqrt(ms + EPS) * gain


def _mod_kernel(c_ref, w_ref, b_ref, o_ref):
    act = _silu(c_ref[...])
    o_ref[...] = _dot(act.astype(BF16), w_ref[...].astype(BF16)) + b_ref[...]


def _mod_table(c8, w_ada, b_ada):
    depth, d, nd = w_ada.shape
    tn = 1024
    nj = d // tn
    out = pl.pallas_call(
        _mod_kernel,
        out_shape=jax.ShapeDtypeStruct((depth, N_MOD, MOD_ROWS, d), F32),
        grid=(depth, N_MOD, nj),
        in_specs=[
            pl.BlockSpec((MOD_ROWS, d), lambda l, k, j: (0, 0)),
            pl.BlockSpec((None, d, tn), lambda l, k, j: (l, 0, k * nj + j)),
            pl.BlockSpec((None, 1, tn), lambda l, k, j: (l, 0, k * nj + j)),
        ],
        out_specs=pl.BlockSpec((None, None, MOD_ROWS, tn), lambda l, k, j: (l, k, 0, j)),
        compiler_params=_cparams(("parallel", "parallel", "parallel")),
        name="mod_table",
    )(c8, w_ada, b_ada.reshape(depth, 1, nd))
    return out


MOD_GROUP = 4


def _with_norm_gains(mod, norm_g):
    depth, _, rows, d = mod.shape
    subs = norm_g.shape[1]
    gains = jnp.broadcast_to(norm_g[:, :, None, None, :], (depth, subs, 1, rows, d))
    table = jnp.concatenate([mod.reshape(depth, subs, N_MOD // subs, rows, d), gains], axis=2)
    return table.reshape(depth, subs * MOD_GROUP, rows, 1, d)


def _mod_spec(layer, sub, row_fn, d):
    return pl.BlockSpec((None, MOD_GROUP, None, 1, d), lambda i, *_: (layer, sub, row_fn(i), 0, 0))


def _ffn_kernel(*refs, tiles_per_source, n_cast):
    n_src = len(tiles_per_source)
    x_refs = refs[:n_src]
    m_ref, wg_ref, wv_ref, wd_ref = refs[n_src:n_src + 4]
    cast_in_refs = refs[n_src + 4:n_src + 4 + n_cast]
    o_ref = refs[n_src + 4 + n_cast]
    cast_out_refs = refs[n_src + 5 + n_cast:n_src + 5 + 2 * n_cast]
    h_ref = refs[n_src + 5 + 2 * n_cast]
    acc_ref = o_ref
    i, f = pl.program_id(0), pl.program_id(1)

    for cast_in_ref, cast_out_ref in zip(cast_in_refs, cast_out_refs):
        cast_out_ref[...] = cast_in_ref[...].astype(BF16)

    def with_own_source(fn):
        start = 0
        for x_ref, n_tiles in zip(x_refs, tiles_per_source):
            if n_src == 1:
                fn(x_ref)
            else:
                pl.when(jnp.logical_and(i >= start, i < start + n_tiles))(lambda x_ref=x_ref: fn(x_ref))
            start += n_tiles

    last = pl.num_programs(1) - 1

    def prologue(x_ref):
        _rms_mod_to(x_ref, m_ref, h_ref)

    def chunk():
        h = h_ref[...]
        act = (_silu(_dot(h, wg_ref[...])) * _dot(h, wv_ref[...])).astype(BF16)
        return _dot(act, wd_ref[...])

    @pl.when(f == 0)
    def _():
        with_own_source(prologue)
        acc_ref[...] = chunk()

    if n_src == 1:
        @pl.when(jnp.logical_and(f > 0, f < last))
        def _():
            acc_ref[...] += chunk()

        @pl.when(f == last)
        def _():
            o_ref[...] = x_refs[0][...] + 0.5 * m_ref[2] * (acc_ref[...] + chunk())
    else:
        @pl.when(f > 0)
        def _():
            acc_ref[...] += chunk()

        def epilogue(x_ref):
            o_ref[...] = x_ref[...] + 0.5 * m_ref[2] * acc_ref[...]

        @pl.when(f == last)
        def _():
            with_own_source(epilogue)


def _ffn(xs, n_rows, mod, layer, which, row_fn, w_up, w_down, wsel, tm, tf, cast=()):
    d = xs[0].shape[1]
    nf = w_up.shape[1] // (2 * tf)
    nt = n_rows // tm
    tiles, starts = [], []
    for x in xs:
        starts.append(sum(tiles))
        tiles.append(min(x.shape[0] // tm, nt - sum(tiles)))

    def src_spec(start, n_tiles):
        mode = pl.Buffered(1) if (n_tiles == 1 and len(xs) > 1) else None
        return pl.BlockSpec((tm, d), lambda i, f: (jnp.clip(i - start, 0, n_tiles - 1), 0), pipeline_mode=mode)

    in_specs = [src_spec(s, t) for s, t in zip(starts, tiles)] + [
        _mod_spec(layer, 2 * which, row_fn, d),
        pl.BlockSpec((d, tf), lambda i, f: (wsel, f)),
        pl.BlockSpec((d, tf), lambda i, f: (wsel, nf + f)),
        pl.BlockSpec((tf, d), lambda i, f: (wsel * nf + f, 0)),
    ]
    out_shape = [jax.ShapeDtypeStruct((n_rows, d), F32)]
    out_specs = [pl.BlockSpec((tm, d), lambda i, f: (i, 0))]
    operands = [*xs, mod, w_up, w_up, w_down]
    row_blocks = 1 << (nt.bit_length() - 1)
    cast_in_specs = []
    for src, split, row0, rows in cast:
        cols = src.shape[1]
        if split == "cols":
            br, bc = rows // row_blocks, cols // nf
            first = row0 // br
            in_blk = lambda i, f, first=first: (first + jnp.minimum(i, row_blocks - 1), f)
            out_blk = lambda i, f: (i, f)
        else:
            br, bc = rows // (row_blocks * nf), cols
            first = row0 // br
            in_blk = lambda i, f, first=first: (first + jnp.minimum(i, row_blocks - 1) * nf + f, 0)
            out_blk = lambda i, f: (i * nf + f, 0)
        assert br % 16 == 0 and bc % LANES == 0 and rows % br == 0 and cols % bc == 0 and row0 % br == 0
        cast_in_specs.append(pl.BlockSpec((br, bc), in_blk))
        out_shape.append(jax.ShapeDtypeStruct((rows // row_blocks * nt, cols), BF16))
        out_specs.append(pl.BlockSpec((br, bc), out_blk))
        operands.append(src)

    outs = pl.pallas_call(
        functools.partial(_ffn_kernel, tiles_per_source=tuple(tiles), n_cast=len(cast)),
        out_shape=out_shape,
        grid=(nt, nf),
        in_specs=in_specs + cast_in_specs,
        out_specs=out_specs,
        scratch_shapes=[pltpu.VMEM((tm, d), BF16)],
        compiler_params=_cparams(("parallel", "arbitrary")),
        name="ffn",
    )(*operands)
    return outs[0] if not cast else tuple(outs)


def _inproj_kernel(x_ref, m_ref, w_ref, w1_ref, qkg_ref, cos_ref, sin_ref, z_ref, r_ref, h_ref, *,
                   full_tiles):
    i, n = pl.program_id(0), pl.program_id(1)

    @pl.when(jnp.logical_and(n != 0, i < full_tiles))
    def _():
        z_ref[...] = _dot(h_ref[...], w_ref[...]).astype(z_ref.dtype)

    @pl.when(jnp.logical_and(n != 0, i >= full_tiles))
    def _():
        z_ref[...] = jnp.zeros_like(z_ref)

    @pl.when(n == 0)
    def _():
        _rms_mod_to(x_ref, m_ref, h_ref)
        h = h_ref[...]
        r_ref[...] = _dot(h, w1_ref[...])
        zt = _dot(h, w_ref[:, :QK_COLS])
        if w_ref.shape[1] > QK_COLS:
            z_ref[:, QK_COLS:] = _dot(h, w_ref[:, QK_COLS:]).astype(z_ref.dtype)
        pair = 2 * HEAD_DIM
        r = lax.broadcasted_iota(jnp.int32, (pair, pair), 0)
        c = lax.broadcasted_iota(jnp.int32, (pair, pair), 1)
        head_sum = jnp.where(jnp.bitwise_and(jnp.bitwise_xor(r, c), HEAD_DIM) == 0, 1.0, 0.0).astype(BF16)
        swap = jnp.where(c == jnp.bitwise_xor(r, HEAD_DIM // 4), 1.0, 0.0).astype(BF16)
        cos = jnp.concatenate([cos_ref[...]] * 2, axis=1)
        sin = jnp.concatenate([sin_ref[...]] * 2, axis=1)
        for p in range(QK_COLS // pair):
            cols = slice(p * pair, (p + 1) * pair)
            y = zt[:, cols]
            ms = _dot((y * y).astype(BF16), head_sum) * (1.0 / HEAD_DIM)
            y = y * lax.rsqrt(ms + EPS) * qkg_ref[:, cols]
            z_ref[:, cols] = (y * cos + _dot(y.astype(BF16), swap) * sin).astype(z_ref.dtype)


def _inproj(x, mod, layer, row_fn, w_in, w1, qk_gain, cos_t, sin_t, n_lat, seq, tm, tn, ctx_state_only):
    n_rows, d = x.shape
    ncols = w_in.shape[2]
    assert tn % QK_COLS == 0 and ncols % tn == 0 and Z_BV + 512 <= tn
    rope_blocks = seq // tm
    nt = n_rows // tm
    full_tiles = n_lat // tm if ctx_state_only else nt

    def rope_blk(i, n):
        return jnp.where(i * tm < n_lat, (i % rope_blocks), rope_blocks)

    def w_blk(i, n):
        return jnp.where(i < full_tiles, n, 0)

    return pl.pallas_call(
        functools.partial(_inproj_kernel, full_tiles=full_tiles),
        out_shape=(jax.ShapeDtypeStruct((n_rows, ncols), BF16),
                   jax.ShapeDtypeStruct((n_rows, LANES), F32)),
        grid=(n_rows // tm, ncols // tn),
        in_specs=[
            pl.BlockSpec((tm, d), lambda i, n: (i, 0)),
            _mod_spec(layer, 1, row_fn, d),
            pl.BlockSpec((None, d, tn), lambda i, n: (layer, 0, w_blk(i, n))),
            pl.BlockSpec((None, d, LANES), lambda i, n: (layer, 0, 0)),
            pl.BlockSpec((1, QK_COLS), lambda i, n: (0, 0)),
            pl.BlockSpec((tm, HEAD_DIM), lambda i, n: (rope_blk(i, n), 0)),
            pl.BlockSpec((tm, HEAD_DIM), lambda i, n: (rope_blk(i, n), 0)),
        ],
        out_specs=(pl.BlockSpec((tm, tn), lambda i, n: (i, n)),
                   pl.BlockSpec((tm, LANES), lambda i, n: (i, 0))),
        scratch_shapes=[pltpu.VMEM((tm, d), BF16)],
        compiler_params=_cparams(("parallel", "arbitrary")),
        name="inproj",
    )(x, mod, w_in, w1, qk_gain, cos_t, sin_t)


def _gmlp_kernel(u_ref, v_ref, gain_ref, ws_ref, bs_ref, o_ref):
    tg, width = u_ref.shape
    groups = width // HEAD_DIM
    u = jax.nn.gelu(u_ref[...].astype(F32))
    v = jax.nn.gelu(v_ref[...].astype(F32))
    mu = jnp.mean(v, axis=-1, keepdims=True)
    vc = v - mu
    var = jnp.mean(vc * vc, axis=-1, keepdims=True)
    vn = (vc * lax.rsqrt(var + EPS) * gain_ref[...]).astype(BF16)
    nchunk = tg // A_CHUNK
    for g in range(groups):
        cols = slice(g * HEAD_DIM, (g + 1) * HEAD_DIM)
        v_g = jnp.concatenate([vn[c * A_CHUNK:(c + 1) * A_CHUNK, cols] for c in range(nchunk)], axis=1)
        mixed = _dot(ws_ref[g], v_g)
        for c in range(nchunk):
            rows = slice(c * A_CHUNK, (c + 1) * A_CHUNK)
            o_ref[rows, cols] = (u[rows, cols] * (mixed[:, c * HEAD_DIM:(c + 1) * HEAD_DIM] + bs_ref[g])
                                 ).astype(o_ref.dtype)


def _gmlp(z, n_rows, a_v_gain, a_ws, a_bs, tg):
    groups = a_ws.shape[0]
    width = groups * HEAD_DIM
    bs_full = jnp.broadcast_to(a_bs[:, :, None], (groups, A_CHUNK, HEAD_DIM)).astype(F32)
    return pl.pallas_call(
        _gmlp_kernel,
        out_shape=jax.ShapeDtypeStruct((n_rows, width), BF16),
        grid=(n_rows // tg,),
        in_specs=[
            pl.BlockSpec((tg, width), lambda i: (i, Z_AU // width)),
            pl.BlockSpec((tg, width), lambda i: (i, Z_AV // width)),
            pl.BlockSpec((1, width), lambda i: (0, 0)),
            pl.BlockSpec((groups, A_CHUNK, A_CHUNK), lambda i: (0, 0, 0)),
            pl.BlockSpec((groups, A_CHUNK, HEAD_DIM), lambda i: (0, 0, 0)),
        ],
        out_specs=pl.BlockSpec((tg, width), lambda i: (i, 0)),
        compiler_params=_cparams(("parallel",)),
        name="gmlp",
    )(z, z, a_v_gain.reshape(1, width), a_ws.astype(BF16), bs_full)


def _gla_prep(k_ref, q_ref, r_ref, w2, bias, tri_ref, reverse):
    tb = k_ref.shape[0]
    heads = k_ref.shape[1] // B_DK
    nchunk = tb // B_CHUNK
    logit = _dot(r_ref[...].astype(BF16), w2) + bias
    soft = jnp.log2(1.0 + jnp.exp2(jnp.abs(logit) * -LOG2E))
    g = jnp.minimum(logit, 0.0) * (1.0 / B_TAU) - soft * (LN2 / B_TAU)
    g_hi = g.astype(BF16)
    g_lo = (g - g_hi.astype(F32)).astype(BF16)
    tri = tri_ref[...]
    bc = _dot(tri, g_hi) + _dot(tri, g_lo)
    last = [c * B_CHUNK + (0 if reverse else B_CHUNK - 1) for c in range(nchunk)]
    tots = [bc[r:r + 1, :] for r in last]
    tot_rows = jnp.concatenate([jnp.broadcast_to(t, (B_CHUNK, t.shape[1])) for t in tots], axis=0)
    kf = k_ref[...].astype(F32)
    q_in = (q_ref[...].astype(F32) * jnp.exp(bc) * (B_DK ** -0.5)).astype(BF16)
    k_in = kf * jnp.exp(-bc)
    k_out = kf * jnp.exp(tot_rows - bc)
    dec = [jnp.exp(t) for t in tots]
    lane = lax.broadcasted_iota(jnp.int32, (tb, LANES), 1)
    zeros = jnp.zeros((B_CHUNK, LANES), BF16)
    q_slabs, k_in_heads, k4_heads = [], [], []
    for h in range(heads):
        slab = slice((h // 2) * LANES, (h // 2 + 1) * LANES)
        mine = (lane >= B_DK) if h % 2 else (lane < B_DK)
        q_slabs.append(q_in[:, slab])
        k_in_heads.append(jnp.where(mine, k_in[:, slab], 0.0).astype(BF16))
        k_out_h = jnp.where(mine, k_out[:, slab], 0.0).astype(BF16)
        k4_heads.append(jnp.concatenate([
            jnp.concatenate([zeros] * c + [k_out_h[c * B_CHUNK:(c + 1) * B_CHUNK]] + [zeros] * (nchunk - 1 - c),
                            axis=0) for c in range(nchunk)], axis=1))
    return q_slabs, k_in_heads, k4_heads, dec


def _gla_kernel(*refs, group, heads):
    ins = refs[:8 * group]
    w2_ref, b_ref, tri_ref, keep_ref = refs[8 * group:8 * group + 4]
    outs = refs[8 * group + 4:8 * group + 6]
    st_ref, a_ref, u_ref = refs[8 * group + 6:]

    @pl.when(pl.program_id(1) == 0)
    def _():
        st_ref[...] = jnp.zeros_like(st_ref)

    nchunk = ins[0].shape[0] // B_CHUNK
    streams = []
    for g in range(group):
        for d in range(2):
            k_ref, v_ref, q_ref, r_ref = ins[8 * g + 4 * d:8 * g + 4 * d + 4]
            streams.append((k_ref, v_ref, q_ref, r_ref, outs[d].at[g], d))
    prep = [_gla_prep(k_ref, q_ref, r_ref, w2_ref[d], b_ref[d], tri_ref.at[d], bool(d))
            for k_ref, _, q_ref, r_ref, _, d in streams]
    jobs = [(s, h) for s in range(len(streams)) for h in range(heads)]
    for s, h in jobs:
        q_slabs, k_in_heads, _, _ = prep[s]
        a_ref[s * heads + h] = _dot_nt(q_slabs[h], k_in_heads[h])
    for s, h in jobs:
        v_h = streams[s][1][:, h * B_DV:(h + 1) * B_DV]
        u_ref[s * heads + h] = _dot_tn(v_h, prep[s][2][h])
    for s, h in jobs:
        v_ref, o_ref, d = streams[s][1], streams[s][4], streams[s][5]
        vcols = slice(h * B_DV, (h + 1) * B_DV)
        attn = jnp.where(keep_ref[d] > 0.5, a_ref[s * heads + h], 0.0).astype(BF16)
        o_ref[:, vcols] = _dot(attn, v_ref[:, vcols])
    for s, h in jobs:
        o_ref, d = streams[s][4], streams[s][5]
        q_s, dec = prep[s][0][h], prep[s][3]
        slab = slice((h // 2) * LANES, (h // 2 + 1) * LANES)
        vcols = slice(h * B_DV, (h + 1) * B_DV)
        st = st_ref[s * heads + h]
        for c in (range(nchunk - 1, -1, -1) if d else range(nchunk)):
            rows = slice(c * B_CHUNK, (c + 1) * B_CHUNK)
            o_ref[rows, vcols] += _dot_nt(q_s[rows], st.astype(BF16))
            st = st * dec[c][:, slab] + u_ref[s * heads + h, :, c * LANES:(c + 1) * LANES]
        st_ref[s * heads + h] = st


def _chunk_triangles(tb):
    t = np.arange(tb)[:, None]
    s = np.arange(tb)[None, :]
    same = (t // B_CHUNK) == (s // B_CHUNK)
    return np.stack([same & (s <= t), same & (s >= t)]).astype(np.float32)


def _gla(z, r, w2pad, bias, batch, seq, ctx_len, tb):
    n_rows = z.shape[0]
    heads = bias.shape[-1] // B_DK
    assert heads % 2 == 0 and 2 * B_DK == LANES
    kw, vw = heads * B_DK, heads * B_DV
    nlat, nctx = seq // tb, ctx_len // tb
    base = batch * nlat
    tri = _chunk_triangles(tb)

    def fwd(b, j):
        return jnp.where(j < nctx, base + b * nctx + j, b * nlat + (j - nctx))

    def bwd(b, j):
        return jnp.where(j < nctx, base + b * nctx + (nctx - 1 - j), b * nlat + (nlat - 1 - (j - nctx)))

    group = GLA_BATCH_GROUP if batch % GLA_BATCH_GROUP == 0 else 1

    def specs(blk, g):
        def at(p, j):
            return blk(p * group + g, j)
        return [pl.BlockSpec((tb, kw), lambda p, j: (at(p, j), Z_BK // kw)),
                pl.BlockSpec((tb, vw), lambda p, j: (at(p, j), Z_BV // vw)),
                pl.BlockSpec((tb, kw), lambda p, j: (at(p, j), Z_BQ // kw)),
                pl.BlockSpec((tb, LANES), lambda p, j: (at(p, j), 0))]

    def out_fwd(p, j):
        return (p, jnp.where(j < nctx, nlat + j, j - nctx), 0)

    def out_bwd(p, j):
        return (p, jnp.where(j < nctx, nlat + (nctx - 1 - j), nlat - 1 - (j - nctx)), 0)

    in_specs, operands = [], []
    for g in range(group):
        in_specs += specs(fwd, g) + specs(bwd, g)
        operands += [z, z, z, r, z, z, z, r]
    out_specs = [pl.BlockSpec((group, tb, vw), out_fwd), pl.BlockSpec((group, tb, vw), out_bwd)]
    in_specs += [pl.BlockSpec((2, LANES, kw), lambda p, j: (0, 0, 0)),
                 pl.BlockSpec((2, 1, kw), lambda p, j: (0, 0, 0)),
                 pl.BlockSpec((2, tb, tb), lambda p, j: (0, 0, 0)),
                 pl.BlockSpec((2, tb, tb), lambda p, j: (0, 0, 0))]
    operands += [w2pad.astype(BF16), bias, jnp.asarray(tri, BF16), jnp.asarray(tri, F32)]
    streams = 2 * group
    outs = pl.pallas_call(
        functools.partial(_gla_kernel, group=group, heads=heads),
        out_shape=[jax.ShapeDtypeStruct((batch, seq + ctx_len, vw), F32)] * 2,
        grid=(batch // group, nctx + nlat),
        in_specs=in_specs,
        out_specs=out_specs,
        scratch_shapes=[pltpu.VMEM((streams * heads, B_DV, LANES), F32),
                        pltpu.VMEM((streams * heads, tb, tb), F32),
                        pltpu.VMEM((streams * heads, B_DV, (tb // B_CHUNK) * LANES), F32)],
        compiler_params=_cparams(("parallel", "arbitrary")),
        name="gla",
    )(*operands)
    return outs


def _attn_kernel(sink_ref, q_ref, kp_ref, kc_ref, kn_ref, kx_ref, vp_ref, vc_ref, vn_ref, vx_ref,
                 bias_ref, o_ref, s_ref):
    tq = q_ref.shape[0]
    n_heads = q_ref.shape[1] // HEAD_DIM
    kv_heads = kc_ref.shape[1] // HEAD_DIM
    group = n_heads // kv_heads
    n_loc = bias_ref.shape[1]

    def kv_cols(h):
        kh = h // group
        return slice(kh * HEAD_DIM, (kh + 1) * HEAD_DIM)

    def scores(kh):
        cols = kv_cols(kh * group)
        k_all = jnp.concatenate([kp_ref[:, cols], kc_ref[:, cols], kn_ref[:, cols], kx_ref[:, cols]], axis=0)
        q = jnp.concatenate([q_ref[:, h * HEAD_DIM:(h + 1) * HEAD_DIM]
                             for h in range(kh * group, (kh + 1) * group)], axis=0)
        s_ref[kh * group * tq:(kh + 1) * group * tq, :] = _dot_nt(q, k_all)

    def softmax_pv(h):
        cols = kv_cols(h)
        rows = slice(h * tq, (h + 1) * tq)
        v_all = jnp.concatenate([vp_ref[:, cols], vc_ref[:, cols], vn_ref[:, cols], vx_ref[:, cols]], axis=0)
        sink = sink_ref[h] * LOG2E
        sg = jnp.concatenate([s_ref[rows, :n_loc] + bias_ref[...], s_ref[rows, n_loc:]], axis=1)
        m = jnp.maximum(jnp.max(sg, axis=-1, keepdims=True), sink)
        e = jnp.exp2(sg - m)
        den = jnp.sum(e, axis=-1, keepdims=True) + jnp.exp2(sink - m)
        o_ref[:, h * HEAD_DIM:(h + 1) * HEAD_DIM] = (_dot(e.astype(BF16), v_all) / den).astype(o_ref.dtype)

    for kh in range(kv_heads):
        scores(kh)
    for h in range(n_heads):
        softmax_pv(h)


def _window_bias(tq):
    i = np.arange(tq)[:, None]
    j = np.arange(3 * tq)[None, :]
    band = (j >= i) & (j <= i + 2 * tq)
    cases = [band & (j >= tq), band, band & (j < 2 * tq), np.zeros_like(band)]
    return jnp.asarray(np.where(np.stack(cases), 0.0, NEG_BIG), F32)


def _attn(z, sink, batch, seq, ctx_len, with_ctx):
    n_heads = sink.shape[0]
    qw = n_heads * HEAD_DIM
    kvw = C_KVW
    tq = C_BLOCK
    nb = seq // tq
    cpb = ctx_len // tq if with_ctx else 0
    assert nb >= 2 and ctx_len % tq == 0
    ctx_base = batch * seq // ctx_len

    def own(b, n):
        return jnp.where(n < nb, b * nb + n, batch * nb + b * cpb + (n - nb))

    def prev(b, n):
        return b * nb + jnp.clip(n - 1, 0, nb - 1)

    def cur(b, n):
        return b * nb + jnp.minimum(n, nb - 1)

    def nxt(b, n):
        return b * nb + jnp.minimum(n + 1, nb - 1)

    def zspec(blk, col0):
        return pl.BlockSpec((tq, kvw), lambda b, n: (blk(b, n), col0 // kvw))

    def bias_case(b, n):
        return jnp.where(n >= nb, 3, jnp.where(n == 0, 0, jnp.where(n == nb - 1, 2, 1)))

    return pl.pallas_call(
        _attn_kernel,
        out_shape=jax.ShapeDtypeStruct((batch * (nb + cpb) * tq, qw), BF16),
        grid=(batch, nb + cpb),
        in_specs=[
            pl.BlockSpec(memory_space=pltpu.SMEM),
            pl.BlockSpec((tq, qw), lambda b, n: (own(b, n), Z_CQ // qw)),
            zspec(prev, Z_CK), zspec(cur, Z_CK), zspec(nxt, Z_CK),
            pl.BlockSpec((ctx_len, kvw), lambda b, n: (ctx_base + b, Z_CK // kvw)),
            zspec(prev, Z_CV), zspec(cur, Z_CV), zspec(nxt, Z_CV),
            pl.BlockSpec((ctx_len, kvw), lambda b, n: (ctx_base + b, Z_CV // kvw)),
            pl.BlockSpec((None, tq, 3 * tq), lambda b, n: (bias_case(b, n), 0, 0)),
        ],
        out_specs=pl.BlockSpec((tq, qw), lambda b, n: (own(b, n), 0)),
        scratch_shapes=[pltpu.VMEM((n_heads * tq, 3 * tq + ctx_len), F32)],
        compiler_params=_cparams(("parallel", "parallel")),
        name="window_attn",
    )(sink, z, z, z, z, z, z, z, z, z, _window_bias(tq))


MERGE_CHUNKS = 4


def _merge_kernel(x_ref, m_ref, a_ref, of_ref, ob_ref, og_ref, c_ref, ga_ref, gb_ref, gc_ref,
                  bg_ref, wa_ref, wb_ref, wc_ref, wo_ref, o_ref):
    heads = of_ref.shape[1] // B_DV
    o = of_ref[...] + ob_ref[...]
    bn = jnp.concatenate([_rms_head(o[:, h * B_DV:(h + 1) * B_DV], bg_ref[...]) for h in range(heads)], axis=1)
    b = (bn * _silu(og_ref[...].astype(F32))).astype(BF16)
    a, c = a_ref[...], c_ref[...]
    d = o_ref.shape[1]
    cw = d // MERGE_CHUNKS

    def merged_cols(k):
        cols = slice(k * cw, (k + 1) * cw)
        return (jax.nn.sigmoid(ga_ref[:, cols].astype(F32)) * _dot(a, wa_ref[:, cols])
                + jax.nn.sigmoid(gb_ref[:, cols].astype(F32)) * _dot(b, wb_ref[:, cols])
                + jax.nn.sigmoid(gc_ref[:, cols].astype(F32)) * _dot(c, wc_ref[:, cols])).astype(BF16)

    pending = merged_cols(0)
    mix = None
    for k in range(MERGE_CHUNKS):
        ready = pending
        if k + 1 < MERGE_CHUNKS:
            pending = merged_cols(k + 1)
        part = _dot(ready, wo_ref[k * cw:(k + 1) * cw, :])
        mix = part if mix is None else mix + part
    o_ref[...] = x_ref[...] + m_ref[2] * mix


def _merge(x, n_rows, mod, layer, row_fn, z, a, o_f, o_b, c, b_norm_g, wa, wb, wc, wo, tm, seq, ctx_len):
    d = x.shape[1]
    aw, bw, cw = a.shape[1], o_f.shape[2], c.shape[1]
    batch = o_f.shape[0]
    lat_tiles, ctx_tiles = seq // tm, ctx_len // tm
    assert seq % tm == 0 and ctx_len % tm == 0

    def gla_blk(i):
        j = i - batch * lat_tiles
        is_lat = i < batch * lat_tiles
        return (jnp.where(is_lat, i // lat_tiles, j // ctx_tiles),
                jnp.where(is_lat, i % lat_tiles, lat_tiles + j % ctx_tiles), 0)

    def const(rows):
        return pl.BlockSpec((None, rows, d), lambda i: (layer, 0, 0), pipeline_mode=pl.Buffered(1))

    return pl.pallas_call(
        _merge_kernel,
        out_shape=jax.ShapeDtypeStruct((n_rows, d), F32),
        grid=(n_rows // tm,),
        in_specs=[
            pl.BlockSpec((tm, d), lambda i: (i, 0)),
            _mod_spec(layer, 1, row_fn, d),
            pl.BlockSpec((tm, aw), lambda i: (i, 0)),
            pl.BlockSpec((None, tm, bw), gla_blk),
            pl.BlockSpec((None, tm, bw), gla_blk),
            pl.BlockSpec((tm, bw), lambda i: (i, Z_BG // bw)),
            pl.BlockSpec((tm, cw), lambda i: (i, 0)),
            pl.BlockSpec((tm, d), lambda i: (i, Z_GATE // d)),
            pl.BlockSpec((tm, d), lambda i: (i, Z_GATE // d + 1)),
            pl.BlockSpec((tm, d), lambda i: (i, Z_GATE // d + 2)),
            pl.BlockSpec((1, B_DV), lambda i: (0, 0)),
            const(aw), const(bw), const(cw), const(d),
        ],
        out_specs=pl.BlockSpec((tm, d), lambda i: (i, 0)),
        compiler_params=_cparams(("parallel",)),
        name="merge",
    )(x, mod, a, o_f, o_b, z, c, z, z, z, b_norm_g.reshape(1, B_DV), wa, wb, wc, wo)


def _rope_tables(seq, pad_rows):
    half = HEAD_DIM // 4
    t = np.arange(seq)
    pos = np.stack([t // GRID_W, t % GRID_W], axis=1).astype(np.float32)
    inv_freq = jnp.asarray(ROPE_BASE, F32) ** (-jnp.arange(half, dtype=F32) / half)
    ang = jnp.asarray(pos)[:, :, None] * inv_freq[None, None, :]
    cos = jnp.cos(ang)
    sin = jnp.sin(ang)
    cos_t = jnp.concatenate([cos, cos], axis=-1).reshape(seq, HEAD_DIM)
    sin_t = jnp.concatenate([-sin, sin], axis=-1).reshape(seq, HEAD_DIM)
    cos_t = jnp.concatenate([cos_t, jnp.ones((pad_rows, HEAD_DIM), F32)], axis=0)
    sin_t = jnp.concatenate([sin_t, jnp.zeros((pad_rows, HEAD_DIM), F32)], axis=0)
    return cos_t, sin_t


def _reorder_w_in(w, d):
    bkw, bvw, ckvw, aw, cqw = 256, 512, 256, 512, 1024
    o_bk = 0
    o_bv = o_bk + bkw
    o_ck = o_bv + bvw
    o_cv = o_ck + ckvw
    o_au = o_cv + ckvw
    o_av = o_au + aw
    o_bq = o_av + aw
    o_bg = o_bq + bkw
    o_cq = o_bg + bvw
    o_gate = o_cq + cqw
    seg = lambda s, width: w[..., s:s + width].astype(BF16)
    return jnp.concatenate([seg(o_cq, cqw), seg(o_ck, ckvw), seg(o_cv, ckvw), seg(o_bk, bkw), seg(o_bq, bkw), seg(o_bv, bvw),
                            seg(o_au, aw), seg(o_av, aw), seg(o_bg, bvw), seg(o_gate, 3 * d)], axis=-1)


def kernel(x, c, ctx, c_ctx, w_ada, b_ada, norm_g, w_ffn_up, w_ffn_down, w_in, a_v_gain, a_ws, a_bs,
           b_decay_w1, b_decay_w2, b_decay_b, b_norm_g, c_q_gain, c_k_gain, c_sink,
           w_br_a, w_br_b, w_br_c, w_out):
    batch, seq, d = x.shape
    ctx_len = ctx.shape[1]
    depth = w_ada.shape[0]
    assert d == 2048 and batch < MOD_ROWS and seq % 256 == 0 and ctx_len == 256
    n_lat, n_ctx = batch * seq, batch * ctx_len
    n_all = n_lat + n_ctx
    for t in (TM_FFN, TM_IN, TM_MERGE, TG_GMLP, TB_GLA):
        assert seq % t == 0 and n_ctx % t == 0

    def row_fn(t):
        return lambda i: jnp.minimum(i * t // seq, batch)

    c8 = jnp.zeros((MOD_ROWS, d), F32).at[:batch].set(c).at[batch].set(c_ctx)
    mod = _with_norm_gains(_mod_table(c8, w_ada, b_ada), norm_g)
    cos_t, sin_t = _rope_tables(seq, TM_IN)

    ffn = w_ffn_down.shape[2]
    w_up_f32 = w_ffn_up.reshape(depth * 2 * d, 2 * ffn)
    w_dn_f32 = w_ffn_down.reshape(depth * 2 * ffn, d)
    first_casts = [(w_up_f32, "cols", d, d), (w_dn_f32, "rows", ffn, ffn)]
    if depth > 1:
        first_casts += [(w_up_f32, "cols", 2 * d, (depth - 1) * 2 * d),
                        (w_dn_f32, "rows", 2 * ffn, (depth - 1) * 2 * ffn)]
    ffn_w = {(0, 0): (w_ffn_up[0, 0].astype(BF16), w_ffn_down[0, 0].astype(BF16), 0)}
    w_in_r = _reorder_w_in(w_in, d)
    w1 = jnp.concatenate([b_decay_w1[:, 0], b_decay_w1[:, 1],
                          jnp.zeros((depth, d, LANES - 2 * B_RANK), F32)], axis=-1).astype(BF16)
    kw = b_decay_w2.shape[-1]
    w2pad = jnp.zeros((depth, 2, LANES, kw), F32).at[:, 0, :B_RANK].set(b_decay_w2[:, 0]).at[
        :, 1, B_RANK:2 * B_RANK].set(b_decay_w2[:, 1])
    n_qh = c_sink.shape[1]
    qk_gain = jnp.concatenate([jnp.tile(c_q_gain * (HEAD_DIM ** -0.5 * LOG2E), (1, n_qh)),
                               jnp.tile(c_k_gain, (1, C_KVW // HEAD_DIM))], axis=1)
    wa, wb, wc, wo = (w.astype(BF16) for w in (w_br_a, w_br_b, w_br_c, w_out))

    srcs = [x.reshape(n_lat, d), ctx.reshape(n_ctx, d)]
    for l in range(depth):
        last = l == depth - 1
        n_mix = n_lat if last else n_all
        tm_first = TM_FFN if len(srcs) == 1 else TM_FFN // 2
        w_up, w_dn, wsel = ffn_w[(l, 0)]
        xs = _ffn(srcs, n_all, mod, l, 0, row_fn(tm_first), w_up, w_dn, wsel, tm_first, TF,
                  cast=first_casts if l == 0 else ())
        if l == 0:
            xs, *cast_out = xs
            ffn_w[(0, 1)] = (cast_out[0], cast_out[1], 0)
            for later in range(1, depth):
                for which in range(2):
                    ffn_w[(later, which)] = (cast_out[2], cast_out[3], 2 * (later - 1) + which)
        z, r = _inproj(xs, mod, l, row_fn(TM_IN), w_in_r, w1, qk_gain[l:l + 1], cos_t, sin_t,
                       n_lat, seq, TM_IN, TN_IN, ctx_state_only=last)
        a_out = _gmlp(z, n_mix, a_v_gain[l], a_ws[l], a_bs[l], TG_GMLP)
        o_f, o_b = _gla(z, r, w2pad[l], b_decay_b[l].reshape(2, 1, kw), batch, seq, ctx_len, TB_GLA)
        c_out = _attn(z, c_sink[l], batch, seq, ctx_len, with_ctx=not last)
        xs = _merge(xs, n_mix, mod, l, row_fn(TM_MERGE), z, a_out, o_f, o_b, c_out, b_norm_g[l],
                    wa, wb, wc, wo, TM_MERGE, seq, ctx_len)
        w_up, w_dn, wsel = ffn_w[(l, 1)]
        xs = _ffn([xs], n_mix, mod, l, 1, row_fn(TM_FFN), w_up, w_dn, wsel, TM_FFN, TF)
        srcs = [xs]
    return xs.reshape(batch, seq, d)
```
